```python
import math
import jax
import jax.numpy as jnp
from jax import lax
import numpy as np

D_MODEL = 1024
BATCH = 4
SEQ = 4096
DEPTH = 2
DEC_BATCH = 32
DEC_SEQ = 1
PAST_LEN = 16384
PAGE_SIZE = 128

F32 = jnp.float32
HEAD_DIM = 64
ROPE_DIM = HEAD_DIM // 4
ROPE_THETA = 500000.0
SCALE = HEAD_DIM ** -0.5
EPS = 1e-6
Q_BLOCK = 128
NEG = -1e30

NSA_HEADS = 4
NSA_CMP_LEN = 32
NSA_CMP_STRIDE = 16
NSA_CMP_HIDDEN = 2 * HEAD_DIM
NSA_SEL_BLOCK = 64
NSA_SEL_TOPK = 16
NSA_WINDOW = 512

MOBA_HEADS = 4
MOBA_BLOCK = 256
MOBA_TOPK = 3

FOX_HEADS = 4

SSD_HEADS = 8
SSD_HEAD_DIM = 64
SSD_INNER = SSD_HEADS * SSD_HEAD_DIM
SSD_STATE = 64
SSD_GROUPS = 2
SSD_CONV = 4
SSD_CHUNK = 128
SSD_BC = SSD_GROUPS * SSD_STATE
SSD_CONV_DIM = SSD_INNER + 2 * SSD_BC

MOE_GROUPS = 4
MOE_PER_GROUP = 4
MOE_EXPERTS = MOE_GROUPS * MOE_PER_GROUP
MOE_TOPK = 2
MOE_HIDDEN = 512

N_BRANCH = 4
NSA_WIDTH = NSA_HEADS * HEAD_DIM
MOBA_WIDTH = MOBA_HEADS * HEAD_DIM
FOX_WIDTH = FOX_HEADS * HEAD_DIM
IN_SIZES = (NSA_WIDTH, 6 * HEAD_DIM, 3 * NSA_HEADS, 3 * MOBA_WIDTH, 3 * FOX_WIDTH, FOX_HEADS,
            2 * SSD_INNER + 2 * SSD_BC + SSD_HEADS, N_BRANCH * D_MODEL)
N_IN = sum(IN_SIZES)
IN_SPLITS = tuple(int(s) for s in np.cumsum(IN_SIZES)[:-1])

kernel_name = 'hybrid_nsa_moba_fox_ssd_hmoe_step'


def rms_norm(x, g):
    xf = x.astype(F32)
    y = xf * lax.rsqrt(jnp.mean(xf * xf, axis=-1, keepdims=True) + EPS)
    return (y * g.astype(F32)).astype(x.dtype)


def rope(x, pos):
    half = ROPE_DIM // 2
    inv = jnp.power(ROPE_THETA, -jnp.arange(half, dtype=F32) / half)
    ang = pos.astype(F32)[:, None] * inv[None, :]
    ang = ang.reshape((ang.shape[0],) + (1,) * (x.ndim - 3) + (half,))
    cos, sin = jnp.cos(ang), jnp.sin(ang)
    xr = x[..., :ROPE_DIM].astype(F32)
    x1, x2 = xr[..., :half], xr[..., half:]
    rot = jnp.concatenate([x1 * cos - x2 * sin, x2 * cos + x1 * sin], axis=-1)
    return jnp.concatenate([rot.astype(x.dtype), x[..., ROPE_DIM:]], axis=-1)


def masked_probs(logits, mask):
    lg = jnp.where(mask, logits.astype(F32), NEG)
    m = jnp.max(lg, axis=-1, keepdims=True)
    p = jnp.exp(lg - m) * mask
    return p / jnp.maximum(jnp.sum(p, axis=-1, keepdims=True), 1e-30)


def qblock(t):
    return Q_BLOCK if t % Q_BLOCK == 0 else t


def map_query_blocks(fn, q_pos, *arrs):
    tq = q_pos.shape[0]
    qb = qblock(tq)
    nb = tq // qb
    xs = (q_pos.reshape(nb, qb),) + tuple(
        a.reshape((a.shape[0], nb, qb) + a.shape[2:]).swapaxes(0, 1) for a in arrs)
    outs = lax.map(lambda t: fn(*t), xs)
    return tuple(o.swapaxes(0, 1).reshape((o.shape[1], tq) + o.shape[3:]) for o in outs)


def take_blocks(blocks, idx):
    return jax.vmap(lambda bl, ix: bl[ix])(blocks, idx)


def nsa_compress(rows, pe, w1, w2):
    seq_k = rows.shape[1]
    n_cmp = (seq_k - NSA_CMP_LEN) // NSA_CMP_STRIDE + 1
    idx = jnp.arange(n_cmp, dtype=jnp.int32)[:, None] * NSA_CMP_STRIDE + jnp.arange(NSA_CMP_LEN, dtype=jnp.int32)[None, :]
    blk = rows[:, idx] + pe
    blk = blk.reshape(blk.shape[:2] + (NSA_CMP_LEN * HEAD_DIM,))
    return jax.nn.gelu(blk @ w1) @ w2


def nsa_cmp_slc(q, qpos, cmp_all, slc_all, prm):
    bsz, seq_k = cmp_all.shape[0], cmp_all.shape[1]
    kc = rms_norm(nsa_compress(cmp_all[:, :, 0], prm['pe_cmp_k'], prm['w_cmp_k1'], prm['w_cmp_k2']), prm['g_nsa_kc'])
    vc = nsa_compress(cmp_all[:, :, 1], prm['pe_cmp_v'], prm['w_cmp_v1'], prm['w_cmp_v2'])
    n_cmp = kc.shape[1]
    cmp_start = jnp.arange(n_cmp, dtype=jnp.int32) * NSA_CMP_STRIDE
    cmp_end = cmp_start + NSA_CMP_LEN - 1
    n_sel = -(-seq_k // NSA_SEL_BLOCK)
    sel_start = jnp.arange(n_sel, dtype=jnp.int32) * NSA_SEL_BLOCK
    cover = ((cmp_start[:, None] <= sel_start[None, :] + NSA_SEL_BLOCK - 1)
             & (cmp_end[:, None] >= sel_start[None, :])).astype(F32)
    slc = jnp.pad(slc_all, ((0, 0), (0, n_sel * NSA_SEL_BLOCK - seq_k), (0, 0), (0, 0)))
    slc = slc.reshape(bsz, n_sel, NSA_SEL_BLOCK, 2, HEAD_DIM)
    ks, vs = slc[:, :, :, 0], slc[:, :, :, 1]
    k_top = min(NSA_SEL_TOPK, n_sel)
    blk = jnp.arange(n_sel, dtype=jnp.int32)
    offs = jnp.arange(NSA_SEL_BLOCK, dtype=jnp.int32)

    def one_block(qp, qb):
        nb_, qn = qb.shape[0], qb.shape[1]
        logits = jnp.einsum('bqhd,bnd->bhqn', qb, kc).astype(F32) * SCALE
        p_cmp = masked_probs(logits, cmp_end[None, None, None, :] <= qp[None, None, :, None])
        o_cmp = jnp.einsum('bhqn,bnd->bqhd', p_cmp, vc)
        imp = jnp.einsum('bhqn,ns->bqs', p_cmp, cover)
        cur = qp // NSA_SEL_BLOCK
        forced = (blk[None] == 0) | (blk[None] == cur[:, None]) | (blk[None] == cur[:, None] - 1)
        score = jnp.where(blk[None] > cur[:, None], -1e9, jnp.where(forced, 1e9, imp))
        _, sel = lax.top_k(score, k_top)
        kg = take_blocks(ks, sel)
        vg = take_blocks(vs, sel)
        kpos = sel[..., None] * NSA_SEL_BLOCK + offs
        mask = (kpos <= qp[None, :, None, None]).reshape(nb_, qn, 1, k_top * NSA_SEL_BLOCK)
        ls = jnp.einsum('bqhd,bqkrd->bqhkr', qb, kg).astype(F32) * SCALE
        ps = masked_probs(ls.reshape(nb_, qn, NSA_HEADS, k_top * NSA_SEL_BLOCK), mask)
        o_slc = jnp.einsum('bqhkr,bqkrd->bqhd', ps.reshape(nb_, qn, NSA_HEADS, k_top, NSA_SEL_BLOCK), vg)
        return o_cmp, o_slc

    return map_query_blocks(one_block, qpos, q)


def window_attention(q, kw, vw, q_pos0, k_pos0):
    bsz, tq = q.shape[0], q.shape[1]
    qb = qblock(tq)
    nb = tq // qb
    span = qb + NSA_WINDOW - 1
    kp = jnp.pad(kw, ((0, 0), (NSA_WINDOW, 0), (0, 0)))
    vp = jnp.pad(vw, ((0, 0), (NSA_WINDOW, 0), (0, 0)))
    start = (q_pos0 - k_pos0) + jnp.arange(nb, dtype=jnp.int32) * qb + 1
    idx = start[:, None] + jnp.arange(span, dtype=jnp.int32)[None, :]
    kb, vb = kp[:, idx], vp[:, idx]
    kpos = k_pos0 + idx - NSA_WINDOW
    qpos = q_pos0 + jnp.arange(tq, dtype=jnp.int32).reshape(nb, qb)
    rel = qpos[:, :, None] - kpos[:, None, :]
    mask = (rel >= 0) & (rel < NSA_WINDOW) & (kpos[:, None, :] >= k_pos0)
    qr = q.reshape(bsz, nb, qb, NSA_HEADS, HEAD_DIM)
    logits = jnp.einsum('bnqhd,bnsd->bnhqs', qr, kb).astype(F32) * SCALE
    p = masked_probs(logits, mask[None, :, None])
    return jnp.einsum('bnhqs,bnsd->bnqhd', p, vb).reshape(bsz, tq, NSA_HEADS, HEAD_DIM)


def moba_attention(q, qpos, k, v):
    bsz, seq_k = k.shape[0], k.shape[1]
    n_blk = -(-seq_k // MOBA_BLOCK)
    pad = ((0, 0), (0, n_blk * MOBA_BLOCK - seq_k), (0, 0), (0, 0))
    kb = jnp.pad(k, pad).reshape(bsz, n_blk, MOBA_BLOCK, MOBA_HEADS, HEAD_DIM).transpose(0, 3, 1, 2, 4)
    vb = jnp.pad(v, pad).reshape(bsz, n_blk, MOBA_BLOCK, MOBA_HEADS, HEAD_DIM).transpose(0, 3, 1, 2, 4)
    kmean = jnp.mean(kb.astype(F32), axis=3)
    k_top = min(MOBA_TOPK, n_blk - 1)
    jn = jnp.arange(n_blk, dtype=jnp.int32)
    offs = jnp.arange(MOBA_BLOCK, dtype=jnp.int32)

    def one_block(qp, qb):
        nb_, qn = qb.shape[0], qb.shape[1]
        cur = qp // MOBA_BLOCK
        own = jnp.broadcast_to(cur[None, None, :, None], (nb_, MOBA_HEADS, qn, 1))
        if k_top > 0:
            gate = jnp.einsum('bqhd,bhnd->bhqn', qb.astype(F32), kmean)
            gate = jnp.where((jn[None, :] < cur[:, None])[None, None], gate, NEG)
            _, top = lax.top_k(gate, k_top)
            idx = jnp.concatenate([top, own], axis=-1)
            ok = jnp.concatenate([top < cur[None, None, :, None], jnp.ones(own.shape, bool)], axis=-1)
        else:
            idx, ok = own, jnp.ones(own.shape, bool)
        kg = jax.vmap(take_blocks)(kb, idx)
        vg = jax.vmap(take_blocks)(vb, idx)
        k1 = idx.shape[-1]
        kpos = idx[..., None] * MOBA_BLOCK + offs
        mask = ok[..., None] & (kpos <= qp[None, None, :, None, None])
        logits = jnp.einsum('bqhd,bhqkrd->bhqkr', qb, kg).astype(F32) * SCALE
        p = masked_probs(logits.reshape(nb_, MOBA_HEADS, qn, k1 * MOBA_BLOCK),
                         mask.reshape(nb_, MOBA_HEADS, qn, k1 * MOBA_BLOCK))
        o = jnp.einsum('bhqkr,bhqkrd->bqhd', p.reshape(nb_, MOBA_HEADS, qn, k1, MOBA_BLOCK), vg)
        return (o,)

    return map_query_blocks(one_block, qpos, q)[0]


def fox_attention(q, qpos, k, v, logf):
    seq_k, tq = k.shape[1], q.shape[1]
    cum = jnp.cumsum(logf.astype(F32), axis=1)
    cq = cum[:, seq_k - tq:]
    cum_t = cum.transpose(0, 2, 1)
    kpos = jnp.arange(seq_k, dtype=jnp.int32)

    def one_block(qp, qb, cqb):
        logits = jnp.einsum('bqhd,bshd->bhqs', qb, k).astype(F32) * SCALE
        logits = logits + cqb.transpose(0, 2, 1)[..., None] - cum_t[:, :, None, :]
        p = masked_probs(logits, kpos[None, None, None, :] <= qp[None, None, :, None])
        return (jnp.einsum('bhqs,bshd->bqhd', p, v),)

    return map_query_blocks(one_block, qpos, q, cq)[0]


def causal_conv(u, buf, w, b):
    t = u.shape[1]
    full = jnp.concatenate([buf.astype(u.dtype), u], axis=1)
    acc = b + full[:, 0:t] * w[0]
    for i in range(1, SSD_CONV):
        acc = acc + full[:, i:i + t] * w[i]
    return jax.nn.silu(acc), full[:, t:]


def ssd_scan(x, dt, a, bm, cm, h0):
    bsz, t = x.shape[0], x.shape[1]
    qc = SSD_CHUNK if t % SSD_CHUNK == 0 else t
    nc = t // qc
    rep = SSD_HEADS // SSD_GROUPS
    bh = jnp.repeat(bm, rep, axis=2)
    ch = jnp.repeat(cm, rep, axis=2)
    causal = jnp.tril(jnp.ones((qc, qc), bool))

    def chunk(arr):
        return arr.reshape((bsz, nc, qc) + arr.shape[2:]).swapaxes(0, 1)

    def step(h, inp):
        xc, dtc, bc, cc = inp
        cum = jnp.cumsum(dtc * a, axis=1)
        seg = cum[:, :, None, :] - cum[:, None, :, :]
        lmat = jnp.exp(jnp.where(causal[None, :, :, None], seg, -jnp.inf))
        xdt = xc * dtc[..., None]
        y_diag = jnp.einsum('blhn,bshn,blsh,bshp->blhp', cc, bc, lmat, xdt)
        y_off = jnp.einsum('blhn,bhpn,blh->blhp', cc, h, jnp.exp(cum))
        decay = jnp.exp(cum[:, -1:] - cum)
        h_new = jnp.exp(cum[:, -1])[:, :, None, None] * h + jnp.einsum('bshn,bsh,bshp->bhpn', bc, decay, xdt)
        return h_new, y_diag + y_off

    h, y = lax.scan(step, h0.astype(F32), (chunk(x), chunk(dt), chunk(bh), chunk(ch)))
    y = y.swapaxes(0, 1).reshape(bsz, t, SSD_HEADS, SSD_HEAD_DIM)
    return y, h.astype(h0.dtype)


def token_mixer(xn, prm, past):
    bsz, t, _ = xn.shape
    pos0 = 0 if past is None else past[0].shape[1]
    pos = pos0 + jnp.arange(t, dtype=jnp.int32)
    proj = xn @ prm['w_in']
    nsa_q, nsa_kv, nsa_g, moba_qkv, fox_qkv, fox_f, ssd_in, merge_g = jnp.split(proj, IN_SPLITS, axis=-1)

    q = rope(rms_norm(nsa_q.reshape(bsz, t, NSA_HEADS, HEAD_DIM), prm['g_nsa_q']), pos)
    kv = nsa_kv.reshape(bsz, t, 3, 2, HEAD_DIM)
    kn = rope(rms_norm(kv[:, :, :, 0], prm['g_nsa_k']), pos)
    cmp_rows = jnp.stack([kn[:, :, 0], kv[:, :, 0, 1]], axis=2)
    slc_rows = jnp.stack([kn[:, :, 1], kv[:, :, 1, 1]], axis=2)
    win_rows = jnp.stack([kn[:, :, 2], kv[:, :, 2, 1]], axis=2)
    if past is None:
        cmp_all, slc_all, win_all, win_pos0 = cmp_rows, slc_rows, win_rows, 0
        win_state = win_rows[:, -min(NSA_WINDOW, t):]
    else:
        cmp_all = jnp.concatenate([past[0], cmp_rows], axis=1)
        slc_all = jnp.concatenate([past[1], slc_rows], axis=1)
        wb = past[2].shape[1]
        win_all = jnp.concatenate([past[2], win_rows], axis=1)
        win_pos0 = pos0 - wb
        win_state = win_all[:, -wb:]
    o_cmp, o_slc = nsa_cmp_slc(q, pos, cmp_all, slc_all, prm)
    o_win = window_attention(q, win_all[:, :, 0], win_all[:, :, 1], pos0, win_pos0)
    g = jax.nn.sigmoid(nsa_g.astype(F32).reshape(bsz, t, NSA_HEADS, 3))
    o_nsa = g[..., 0:1] * o_cmp + g[..., 1:2] * o_slc + g[..., 2:3] * o_win

    mqkv = moba_qkv.reshape(bsz, t, 3, MOBA_HEADS, HEAD_DIM)
    mq = rope(rms_norm(mqkv[:, :, 0], prm['g_moba_q']), pos)
    mk = rope(rms_norm(mqkv[:, :, 1], prm['g_moba_k']), pos)
    moba_rows = jnp.stack([mk, mqkv[:, :, 2]], axis=2)
    moba_all = moba_rows if past is None else jnp.concatenate([past[3], moba_rows], axis=1)
    o_moba = moba_attention(mq, pos, moba_all[:, :, 0], moba_all[:, :, 1])

    fqkv = fox_qkv.reshape(bsz, t, 3, FOX_HEADS, HEAD_DIM)
    fq = rms_norm(fqkv[:, :, 0], prm['g_fox_q'])
    fk = rms_norm(fqkv[:, :, 1], prm['g_fox_k'])
    fox_rows = jnp.stack([fk, fqkv[:, :, 2]], axis=2)
    logf_rows = jax.nn.log_sigmoid(fox_f.astype(F32) + prm['b_fox_f'].astype(F32))
    if past is None:
        fox_all, logf_all = fox_rows, logf_rows
    else:
        fox_all = jnp.concatenate([past[4], fox_rows], axis=1)
        logf_all = jnp.concatenate([past[5].astype(F32), logf_rows], axis=1)
    o_fox = fox_attention(fq, pos, fox_all[:, :, 0], fox_all[:, :, 1], logf_all)

    z, xs_, bm, cm, dt = jnp.split(ssd_in, (SSD_INNER, 2 * SSD_INNER, 2 * SSD_INNER + SSD_BC,
                                            2 * SSD_INNER + 2 * SSD_BC), axis=-1)
    xbc = jnp.concatenate([xs_, bm, cm], axis=-1)
    if past is None:
        conv_buf = jnp.zeros((bsz, SSD_CONV - 1, SSD_CONV_DIM), xn.dtype)
        h0 = jnp.zeros((bsz, SSD_HEADS, SSD_HEAD_DIM, SSD_STATE), F32)
    else:
        h0, conv_buf = past[6], past[7]
    xbc, conv_state = causal_conv(xbc, conv_buf, prm['ssd_conv_w'], prm['ssd_conv_b'])
    xs_, bm, cm = jnp.split(xbc, (SSD_INNER, SSD_INNER + SSD_BC), axis=-1)
    dt = jax.nn.softplus(dt.astype(F32) + prm['ssd_dt_bias'].astype(F32))
    a = -jnp.exp(prm['ssd_a_log'].astype(F32))
    xh = xs_.reshape(bsz, t, SSD_HEADS, SSD_HEAD_DIM)
    y, ssd_state = ssd_scan(xh, dt, a, bm.reshape(bsz, t, SSD_GROUPS, SSD_STATE),
                            cm.reshape(bsz, t, SSD_GROUPS, SSD_STATE), h0)
    y = y + prm['ssd_d'][:, None] * xh
    y = y.reshape(bsz, t, SSD_INNER) * jax.nn.silu(z)
    o_ssd = rms_norm(y.reshape(bsz, t, SSD_GROUPS, SSD_INNER // SSD_GROUPS),
                     prm['ssd_norm'].reshape(SSD_GROUPS, SSD_INNER // SSD_GROUPS)).reshape(bsz, t, SSD_INNER)

    gates = jax.nn.sigmoid(merge_g.astype(F32).reshape(bsz, t, N_BRANCH, D_MODEL))
    merged = (gates[:, :, 0] * (o_nsa.reshape(bsz, t, NSA_WIDTH) @ prm['w_br_nsa'])
              + gates[:, :, 1] * (o_moba.reshape(bsz, t, MOBA_WIDTH) @ prm['w_br_moba'])
              + gates[:, :, 2] * (o_fox.reshape(bsz, t, FOX_WIDTH) @ prm['w_br_fox'])
              + gates[:, :, 3] * (o_ssd @ prm['w_br_ssd']))
    out = (merged @ prm['w_out']).astype(xn.dtype)
    state = (cmp_rows, slc_rows, win_state, moba_rows, fox_rows, logf_rows, ssd_state, conv_state)
    return out, state


def hier_moe(xn, prm):
    bsz, t, d = xn.shape
    xt = xn.reshape(bsz * t, d)
    g_logits = (xt @ prm['w_router_grp'] + prm['b_router_grp']).astype(F32)
    grp = jnp.argmax(g_logits, axis=-1)
    g_w = jnp.take_along_axis(jax.nn.softmax(g_logits, axis=-1), grp[:, None], axis=-1)
    e_logits = (xt @ prm['w_router_exp'] + prm['b_router_exp']).astype(F32).reshape(-1, MOE_GROUPS, MOE_PER_GROUP)
    e_in = jnp.take_along_axis(e_logits, grp[:, None, None], axis=1)[:, 0]
    top_v, top_i = lax.top_k(e_in, MOE_TOPK)
    w_top = jax.nn.softmax(top_v, axis=-1) * g_w
    eid = grp[:, None] * MOE_PER_GROUP + top_i
    combine = jnp.sum(jax.nn.one_hot(eid, MOE_EXPERTS, dtype=F32) * w_top[..., None], axis=1)
    y = jnp.zeros(xt.shape, F32)
    for e in range(MOE_EXPERTS):
        h = jax.nn.silu(xt @ prm['w_exp_gate'][e]) * (xt @ prm['w_exp_up'][e])
        y = y + combine[:, e:e + 1] * (h @ prm['w_exp_down'][e])
    return y.astype(xn.dtype).reshape(bsz, t, d)


def layer(x, prm, past):
    mix, state = token_mixer(rms_norm(x, prm['norm_mix']), prm, past)
    x = x + mix
    x = x + hier_moe(rms_norm(x, prm['norm_ffn']), prm)
    return x, state


def setup_inputs(seed: int = 0) -> dict:
    key = jax.random.key(seed)
    keys = iter(jax.random.split(key, 64))

    def nrm(shape, scale=1.0):
        return jax.random.normal(next(keys), shape, F32) * scale

    def gain(shape):
        return 1.0 + nrm(shape, 0.02)

    n_pages = PAST_LEN // PAGE_SIZE
    n_pool = (DEC_BATCH * n_pages * 5) // 4
    win_buf = min(NSA_WINDOW, PAST_LEN)
    page_table = jax.random.permutation(next(keys), n_pool)[:DEC_BATCH * n_pages].reshape(DEC_BATCH, n_pages).astype(jnp.int32)
    dt0 = jnp.exp(jax.random.uniform(next(keys), (DEPTH, SSD_HEADS), F32, math.log(1e-3), math.log(1e-1)))
    a_init = jax.random.uniform(next(keys), (DEPTH, SSD_HEADS), F32, 1.0, 16.0)
    cmp_in = NSA_CMP_LEN * HEAD_DIM
    return {
        'x_prompt': nrm((BATCH, SEQ, D_MODEL)),
        'x_sample': nrm((DEC_BATCH, DEC_SEQ, D_MODEL)),
        'cache_nsa_cmp': nrm((DEPTH, n_pool, PAGE_SIZE, 2, HEAD_DIM)),
        'cache_nsa_slc': nrm((DEPTH, n_pool, PAGE_SIZE, 2, HEAD_DIM)),
        'state_nsa_win': nrm((DEPTH, DEC_BATCH, win_buf, 2, HEAD_DIM)),
        'cache_moba': nrm((DEPTH, n_pool, PAGE_SIZE, 2, MOBA_HEADS, HEAD_DIM)),
        'cache_fox': nrm((DEPTH, n_pool, PAGE_SIZE, 2, FOX_HEADS, HEAD_DIM)),
        'cache_fox_logf': jax.nn.log_sigmoid(3.0 + nrm((DEPTH, n_pool, PAGE_SIZE, FOX_HEADS))),
        'state_ssd': nrm((DEPTH, DEC_BATCH, SSD_HEADS, SSD_HEAD_DIM, SSD_STATE), 0.1),
        'state_ssd_conv': nrm((DEPTH, DEC_BATCH, SSD_CONV - 1, SSD_CONV_DIM)),
        'page_table': page_table,
        'norm_mix': gain((DEPTH, D_MODEL)),
        'norm_ffn': gain((DEPTH, D_MODEL)),
        'w_in': nrm((DEPTH, D_MODEL, N_IN), D_MODEL ** -0.5),
        'g_nsa_q': gain((DEPTH, HEAD_DIM)),
        'g_nsa_k': gain((DEPTH, 3, HEAD_DIM)),
        'g_nsa_kc': gain((DEPTH, HEAD_DIM)),
        'pe_cmp_k': nrm((DEPTH, NSA_CMP_LEN, HEAD_DIM), 0.02),
        'pe_cmp_v': nrm((DEPTH, NSA_CMP_LEN, HEAD_DIM), 0.02),
        'w_cmp_k1': nrm((DEPTH, cmp_in, NSA_CMP_HIDDEN), cmp_in ** -0.5),
        'w_cmp_k2': nrm((DEPTH, NSA_CMP_HIDDEN, HEAD_DIM), NSA_CMP_HIDDEN ** -0.5),
        'w_cmp_v1': nrm((DEPTH, cmp_in, NSA_CMP_HIDDEN), cmp_in ** -0.5),
        'w_cmp_v2': nrm((DEPTH, NSA_CMP_HIDDEN, HEAD_DIM), NSA_CMP_HIDDEN ** -0.5),
        'g_moba_q': gain((DEPTH, HEAD_DIM)),
        'g_moba_k': gain((DEPTH, HEAD_DIM)),
        'g_fox_q': gain((DEPTH, HEAD_DIM)),
        'g_fox_k': gain((DEPTH, HEAD_DIM)),
        'b_fox_f': 3.0 + nrm((DEPTH, FOX_HEADS), 0.5),
        'ssd_conv_w': nrm((DEPTH, SSD_CONV, SSD_CONV_DIM), SSD_CONV ** -0.5),
        'ssd_conv_b': nrm((DEPTH, SSD_CONV_DIM), 0.01),
        'ssd_dt_bias': dt0 + jnp.log(-jnp.expm1(-dt0)),
        'ssd_a_log': jnp.log(a_init),
        'ssd_d': 1.0 + nrm((DEPTH, SSD_HEADS), 0.1),
        'ssd_norm': gain((DEPTH, SSD_INNER)),
        'w_br_nsa': nrm((DEPTH, NSA_WIDTH, D_MODEL), NSA_WIDTH ** -0.5),
        'w_br_moba': nrm((DEPTH, MOBA_WIDTH, D_MODEL), MOBA_WIDTH ** -0.5),
        'w_br_fox': nrm((DEPTH, FOX_WIDTH, D_MODEL), FOX_WIDTH ** -0.5),
        'w_br_ssd': nrm((DEPTH, SSD_INNER, D_MODEL), SSD_INNER ** -0.5),
        'w_out': nrm((DEPTH, D_MODEL, D_MODEL), D_MODEL ** -0.5),
        'w_router_grp': nrm((DEPTH, D_MODEL, MOE_GROUPS), D_MODEL ** -0.5),
        'b_router_grp': nrm((DEPTH, MOE_GROUPS), 0.01),
        'w_router_exp': nrm((DEPTH, D_MODEL, MOE_EXPERTS), D_MODEL ** -0.5),
        'b_router_exp': nrm((DEPTH, MOE_EXPERTS), 0.01),
        'w_exp_gate': nrm((DEPTH, MOE_EXPERTS, D_MODEL, MOE_HIDDEN), D_MODEL ** -0.5),
        'w_exp_up': nrm((DEPTH, MOE_EXPERTS, D_MODEL, MOE_HIDDEN), D_MODEL ** -0.5),
        'w_exp_down': nrm((DEPTH, MOE_EXPERTS, MOE_HIDDEN, D_MODEL), MOE_HIDDEN ** -0.5),
    }


def reference(x_prompt, x_sample, cache_nsa_cmp, cache_nsa_slc, state_nsa_win, cache_moba, cache_fox,
              cache_fox_logf, state_ssd, state_ssd_conv, page_table, norm_mix, norm_ffn, w_in, g_nsa_q,
              g_nsa_k, g_nsa_kc, pe_cmp_k, pe_cmp_v, w_cmp_k1, w_cmp_k2, w_cmp_v1, w_cmp_v2, g_moba_q,
              g_moba_k, g_fox_q, g_fox_k, b_fox_f, ssd_conv_w, ssd_conv_b, ssd_dt_bias, ssd_a_log, ssd_d,
              ssd_norm, w_br_nsa, w_br_moba, w_br_fox, w_br_ssd, w_out, w_router_grp, b_router_grp,
              w_router_exp, b_router_exp, w_exp_gate, w_exp_up, w_exp_down):
    def gather_past(pool):
        rows = pool[page_table]
        return rows.reshape((rows.shape[0], rows.shape[1] * rows.shape[2]) + rows.shape[3:])

    xp, xs = x_prompt, x_sample
    sp, ss = [], []
    for l in range(DEPTH):
        prm = {
            'norm_mix': norm_mix[l], 'norm_ffn': norm_ffn[l], 'w_in': w_in[l],
            'g_nsa_q': g_nsa_q[l], 'g_nsa_k': g_nsa_k[l], 'g_nsa_kc': g_nsa_kc[l],
            'pe_cmp_k': pe_cmp_k[l], 'pe_cmp_v': pe_cmp_v[l],
            'w_cmp_k1': w_cmp_k1[l], 'w_cmp_k2': w_cmp_k2[l], 'w_cmp_v1': w_cmp_v1[l], 'w_cmp_v2': w_cmp_v2[l],
            'g_moba_q': g_moba_q[l], 'g_moba_k': g_moba_k[l], 'g_fox_q': g_fox_q[l], 'g_fox_k': g_fox_k[l],
            'b_fox_f': b_fox_f[l], 'ssd_conv_w': ssd_conv_w[l], 'ssd_conv_b': ssd_conv_b[l],
            'ssd_dt_bias': ssd_dt_bias[l], 'ssd_a_log': ssd_a_log[l], 'ssd_d': ssd_d[l], 'ssd_norm': ssd_norm[l],
            'w_br_nsa': w_br_nsa[l], 'w_br_moba': w_br_moba[l], 'w_br_fox': w_br_fox[l], 'w_br_ssd': w_br_ssd[l],
            'w_out': w_out[l], 'w_router_grp': w_router_grp[l], 'b_router_grp': b_router_grp[l],
            'w_router_exp': w_router_exp[l], 'b_router_exp': b_router_exp[l],
            'w_exp_gate': w_exp_gate[l], 'w_exp_up': w_exp_up[l], 'w_exp_down': w_exp_down[l],
        }
        past = (gather_past(cache_nsa_cmp[l]), gather_past(cache_nsa_slc[l]), state_nsa_win[l],
                gather_past(cache_moba[l]), gather_past(cache_fox[l]), gather_past(cache_fox_logf[l]),
                state_ssd[l], state_ssd_conv[l])
        xp, st_p = layer(xp, prm, None)
        xs, st_s = layer(xs, prm, past)
        sp.append(st_p)
        ss.append(st_s)

    def stk(states, i):
        return jnp.stack([s[i] for s in states], axis=0)

    return (xp, xs, stk(sp, 0), stk(ss, 0), stk(sp, 1), stk(ss, 1), stk(sp, 2), stk(ss, 2),
            stk(sp, 3), stk(ss, 3), stk(sp, 4), stk(ss, 4), stk(sp, 5), stk(ss, 5),
            stk(sp, 6), stk(ss, 6), stk(sp, 7), stk(ss, 7))
```

```python
import functools
import math

import jax
import jax.numpy as jnp
from jax import lax
from jax.experimental import pallas as pl
from jax.experimental.pallas import tpu as pltpu

F32 = jnp.float32
BF16 = jnp.bfloat16
I32 = jnp.int32

D_MODEL = 1024
HEAD_DIM = 64
N_HEADS = 4
HEADS_W = N_HEADS * HEAD_DIM
ROPE_DIM = HEAD_DIM // 4
ROPE_HALF = ROPE_DIM // 2
ROPE_THETA = 500000.0
SCALE = HEAD_DIM ** -0.5
EPS = 1e-6
NEG = -1e30
PAGE = 128

NSA_CMP_LEN = 32
NSA_CMP_STRIDE = 16
NSA_CMP_HIDDEN = 2 * HEAD_DIM
NSA_SEL_BLOCK = 64
NSA_SEL_TOPK = 16
NSA_WINDOW = 512
MOBA_BLOCK = 256
MOBA_TOPK = 3

SSD_HEADS = 8
SSD_HEAD_DIM = 64
SSD_INNER = SSD_HEADS * SSD_HEAD_DIM
SSD_STATE = 64
SSD_GROUPS = 2
SSD_CONV = 4
SSD_CHUNK = 128
SSD_BC = SSD_GROUPS * SSD_STATE
SSD_CONV_DIM = SSD_INNER + 2 * SSD_BC

MOE_GROUPS = 4
MOE_PER_GROUP = 4
MOE_EXPERTS = 16
MOE_HIDDEN = 512

LANE = 128
SUB = 8
VMEM_LIMIT = 56 * 1024 * 1024

_IN_SIZES = (HEADS_W, 6 * HEAD_DIM, 3 * N_HEADS, 3 * HEADS_W, 3 * HEADS_W, N_HEADS,
             2 * SSD_INNER + 2 * SSD_BC + SSD_HEADS, 4 * D_MODEL)
_IN_OFFS = tuple(int(sum(_IN_SIZES[:i])) for i in range(len(_IN_SIZES) + 1))

P_NSA_Q = 0
P_NSA_KV = 256
P_MOBA = 640
P_FOX = 1408
P_SSD = 2176
P_SMALL = 3456
P_DT = 3584
P_WIDTH = 3712
SM_FOXF = 0
SM_NSAG = 4


def _cparams(sem):
    return pltpu.CompilerParams(dimension_semantics=sem, vmem_limit_bytes=VMEM_LIMIT)


def _sigmoid(x):
    return 1.0 / (1.0 + jnp.exp(-x))


def _silu(x):
    return x * _sigmoid(x)


def _softplus(x):
    return jnp.maximum(x, 0.0) + jnp.log(1.0 + jnp.exp(-jnp.abs(x)))


def _dot(a, b):
    return jnp.dot(a, b, preferred_element_type=F32)


def _dot_nt(a, b):
    return lax.dot_general(a, b, (((1,), (1,)), ((), ())), preferred_element_type=F32)


def _split2(x):
    hi = x.astype(BF16)
    lo = (x - hi.astype(F32)).astype(BF16)
    return hi, lo


def _split3(x):
    h1 = x.astype(BF16)
    r = x - h1.astype(F32)
    h2 = r.astype(BF16)
    h3 = (r - h2.astype(F32)).astype(BF16)
    return h1, h2, h3


def _mm(a, b, precise):
    if precise:
        ah, al = _split2(a)
        bh, bl = _split2(b)
        return _dot(ah, bh) + _dot(ah, bl) + _dot(al, bh)
    return _dot(a.astype(BF16), b.astype(BF16))


def _mm_nt(a, b, precise):
    if precise:
        ah, al = _split2(a)
        bh, bl = _split2(b)
        return _dot_nt(ah, bh) + _dot_nt(ah, bl) + _dot_nt(al, bh)
    return _dot_nt(a.astype(BF16), b.astype(BF16))


def _dot_exact(x, e):
    h1, h2, h3 = _split3(x)
    return _dot(h1, e) + _dot(h2, e) + _dot(h3, e)


def _head_sumsq(y, precise=False):
    r = lax.broadcasted_iota(I32, (LANE, LANE), 0) // HEAD_DIM
    c = lax.broadcasted_iota(I32, (LANE, LANE), 1) // HEAD_DIM
    e = jnp.where(r == c, 1.0, 0.0).astype(BF16)
    if precise:
        return _dot_exact(y * y, e)
    hi, lo = _split2(y * y)
    return _dot(hi, e) + _dot(lo, e)


def _rope_chunk(y, cos_t, sin_a, sin_b):
    return (y * cos_t + pltpu.roll(y, LANE - ROPE_HALF, 1) * sin_a
            + pltpu.roll(y, ROPE_HALF, 1) * sin_b)


def _proj_kernel(x_ref, gmix_ref, w_ref, gains_ref, tab_ref, bias_ref,
                 qn_ref, cmp_ref, slc_ref, win_ref, qm_ref, moba_ref, qf_ref, fox_ref,
                 z_ref, xbc_ref, small_ref, dt_ref, *, precise):
    x = x_ref[...]
    ms = jnp.mean(x * x, axis=-1, keepdims=True)
    xn = x * lax.rsqrt(ms + EPS) * gmix_ref[...]
    xn = xn if precise else xn.astype(BF16)
    cos_t = tab_ref[0]
    sin_a = tab_ref[1]
    sin_b = tab_ref[2]
    lane = lax.broadcasted_iota(I32, (1, LANE), 1)
    first = lane < HEAD_DIM

    def proj(off, width):
        return _mm(xn, w_ref[:, off:off + width], precise)

    def normed(y, gain_row, k_only=False):
        g = gains_ref[gain_row:gain_row + 1, :]
        yn = y * lax.rsqrt(_head_sumsq(y, precise) * (1.0 / HEAD_DIM) + EPS) * g
        return jnp.where(first, yn, y) if k_only else yn

    p = proj(P_NSA_Q, HEADS_W)
    for c in range(2):
        y = _rope_chunk(normed(p[:, c * LANE:(c + 1) * LANE], 0), cos_t, sin_a, sin_b)
        qn_ref[:, c * LANE:(c + 1) * LANE] = (y * SCALE).astype(qn_ref.dtype)
    p = proj(P_NSA_KV, 3 * LANE)
    cos_k = jnp.where(first, cos_t, 1.0)
    sin_ak = jnp.where(first, sin_a, 0.0)
    sin_bk = jnp.where(first, sin_b, 0.0)
    for c, ref in enumerate((cmp_ref, slc_ref, win_ref)):
        y = normed(p[:, c * LANE:(c + 1) * LANE], 1 + c, k_only=True)
        ref[...] = _rope_chunk(y, cos_k, sin_ak, sin_bk)
    p = proj(P_MOBA, 3 * HEADS_W)
    for c in range(2):
        y = _rope_chunk(normed(p[:, c * LANE:(c + 1) * LANE], 4), cos_t, sin_a, sin_b)
        qm_ref[:, c * LANE:(c + 1) * LANE] = (y * SCALE).astype(qn_ref.dtype)
        y = _rope_chunk(normed(p[:, HEADS_W + c * LANE:HEADS_W + (c + 1) * LANE], 5),
                        cos_t, sin_a, sin_b)
        moba_ref[:, c * LANE:(c + 1) * LANE] = y
    moba_ref[:, HEADS_W:] = p[:, 2 * HEADS_W:]
    p = proj(P_FOX, 3 * HEADS_W)
    for c in range(2):
        y = normed(p[:, c * LANE:(c + 1) * LANE], 6)
        qf_ref[:, c * LANE:(c + 1) * LANE] = (y * SCALE).astype(qn_ref.dtype)
        fox_ref[:, c * LANE:(c + 1) * LANE] = normed(
            p[:, HEADS_W + c * LANE:HEADS_W + (c + 1) * LANE], 7)
    fox_ref[:, HEADS_W:] = p[:, 2 * HEADS_W:]
    p = proj(P_SSD, 2 * SSD_INNER + 2 * SSD_BC)
    z_ref[...] = p[:, :SSD_INNER]
    xbc_ref[...] = p[:, SSD_INNER:]
    p = proj(P_SMALL, LANE)
    logf = -_softplus(-(p + bias_ref[0:1, :]))
    small_ref[...] = jnp.where(lane < SM_NSAG, logf, p)
    p = proj(P_DT, LANE)
    dt_ref[...] = _softplus(p + bias_ref[1:2, :])


def _proj_call(x2, gmix, wp, gains, tabs, biases, tm, precise=False):
    n = x2.shape[0]
    nt = tabs.shape[1] // tm
    row = lambda w: pl.BlockSpec((tm, w), lambda i: (i, 0))
    full = lambda a: pl.BlockSpec(a.shape, lambda i: (0,) * a.ndim)
    widths = (HEADS_W, LANE, LANE, LANE, HEADS_W, 2 * HEADS_W, HEADS_W, 2 * HEADS_W,
              SSD_INNER, SSD_CONV_DIM, LANE, LANE)
    qd = F32 if precise else BF16
    dtypes = (qd, F32, F32, F32, qd, F32, qd, F32, F32, F32, F32, F32)
    return pl.pallas_call(
        functools.partial(_proj_kernel, precise=precise),
        grid=(n // tm,),
        in_specs=[row(D_MODEL), full(gmix), full(wp), full(gains),
                  pl.BlockSpec((3, tm, LANE), lambda i: (0, i % nt, 0)), full(biases)],
        out_specs=[row(w) for w in widths],
        out_shape=[jax.ShapeDtypeStruct((n, w), d) for w, d in zip(widths, dtypes)],
        compiler_params=_cparams(("parallel",)),
        name="proj",
    )(x2, gmix, wp, gains, tabs, biases)


def _rope_tables(pos):
    inv = jnp.power(ROPE_THETA, -jnp.arange(ROPE_HALF, dtype=F32) / ROPE_HALF)
    ang = pos.astype(F32)[:, None] * inv[None, :]
    cos, sin = jnp.cos(ang), jnp.sin(ang)
    t = pos.shape[0]
    one = jnp.ones((t, HEAD_DIM - ROPE_DIM), F32)
    zero = jnp.zeros((t, HEAD_DIM - ROPE_DIM), F32)
    zh = jnp.zeros((t, ROPE_HALF), F32)
    c = jnp.concatenate([cos, cos, one], axis=1)
    a = jnp.concatenate([-sin, zh, zero], axis=1)
    b = jnp.concatenate([zh, sin, zero], axis=1)
    return jnp.stack([jnp.tile(c, (1, 2)), jnp.tile(a, (1, 2)), jnp.tile(b, (1, 2))], axis=0)


def _pack_w_in(w_in):
    o = _IN_OFFS
    ssd = w_in[:, o[6]:o[7]]
    small = jnp.concatenate([w_in[:, o[5]:o[6]], w_in[:, o[2]:o[3]],
                             jnp.zeros((D_MODEL, LANE - 4 * N_HEADS), F32)], axis=1)
    dt = jnp.concatenate([ssd[:, 2 * SSD_INNER + 2 * SSD_BC:],
                          jnp.zeros((D_MODEL, LANE - SSD_HEADS), F32)], axis=1)
    wp = jnp.concatenate([w_in[:, o[0]:o[2]], w_in[:, o[3]:o[5]],
                          ssd[:, :2 * SSD_INNER + 2 * SSD_BC], small, dt], axis=1)
    return wp, w_in[:, o[7]:o[8]]


def _proj_gains(prm):
    two = lambda g: jnp.tile(g, 2)
    ones = jnp.ones((HEAD_DIM,), F32)
    rows = [two(prm['g_nsa_q'])]
    rows += [jnp.concatenate([prm['g_nsa_k'][i], ones]) for i in range(3)]
    rows += [two(prm['g_moba_q']), two(prm['g_moba_k']), two(prm['g_fox_q']), two(prm['g_fox_k'])]
    return jnp.stack(rows, axis=0)


def _proj_biases(prm):
    pad = lambda v: jnp.concatenate([v.astype(F32), jnp.zeros((LANE - v.shape[0],), F32)])
    rows = [pad(prm['b_fox_f']), pad(prm['ssd_dt_bias'])] + [jnp.zeros((LANE,), F32)] * 6
    return jnp.stack(rows, axis=0)


def _paged_gather_kernel(pt_ref, *refs, g, r):
    out_ref = refs[g]
    for i in range(g):
        out_ref[0, i * r:(i + 1) * r, :] = refs[i][0]


def _paged_gather(pool, page_table, g):
    _, r, c = pool.shape
    bsz, n_pages = page_table.shape
    assert n_pages % g == 0 and (g * r) % SUB == 0
    specs = [pl.BlockSpec((1, r, c), functools.partial(
        lambda b, j, pt, i: (pt[b, j * g + i], 0, 0), i=i)) for i in range(g)]
    return pl.pallas_call(
        functools.partial(_paged_gather_kernel, g=g, r=r),
        grid_spec=pltpu.PrefetchScalarGridSpec(
            num_scalar_prefetch=1, grid=(bsz, n_pages // g), in_specs=specs,
            out_specs=pl.BlockSpec((1, g * r, c), lambda b, j, pt: (b, j, 0))),
        out_shape=jax.ShapeDtypeStruct((bsz, n_pages * r, c), pool.dtype),
        compiler_params=_cparams(("parallel", "parallel")),
        name="paged_gather",
    )(page_table, *([pool] * g))


def _cmp_mlp_kernel(r_ref, wab_ref, pe_ref, w2_ref, g_ref, out_ref, *, precise):
    n16 = r_ref.shape[1]
    hw = 2 * NSA_CMP_HIDDEN
    h = _mm(r_ref[0], wab_ref[...], precise)
    c = _mm(pe_ref[...], wab_ref[...], precise)
    c = c[0:1, :hw] + c[1:2, hw:]
    pre = h[:, :hw] + pltpu.roll(h[:, hw:], n16 - 1, 0) + c
    act = 0.5 * pre * (1.0 + jnp.tanh(math.sqrt(2.0 / math.pi) * (pre + 0.044715 * pre * pre * pre)))
    y = _mm(act, w2_ref[...], precise)
    lane = lax.broadcasted_iota(I32, (1, LANE), 1)
    yn = y * lax.rsqrt(_head_sumsq(y, precise) * (1.0 / HEAD_DIM) + EPS) * g_ref[...]
    out_ref[0] = jnp.where(lane < HEAD_DIM, yn, y)


def _cmp_mlp_call(rows16, wab, pe2, w2, gkc, precise=False):
    bsz, n16, w = rows16.shape
    full = lambda a: pl.BlockSpec(a.shape, lambda b: (0,) * a.ndim)
    return pl.pallas_call(
        functools.partial(_cmp_mlp_kernel, precise=precise),
        grid=(bsz,),
        in_specs=[pl.BlockSpec((1, n16, w), lambda b: (b, 0, 0)), full(wab), full(pe2), full(w2),
                  full(gkc)],
        out_specs=pl.BlockSpec((1, n16, LANE), lambda b: (b, 0, 0)),
        out_shape=jax.ShapeDtypeStruct((bsz, n16, LANE), F32),
        compiler_params=_cparams(("parallel",)),
        name="cmp_mlp",
    )(rows16, wab, pe2, w2, gkc)


def _cmp_weights(prm, dtype):
    def expand(w1, slot):
        w = w1.reshape(2, NSA_CMP_STRIDE, HEAD_DIM, NSA_CMP_HIDDEN)
        z = jnp.zeros_like(w)
        pair = (w, z) if slot == 0 else (z, w)
        return jnp.concatenate(pair, axis=2).reshape(2, NSA_CMP_STRIDE * LANE, NSA_CMP_HIDDEN)
    wk, wv = expand(prm['w_cmp_k1'], 0), expand(prm['w_cmp_v1'], 1)
    wab = jnp.concatenate([wk[0], wv[0], wk[1], wv[1]], axis=1).astype(dtype)
    pe = jnp.concatenate([prm['pe_cmp_k'], prm['pe_cmp_v']], axis=1)
    pe = pe.reshape(2, NSA_CMP_STRIDE * LANE)
    pe2 = jnp.concatenate([pe, jnp.zeros((SUB - 2, NSA_CMP_STRIDE * LANE), F32)], axis=0)
    z = jnp.zeros((NSA_CMP_HIDDEN, HEAD_DIM), F32)
    w2 = jnp.concatenate([jnp.concatenate([prm['w_cmp_k2'], z], axis=1),
                          jnp.concatenate([z, prm['w_cmp_v2']], axis=1)], axis=0).astype(dtype)
    gkc = jnp.concatenate([prm['g_nsa_kc'], jnp.ones((HEAD_DIM,), F32)])[None, :]
    return wab, pe2, w2, gkc


def _first_argmax(score, lane):
    m = jnp.max(score, axis=-1, keepdims=True)
    cand = jnp.where(score == m, lane.astype(F32), float(score.shape[-1]))
    return m, jnp.min(cand, axis=-1, keepdims=True).astype(I32)


def _nsa_cmp_kernel(q_ref, kvc_ref, ocmp_ref, sel_ref, idx_ref, *, tq, q_pos0, n_cmp, n_sel, k_top,
                    precise):
    n16 = kvc_ref.shape[1]
    nselp = sel_ref.shape[2]
    qi = pl.program_id(1)
    qpos = q_pos0 + qi * tq + lax.broadcasted_iota(I32, (tq, 1), 0)
    q = q_ref[0]
    kvc = kvc_ref[0]
    kc = kvc[:, :HEAD_DIM] if precise else kvc[:, :HEAD_DIM].astype(BF16)
    vc = kvc[:, HEAD_DIM:] if precise else kvc[:, HEAD_DIM:].astype(BF16)
    n_idx = lax.broadcasted_iota(I32, (1, n16), 1)
    mask = jnp.logical_and(n_idx * NSA_CMP_STRIDE + (NSA_CMP_LEN - 1) <= qpos, n_idx < n_cmp)
    psum = jnp.zeros((tq, n16), F32)
    outs = []
    for h in range(N_HEADS):
        lg = jnp.where(mask, _mm_nt(q[:, h * HEAD_DIM:(h + 1) * HEAD_DIM], kc, precise), NEG)
        m = jnp.max(lg, axis=-1, keepdims=True)
        p = jnp.where(mask, jnp.exp(lg - m), 0.0)
        p = p / jnp.maximum(jnp.sum(p, axis=-1, keepdims=True), 1e-30)
        outs.append(_mm(p, vc, precise))
        psum = psum + p
    ocmp_ref[0] = jnp.concatenate(outs, axis=-1)
    n_col = lax.broadcasted_iota(I32, (n16, nselp), 0) * NSA_CMP_STRIDE
    s_row = lax.broadcasted_iota(I32, (n16, nselp), 1) * NSA_SEL_BLOCK
    cover = jnp.logical_and(n_col <= s_row + (NSA_SEL_BLOCK - 1), n_col + (NSA_CMP_LEN - 1) >= s_row)
    cover = jnp.where(cover, 1.0, 0.0).astype(BF16)
    imp = _dot_exact(psum, cover)
    blk = lax.broadcasted_iota(I32, (1, nselp), 1)
    cur = qpos // NSA_SEL_BLOCK
    forced = jnp.logical_or(blk == 0, jnp.logical_or(blk == cur, blk == cur - 1))
    score = jnp.where(blk > cur, -1e9, jnp.where(forced, 1e9, imp))
    score = jnp.where(blk < n_sel, score, -jnp.inf)
    sel = jnp.zeros((tq, nselp), F32)
    lane = lax.broadcasted_iota(I32, (1, LANE), 1)
    picks = jnp.full((tq, LANE), -1, I32)
    for i in range(k_top):
        _, idx = _first_argmax(score, blk)
        hit = blk == idx
        sel = jnp.where(hit, 1.0, sel)
        picks = jnp.where(lane == i, idx, picks)
        score = jnp.where(hit, -jnp.inf, score)
    sel_ref[0] = sel
    idx_ref[0] = picks


def _nsa_cmp_call(q, kvc, *, tq, q_pos0, n_cmp, n_sel, nselp, precise=False):
    bsz, t, _ = q.shape
    n16 = kvc.shape[1]
    kern = functools.partial(_nsa_cmp_kernel, tq=tq, q_pos0=q_pos0, n_cmp=n_cmp, n_sel=n_sel,
                             k_top=min(NSA_SEL_TOPK, n_sel), precise=precise)
    return pl.pallas_call(
        kern,
        grid=(bsz, t // tq),
        in_specs=[pl.BlockSpec((1, tq, HEADS_W), lambda b, i: (b, i, 0)),
                  pl.BlockSpec((1, n16, LANE), lambda b, i: (b, 0, 0))],
        out_specs=[pl.BlockSpec((1, tq, HEADS_W), lambda b, i: (b, i, 0)),
                   pl.BlockSpec((1, tq, nselp), lambda b, i: (b, i, 0)),
                   pl.BlockSpec((1, tq, LANE), lambda b, i: (b, i, 0))],
        out_shape=[jax.ShapeDtypeStruct((bsz, t, HEADS_W), F32),
                   jax.ShapeDtypeStruct((bsz, t, nselp), F32),
                   jax.ShapeDtypeStruct((bsz, t, LANE), I32)],
        compiler_params=_cparams(("parallel", "parallel")),
        name="nsa_cmp",
    )(q, kvc)


def _attn_kernel(*refs, cfg):
    tq, tkb, g, nk, kw = cfg['tq'], cfg['tkb'], cfg['g'], cfg['nk'], cfg['kw']
    refs = list(refs)
    if cfg['paged']:
        refs.pop(0)
    q_ref = refs.pop(0)
    kv_refs = [refs.pop(0) for _ in range(g)]
    self_ref = refs.pop(0) if cfg['n_self'] else None
    bm_ref = refs.pop(0) if cfg['bm_bs'] else None
    if cfg['bias']:
        cq_ref, cum_ref = refs.pop(0), refs.pop(0)
        cums_ref = refs.pop(0) if cfg['n_self'] else None
    o_ref, m_ref, l_ref, acc_ref = refs

    qi, ki = pl.program_id(1), pl.program_id(2)

    @pl.when(ki == 0)
    def _():
        m_ref[...] = jnp.full(m_ref.shape, NEG, F32)
        l_ref[...] = jnp.zeros(l_ref.shape, F32)
        acc_ref[...] = jnp.zeros(acc_ref.shape, F32)

    q_start = cfg['q_pos0'] + qi * tq
    qpos = q_start + lax.broadcasted_iota(I32, (tq, 1), 0)
    q = q_ref[0]
    bmv = bm_ref[0].astype(BF16) if bm_ref is not None else None
    cq = cq_ref[0] if cfg['bias'] else None

    def tile(kv, k_start, width, cum, n_valid):
        kpos = k_start + lax.broadcasted_iota(I32, (1, width), 1)
        mask = kpos <= qpos
        if cfg['window']:
            mask = jnp.logical_and(mask, qpos - kpos < cfg['window'])
        if n_valid is not None:
            mask = jnp.logical_and(mask, kpos < k_start + n_valid)
        for h in range(N_HEADS):
            ko = h * HEAD_DIM if kw > HEAD_DIM else 0
            kh = kv[:, ko:ko + HEAD_DIM]
            vh = kv[:, kw + ko:kw + ko + HEAD_DIM]
            s = _mm_nt(q[:, h * HEAD_DIM:(h + 1) * HEAD_DIM], kh, cfg['precise'])
            if cum is not None:
                s = s + (cq[:, h:h + 1] - cum[h:h + 1, :])
            mh = mask
            if bmv is not None:
                j = lax.broadcasted_iota(I32, (bmv.shape[1], 1), 0)
                e = jnp.where(j == kpos // cfg['bm_bs'] + h * cfg['bm_hs'], 1.0, 0.0).astype(BF16)
                mh = jnp.logical_and(mask, _dot(bmv, e) > 0.5)
            s = jnp.where(mh, s, NEG)
            m_old = m_ref[h]
            m_new = jnp.maximum(m_old, jnp.max(s, axis=-1, keepdims=True))
            p = jnp.where(mh, jnp.exp(s - m_new), 0.0)
            alpha = jnp.exp(m_old - m_new)
            l_ref[h] = alpha * l_ref[h] + jnp.sum(p, axis=-1, keepdims=True)
            acc_ref[h] = alpha * acc_ref[h] + _mm(p, vh, cfg['precise'])
            m_ref[h] = m_new

    for i in range(g):
        k_start = cfg['k_pos0'] + (ki * g + i) * tkb

        def run(i=i, k_start=k_start):
            cum = cum_ref[0][:, i * tkb:(i + 1) * tkb] if cfg['bias'] else None
            tile(kv_refs[i][0], k_start, tkb, cum, None)

        if cfg['skip']:
            need = k_start <= q_start + tq - 1
            if cfg['window']:
                need = jnp.logical_and(need, k_start + tkb - 1 >= q_start - (cfg['window'] - 1))
            pl.when(need)(run)
        else:
            run()

    @pl.when(ki == nk - 1)
    def _():
        if self_ref is not None:
            cum = cums_ref[0] if cfg['bias'] else None
            tile(self_ref[0], cfg['k_pos0'] + nk * g * tkb, self_ref.shape[1], cum, cfg['n_self'])
        o_ref[0] = jnp.concatenate(
            [acc_ref[h] / jnp.maximum(l_ref[h], 1e-30) for h in range(N_HEADS)], axis=-1)


def _attn_call(q, kv, *, tq, tkb, g=1, q_pos0=0, k_pos0=0, window=0, page_table=None,
               self_rows=None, n_self=0, bm=None, bm_bs=0, bm_hs=0, cq=None, cum_t=None,
               precise=False, name="attn"):
    bsz, t, _ = q.shape
    paged = page_table is not None
    kw = kv.shape[-1] // 2
    tk = page_table.shape[1] * PAGE if paged else kv.shape[1]
    assert t % tq == 0 and tk % (g * tkb) == 0 and (not paged or tkb == PAGE)
    nk = tk // (g * tkb)
    skip = (not paged) and q_pos0 == k_pos0 and not n_self
    cfg = dict(tq=tq, tkb=tkb, g=g, nk=nk, kw=kw, paged=paged, n_self=n_self, bm_bs=bm_bs,
               bm_hs=bm_hs, bias=cq is not None, window=window, q_pos0=q_pos0, k_pos0=k_pos0,
               skip=skip, precise=precise)

    def kv_block(i):
        if paged:
            return pl.BlockSpec((1, PAGE, 2 * kw),
                                lambda b, qi, ki, pt: (pt[b, ki * g + i], 0, 0))

        def imap(b, qi, ki):
            j = ki * g + i
            if skip:
                j = jnp.minimum(j, (q_pos0 + qi * tq + tq - 1 - k_pos0) // tkb)
                if window:
                    j = jnp.maximum(j, jnp.maximum(q_pos0 + qi * tq - (window - 1) - k_pos0, 0) // tkb)
            return (b, j, 0)
        return pl.BlockSpec((1, tkb, 2 * kw), imap)

    def fixed(shape, imap):
        if paged:
            return pl.BlockSpec(shape, lambda b, qi, ki, pt: imap(b, qi, ki))
        return pl.BlockSpec(shape, imap)

    args = [q] + [kv] * g
    specs = [fixed((1, tq, HEADS_W), lambda b, qi, ki: (b, qi, 0))] + [kv_block(i) for i in range(g)]
    if n_self:
        args.append(self_rows)
        specs.append(fixed((1,) + self_rows.shape[1:], lambda b, qi, ki: (b, 0, 0)))
    if bm is not None:
        args.append(bm)
        specs.append(fixed((1, tq, bm.shape[2]), lambda b, qi, ki: (b, qi, 0)))
    if cq is not None:
        args += [cq, cum_t]
        specs += [fixed((1, tq, LANE), lambda b, qi, ki: (b, qi, 0)),
                  fixed((1, SUB, g * tkb), lambda b, qi, ki: (b, 0, ki))]
        if n_self:
            args.append(cum_t)
            specs.append(fixed((1, SUB, LANE), lambda b, qi, ki: (b, 0, tk // LANE)))
            assert self_rows.shape[1] == LANE
    out_spec = fixed((1, tq, HEADS_W), lambda b, qi, ki: (b, qi, 0))
    scratch = [pltpu.VMEM((N_HEADS, tq, 1), F32), pltpu.VMEM((N_HEADS, tq, 1), F32),
               pltpu.VMEM((N_HEADS, tq, HEAD_DIM), F32)]
    kern = functools.partial(_attn_kernel, cfg=cfg)
    out_shape = jax.ShapeDtypeStruct((bsz, t, HEADS_W), F32)
    sem = ("parallel", "parallel", "arbitrary")
    if paged:
        return pl.pallas_call(
            kern,
            grid_spec=pltpu.PrefetchScalarGridSpec(
                num_scalar_prefetch=1, grid=(bsz, t // tq, nk), in_specs=specs,
                out_specs=out_spec, scratch_shapes=scratch),
            out_shape=out_shape, compiler_params=_cparams(sem), name=name,
        )(page_table, *args)
    return pl.pallas_call(
        kern, grid=(bsz, t // tq, nk), in_specs=specs, out_specs=out_spec,
        scratch_shapes=scratch, out_shape=out_shape, compiler_params=_cparams(sem), name=name,
    )(*args)


def _head_expand_matrix():
    r = lax.broadcasted_iota(I32, (LANE, HEADS_W), 0)
    c = lax.broadcasted_iota(I32, (LANE, HEADS_W), 1)
    return jnp.where(c // HEAD_DIM == r, 1.0, 0.0).astype(BF16)


def _dec_attn_kernel(pt_ref, ft_ref, hd_ref, q_ref, *refs, cfg):
    g, nk, kw, r, lo, bias = cfg['g'], cfg['nk'], cfg['kw'], cfg['r'], cfg['lo'], cfg['bias']
    refs = list(refs)
    kv_refs = [refs.pop(0) for _ in range(g)]
    self_ref = refs.pop(0)
    if bias:
        cq_ref = refs.pop(0)
        cum_refs = [refs.pop(0) for _ in range(g)]
    o_ref, m_ref, l_ref, acc_ref, qhi_ref, qlo_ref = refs
    b, ki = pl.program_id(0), pl.program_id(1)
    mqa = kw == HEAD_DIM
    kwid = LANE if mqa else kw
    lane = lax.broadcasted_iota(I32, (1, LANE), 1)
    emat = _head_expand_matrix()

    @pl.when(ki == 0)
    def _():
        m_ref[...] = jnp.full(m_ref.shape, NEG, F32)
        l_ref[...] = jnp.zeros(l_ref.shape, F32)
        acc_ref[...] = jnp.zeros(acc_ref.shape, F32)
        n = HEADS_W
        qrow = q_ref[0][0:1, :]
        eye = lax.broadcasted_iota(I32, (n, n), 0) == lax.broadcasted_iota(I32, (n, n), 1)
        qcol = jnp.sum(jnp.where(eye, jnp.broadcast_to(qrow, (n, n)), 0.0), axis=1, keepdims=True)
        c = lax.broadcasted_iota(I32, (n, LANE), 0)
        h = lax.broadcasted_iota(I32, (n, LANE), 1)
        qmat = jnp.where(c // HEAD_DIM == h, qcol, 0.0)
        if mqa:
            d = HEAD_DIM
            qmat = jnp.concatenate([qmat[0:d] + qmat[d:2 * d] + qmat[2 * d:3 * d] + qmat[3 * d:],
                                    jnp.zeros((LANE - d, LANE), F32)], axis=0)
        hi, lo_ = _split2(qmat)
        qhi_ref[...] = hi
        qlo_ref[...] = lo_

    def block(rows, n_rows, kpos0, hd, cumcol, n_valid):
        khi, klo = _split2(rows[:, :kwid])
        s = _dot(khi, qhi_ref[...]) + _dot(khi, qlo_ref[...]) + _dot(klo, qhi_ref[...])
        if cumcol is not None:
            s = s + (cq_ref[0][0:1, :] - cumcol)
        kpos = kpos0 + lax.broadcasted_iota(I32, (n_rows, 1), 0)
        ok = jnp.logical_and(kpos >= lo, kpos < kpos0 + n_valid)
        heads = jnp.logical_and(lane < N_HEADS, jnp.logical_or(hd < 0, lane == hd))
        mask = jnp.logical_and(ok, heads)
        if mqa:
            vv = jnp.where(lane < HEAD_DIM, pltpu.roll(rows, HEAD_DIM, 1), rows)
            v = jnp.concatenate([vv, vv], axis=1)
        else:
            v = rows[:, kw:]
        return jnp.where(mask, s, NEG), mask, v

    def update(blocks):
        m_old = m_ref[0:1, :]
        m_new = m_old
        for s, _, _ in blocks:
            m_new = jnp.maximum(m_new, jnp.max(s, axis=0, keepdims=True))
        alpha = jnp.exp(m_old - m_new)
        l_new = alpha * l_ref[0:1, :]
        contrib = jnp.zeros((SUB, HEADS_W), F32)
        for s, mask, v in blocks:
            p = jnp.where(mask, jnp.exp(s - m_new), 0.0)
            l_new = l_new + jnp.sum(p, axis=0, keepdims=True)
            pv = _dot_exact(p, emat) * v
            n_rows = pv.shape[0]
            contrib = contrib + (pv.reshape(n_rows // SUB, SUB, HEADS_W).sum(axis=0)
                                 if n_rows > SUB else pv)
        alpha_e = _dot_exact(jnp.broadcast_to(alpha, (SUB, LANE)), emat)
        acc_ref[...] = alpha_e * acc_ref[...] + contrib
        m_ref[...] = jnp.broadcast_to(m_new, m_ref.shape)
        l_ref[...] = jnp.broadcast_to(l_new, l_ref.shape)

    blocks = []
    for i in range(g):
        j = ki * g + i
        cumcol = None
        if bias:
            ct = jnp.concatenate([cum_refs[i][0], jnp.zeros((LANE - SUB, LANE), F32)], axis=0)
            cumcol = ct.T
        blocks.append(block(kv_refs[i][0], r, ft_ref[b, j] * r, hd_ref[b, j], cumcol, r))
    update(blocks)

    @pl.when(ki == nk - 1)
    def _():
        own = jnp.broadcast_to(cq_ref[0][0:1, :], (SUB, LANE)) if bias else None
        update([block(self_ref[0], SUB, cfg['self_pos'], -1, own, 1)])
        tot = jnp.sum(acc_ref[...], axis=0, keepdims=True)
        l_e = _dot_exact(l_ref[...], emat)[0:1, :]
        o_ref[0] = jnp.broadcast_to(tot / jnp.maximum(l_e, 1e-30), (SUB, HEADS_W))


def _dec_attn_call(q8, pool, page_table, fetch, heads, self8, *, r, g, lo=0, self_pos, cq=None,
                   cum_t=None, name="dec_attn"):
    bsz = q8.shape[0]
    n_pool, _, w2 = pool.shape
    kw = w2 // 2
    rpp = PAGE // r
    nf = fetch.shape[1]
    assert nf % g == 0
    nk = nf // g
    n_blocks = page_table.shape[1] * rpp
    view = pool.reshape(n_pool * rpp, r, w2)
    bias = cq is not None
    cfg = dict(g=g, nk=nk, kw=kw, r=r, lo=lo, bias=bias, self_pos=self_pos)

    def kv_map(i):
        def imap(b, ki, pt, ft, hd):
            blk = jnp.clip(ft[b, ki * g + i], 0, n_blocks - 1)
            return (pt[b, blk // rpp] * rpp + blk % rpp, 0, 0)
        return imap

    def cum_map(i):
        def imap(b, ki, pt, ft, hd):
            return (b, 0, jnp.clip(ft[b, ki * g + i], 0, n_blocks - 1))
        return imap

    fixed = lambda shape: pl.BlockSpec(shape, lambda b, ki, pt, ft, hd: (b, 0, 0))
    args = [q8] + [view] * g + [self8]
    specs = ([fixed((1, SUB, HEADS_W))] + [pl.BlockSpec((1, r, w2), kv_map(i)) for i in range(g)]
             + [fixed((1, SUB, w2))])
    if bias:
        assert r == LANE
        args += [cq] + [cum_t] * g
        specs += [fixed((1, SUB, LANE))] + [pl.BlockSpec((1, SUB, LANE), cum_map(i)) for i in range(g)]
    kwid = LANE if kw == HEAD_DIM else kw
    return pl.pallas_call(
        functools.partial(_dec_attn_kernel, cfg=cfg),
        grid_spec=pltpu.PrefetchScalarGridSpec(
            num_scalar_prefetch=3, grid=(bsz, nk), in_specs=specs,
            out_specs=fixed((1, SUB, HEADS_W)),
            scratch_shapes=[pltpu.VMEM((SUB, LANE), F32), pltpu.VMEM((SUB, LANE), F32),
                            pltpu.VMEM((SUB, HEADS_W), F32), pltpu.VMEM((kwid, LANE), BF16),
                            pltpu.VMEM((kwid, LANE), BF16)]),
        out_shape=jax.ShapeDtypeStruct((bsz, SUB, HEADS_W), F32),
        compiler_params=_cparams(("parallel", "arbitrary")),
        name=name,
    )(page_table, fetch, heads, *args)


MOBA_BLOCKS_PER_STEP = 8


def _kmean_kernel(*refs, g, paged):
    refs = list(refs)
    if paged:
        refs.pop(0)
    out_ref = refs[g]
    rows_per_ref = refs[0].shape[1]
    per = MOBA_BLOCK // rows_per_ref if rows_per_ref < MOBA_BLOCK else 1
    row = lax.broadcasted_iota(I32, (MOBA_BLOCKS_PER_STEP, 1), 0)
    acc = jnp.zeros((MOBA_BLOCKS_PER_STEP, HEADS_W), F32)
    if rows_per_ref < MOBA_BLOCK:
        for i in range(g):
            s = jnp.sum(refs[i][0], axis=0, keepdims=True)
            acc = acc + jnp.where(row == i // per, s, 0.0)
    else:
        for j in range(MOBA_BLOCKS_PER_STEP):
            s = jnp.sum(refs[0][0, j * MOBA_BLOCK:(j + 1) * MOBA_BLOCK, :], axis=0, keepdims=True)
            acc = acc + jnp.where(row == j, s, 0.0)
    out_ref[0] = acc * (1.0 / MOBA_BLOCK)


def _kmean_call(kv, page_table=None):
    paged = page_table is not None
    span = MOBA_BLOCKS_PER_STEP * MOBA_BLOCK
    if paged:
        bsz, n_pages = page_table.shape
        tk = n_pages * PAGE
        g = span // PAGE
        specs = [pl.BlockSpec((1, PAGE, HEADS_W), functools.partial(
            lambda b, j, pt, i: (pt[b, j * g + i], 0, 0), i=i)) for i in range(g)]
        out_spec = pl.BlockSpec((1, MOBA_BLOCKS_PER_STEP, HEADS_W), lambda b, j, pt: (b, j, 0))
    else:
        bsz, tk, _ = kv.shape
        g = 1
        specs = [pl.BlockSpec((1, span, HEADS_W), lambda b, j: (b, j, 0))]
        out_spec = pl.BlockSpec((1, MOBA_BLOCKS_PER_STEP, HEADS_W), lambda b, j: (b, j, 0))
    assert tk % span == 0
    kern = functools.partial(_kmean_kernel, g=g, paged=paged)
    out_shape = jax.ShapeDtypeStruct((bsz, tk // MOBA_BLOCK, HEADS_W), F32)
    sem = ("parallel", "parallel")
    if paged:
        return pl.pallas_call(
            kern, grid_spec=pltpu.PrefetchScalarGridSpec(
                num_scalar_prefetch=1, grid=(bsz, tk // span), in_specs=specs, out_specs=out_spec),
            out_shape=out_shape, compiler_params=_cparams(sem), name="moba_kmean",
        )(page_table, *([kv] * g))
    return pl.pallas_call(kern, grid=(bsz, tk // span), in_specs=specs, out_specs=out_spec,
                          out_shape=out_shape, compiler_params=_cparams(sem), name="moba_kmean")(kv)


def _moba_gate_kernel(q_ref, km_ref, bm_ref, idx_ref, *, tq, q_pos0, k_top, precise):
    qi = pl.program_id(1)
    qpos = q_pos0 + qi * tq + lax.broadcasted_iota(I32, (tq, 1), 0)
    cur = qpos // MOBA_BLOCK
    blk = lax.broadcasted_iota(I32, (1, LANE), 1)
    valid = blk < cur
    q = q_ref[0]
    km = km_ref[0]
    for h in range(N_HEADS):
        kmh = km[:, h * HEAD_DIM:(h + 1) * HEAD_DIM]
        qh = q[:, h * HEAD_DIM:(h + 1) * HEAD_DIM]
        if precise:
            gate = _mm_nt(qh, kmh, True)
        else:
            hi, lo = _split2(kmh)
            gate = _dot_nt(qh, hi) + _dot_nt(qh, lo)
        score = jnp.where(valid, gate, NEG)
        sel = jnp.where(blk == cur, 1.0, 0.0)
        picks = jnp.full((tq, LANE), -1, I32)
        for i in range(k_top):
            m, idx = _first_argmax(score, blk)
            hit = blk == idx
            ok = m > 0.5 * NEG
            sel = jnp.where(jnp.logical_and(hit, ok), 1.0, sel)
            picks = jnp.where(jnp.logical_and(blk == i, ok), idx, picks)
            score = jnp.where(hit, -jnp.inf, score)
        bm_ref[0, :, h * LANE:(h + 1) * LANE] = sel
        idx_ref[0, :, h * LANE:(h + 1) * LANE] = picks


def _moba_gate_call(q, kmean, *, tq, q_pos0, k_top, precise=False):
    bsz, t, _ = q.shape
    wide = pl.BlockSpec((1, tq, N_HEADS * LANE), lambda b, i: (b, i, 0))
    return pl.pallas_call(
        functools.partial(_moba_gate_kernel, tq=tq, q_pos0=q_pos0, k_top=k_top, precise=precise),
        grid=(bsz, t // tq),
        in_specs=[pl.BlockSpec((1, tq, HEADS_W), lambda b, i: (b, i, 0)),
                  pl.BlockSpec((1, LANE, HEADS_W), lambda b, i: (b, 0, 0))],
        out_specs=[wide, wide],
        out_shape=[jax.ShapeDtypeStruct((bsz, t, N_HEADS * LANE), F32),
                   jax.ShapeDtypeStruct((bsz, t, N_HEADS * LANE), I32)],
        compiler_params=_cparams(("parallel", "parallel")),
        name="moba_gate",
    )(q, kmean)


CUM_CHUNK = 512


def _cum_kernel(l_ref, o_ref, carry_ref):
    @pl.when(pl.program_id(1) == 0)
    def _():
        carry_ref[...] = jnp.zeros(carry_ref.shape, F32)

    r = lax.broadcasted_iota(I32, (CUM_CHUNK, CUM_CHUNK), 0)
    c = lax.broadcasted_iota(I32, (CUM_CHUNK, CUM_CHUNK), 1)
    u = jnp.where(r <= c, 1.0, 0.0).astype(BF16)
    h1, h2, h3 = _split3(l_ref[0])
    cum = _dot(h1, u) + _dot(h2, u) + _dot(h3, u) + carry_ref[:, 0:1]
    o_ref[0] = cum
    carry_ref[...] = jnp.broadcast_to(cum[:, CUM_CHUNK - 1:CUM_CHUNK], carry_ref.shape)


def _cum_call(logf_t):
    bsz, rows, tk = logf_t.shape
    assert rows == SUB and tk % CUM_CHUNK == 0
    return pl.pallas_call(
        _cum_kernel,
        grid=(bsz, tk // CUM_CHUNK),
        in_specs=[pl.BlockSpec((1, SUB, CUM_CHUNK), lambda b, j: (b, 0, j))],
        out_specs=pl.BlockSpec((1, SUB, CUM_CHUNK), lambda b, j: (b, 0, j)),
        out_shape=jax.ShapeDtypeStruct(logf_t.shape, F32),
        scratch_shapes=[pltpu.VMEM((SUB, LANE), F32)],
        compiler_params=_cparams(("parallel", "arbitrary")),
        name="fox_cum",
    )(logf_t)


def _group_norm_gate(y, z, norm):
    y = y * _silu(z)
    gw = SSD_INNER // SSD_GROUPS
    outs = []
    for g in range(SSD_GROUPS):
        yg = y[:, g * gw:(g + 1) * gw]
        ms = jnp.mean(yg * yg, axis=-1, keepdims=True)
        outs.append(yg * lax.rsqrt(ms + EPS) * norm[:, g * gw:(g + 1) * gw])
    return jnp.concatenate(outs, axis=-1)


def _ssd_kernel(xbc_ref, z_ref, dt_ref, convw_ref, convb_ref, hp_ref, norm_ref, y_ref, state_ref,
                ext_ref, h_ref, *, nc, precise):
    q = SSD_CHUNK
    ci = pl.program_id(1)

    @pl.when(ci == 0)
    def _():
        ext_ref[0:SUB, :] = jnp.zeros((SUB, SSD_CONV_DIM), F32)
        h_ref[...] = jnp.zeros(h_ref.shape, F32)

    u = xbc_ref[0]
    ext_ref[SUB:SUB + q, :] = u
    acc = jnp.broadcast_to(convb_ref[...], (q, SSD_CONV_DIM))
    for i in range(SSD_CONV):
        k = SSD_CONV - 1 - i
        acc = acc + ext_ref[SUB - k:SUB - k + q, :] * convw_ref[i:i + 1, :]
    ext_ref[0:SUB, :] = u[q - SUB:, :]
    xbc = _silu(acc)
    xs = xbc[:, :SSD_INNER]
    dt = dt_ref[0]
    a = -jnp.exp(hp_ref[0:1, :])
    d_skip = hp_ref[1:2, :]
    row = lax.broadcasted_iota(I32, (q, 1), 0)
    cum = dt * a
    s = 1
    while s < q:
        cum = cum + jnp.where(row >= s, pltpu.roll(cum, s, 0), 0.0)
        s *= 2
    cum_t = cum.T
    tri = lax.broadcasted_iota(I32, (q, q), 0) >= lax.broadcasted_iota(I32, (q, q), 1)
    ys, xdd = [], []
    cbs = []
    for g in range(SSD_GROUPS):
        bm = xbc[:, SSD_INNER + g * SSD_STATE:SSD_INNER + (g + 1) * SSD_STATE]
        cm = xbc[:, SSD_INNER + SSD_BC + g * SSD_STATE:SSD_INNER + SSD_BC + (g + 1) * SSD_STATE]
        cbs.append((bm, cm, _mm_nt(cm, bm, precise)))
    rep = SSD_HEADS // SSD_GROUPS
    for h in range(SSD_HEADS):
        bm, cm, cb = cbs[h // rep]
        ch = cum[:, h:h + 1]
        lmat = jnp.where(tri, jnp.exp(ch - cum_t[h:h + 1, :]), 0.0)
        xh = xs[:, h * SSD_HEAD_DIM:(h + 1) * SSD_HEAD_DIM]
        xdt = xh * dt[:, h:h + 1]
        y = _mm(cb * lmat, xdt, precise)
        y = y + jnp.exp(ch) * _mm_nt(cm, h_ref[h], precise)
        ys.append(y + d_skip[:, h:h + 1] * xh)
        xdd.append(xdt * jnp.exp(cum[q - 1:q, h:h + 1] - ch))
    xdd_t = jnp.concatenate(xdd, axis=-1).T
    for h in range(SSD_HEADS):
        bm = cbs[h // rep][0]
        upd = _mm(xdd_t[h * SSD_HEAD_DIM:(h + 1) * SSD_HEAD_DIM, :], bm, precise)
        h_ref[h] = jnp.exp(cum[q - 1:q, h:h + 1]) * h_ref[h] + upd
    y_ref[0] = _group_norm_gate(jnp.concatenate(ys, axis=-1), z_ref[0], norm_ref[...])

    @pl.when(ci == nc - 1)
    def _():
        state_ref[0] = h_ref[...]


def _ssd_call(xbc, z, dt, convw, convb, hp, norm, precise=False):
    bsz, t, _ = xbc.shape
    nc = t // SSD_CHUNK
    blk = lambda w: pl.BlockSpec((1, SSD_CHUNK, w), lambda b, c: (b, c, 0))
    full = lambda a: pl.BlockSpec(a.shape, lambda b, c: (0,) * a.ndim)
    return pl.pallas_call(
        functools.partial(_ssd_kernel, nc=nc, precise=precise),
        grid=(bsz, nc),
        in_specs=[blk(SSD_CONV_DIM), blk(SSD_INNER), blk(LANE), full(convw), full(convb), full(hp),
                  full(norm)],
        out_specs=[blk(SSD_INNER),
                   pl.BlockSpec((1, SSD_HEADS, SSD_HEAD_DIM, SSD_STATE), lambda b, c: (b, 0, 0, 0))],
        out_shape=[jax.ShapeDtypeStruct((bsz, t, SSD_INNER), F32),
                   jax.ShapeDtypeStruct((bsz, SSD_HEADS, SSD_HEAD_DIM, SSD_STATE), F32)],
        scratch_shapes=[pltpu.VMEM((SUB + SSD_CHUNK, SSD_CONV_DIM), F32),
                        pltpu.VMEM((SSD_HEADS, SSD_HEAD_DIM, SSD_STATE), F32)],
        compiler_params=_cparams(("parallel", "arbitrary")),
        name="ssd_scan",
    )(xbc, z, dt, convw, convb, hp, norm)


def _ssd_step_kernel(full_ref, z_ref, dt_ref, st_ref, w8_ref, convb_ref, hp_ref, norm_ref,
                     y_ref, state_ref):
    acc = convb_ref[...] + jnp.sum(full_ref[0] * w8_ref[...], axis=0, keepdims=True)
    xbc = _silu(acc)
    xs = xbc[:, :SSD_INNER]
    dt = dt_ref[0]
    a = -jnp.exp(hp_ref[0:1, :])
    d_skip = hp_ref[1:2, :]
    n = SSD_HEAD_DIM
    eye = lax.broadcasted_iota(I32, (n, n), 0) == lax.broadcasted_iota(I32, (n, n), 1)
    rep = SSD_HEADS // SSD_GROUPS
    ys = []
    for h in range(SSD_HEADS):
        g = h // rep
        bm = xbc[:, SSD_INNER + g * SSD_STATE:SSD_INNER + (g + 1) * SSD_STATE]
        cm = xbc[:, SSD_INNER + SSD_BC + g * SSD_STATE:SSD_INNER + SSD_BC + (g + 1) * SSD_STATE]
        dth = dt[:, h:h + 1]
        xh = xs[:, h * n:(h + 1) * n]
        xcol = jnp.sum(jnp.where(eye, jnp.broadcast_to(xh * dth, (n, n)), 0.0), axis=1, keepdims=True)
        hn = jnp.exp(dth * a[:, h:h + 1]) * st_ref[0, h] + xcol * bm
        state_ref[0, h] = hn
        ycol = jnp.sum(hn * cm, axis=1, keepdims=True)
        yrow = jnp.sum(jnp.where(eye, jnp.broadcast_to(ycol, (n, n)), 0.0), axis=0, keepdims=True)
        ys.append(yrow + d_skip[:, h:h + 1] * xh)
    y_ref[0] = _group_norm_gate(jnp.concatenate(ys, axis=-1), z_ref[0], norm_ref[...])


def _ssd_step_call(full8, z, dt, state, w8, convb, hp, norm):
    bsz = full8.shape[0]
    one = lambda w: pl.BlockSpec((1, 1, w), lambda b: (b, 0, 0))
    full = lambda a: pl.BlockSpec(a.shape, lambda b: (0,) * a.ndim)
    st = pl.BlockSpec((1, SSD_HEADS, SSD_HEAD_DIM, SSD_STATE), lambda b: (b, 0, 0, 0))
    return pl.pallas_call(
        _ssd_step_kernel,
        grid=(bsz,),
        in_specs=[pl.BlockSpec((1, SUB, SSD_CONV_DIM), lambda b: (b, 0, 0)), one(SSD_INNER), one(LANE),
                  st, full(w8), full(convb), full(hp), full(norm)],
        out_specs=[one(SSD_INNER), st],
        out_shape=[jax.ShapeDtypeStruct((bsz, 1, SSD_INNER), F32),
                   jax.ShapeDtypeStruct(state.shape, F32)],
        compiler_params=_cparams(("parallel",)),
        name="ssd_step",
    )(full8, z, dt, state, w8, convb, hp, norm)


R_GROUP0 = MOE_EXPERTS


def _post_kernel(x_ref, gmix_ref, wg_ref, small_ref, ocmp_ref, oslc_ref, owin_ref, omoba_ref,
                 ofox_ref, ossd_ref, wbr_ref, wout_ref, gffn_ref, wr_ref, br_ref,
                 x1_ref, xn2_ref, comb_ref, *, precise):
    x = x_ref[...]
    xn = x * lax.rsqrt(jnp.mean(x * x, axis=-1, keepdims=True) + EPS) * gmix_ref[...]
    xn = xn if precise else xn.astype(BF16)
    r = lax.broadcasted_iota(I32, (LANE, 3 * HEADS_W), 0)
    c = lax.broadcasted_iota(I32, (LANE, 3 * HEADS_W), 1)
    pick = r == SM_NSAG + 3 * ((c % HEADS_W) // HEAD_DIM) + c // HEADS_W
    pick = jnp.where(pick, 1.0, 0.0).astype(BF16)
    gexp = _dot_exact(_sigmoid(small_ref[...]), pick)
    o_nsa = (gexp[:, :HEADS_W] * ocmp_ref[...] + gexp[:, HEADS_W:2 * HEADS_W] * oslc_ref[...]
             + gexp[:, 2 * HEADS_W:] * owin_ref[...])
    merged = jnp.zeros(x.shape, F32)
    off = 0
    for b, o in enumerate((o_nsa, omoba_ref[...], ofox_ref[...], ossd_ref[...])):
        w = o.shape[1]
        gate = _sigmoid(_mm(xn, wg_ref[:, b * D_MODEL:(b + 1) * D_MODEL], precise))
        merged = merged + gate * _mm(o, wbr_ref[off:off + w, :], precise)
        off += w
    x1 = x + _mm(merged, wout_ref[...], precise)
    x1_ref[...] = x1
    xn2 = x1 * lax.rsqrt(jnp.mean(x1 * x1, axis=-1, keepdims=True) + EPS) * gffn_ref[...]
    xn2_ref[...] = xn2.astype(xn2_ref.dtype)
    if precise:
        logits = _mm(xn2, wr_ref[0], True) + br_ref[...]
    else:
        hi, lo = _split2(xn2)
        logits = _dot(hi, wr_ref[0]) + _dot(lo, wr_ref[0]) + _dot(hi, wr_ref[1]) + br_ref[...]
    lane = lax.broadcasted_iota(I32, (1, LANE), 1)
    is_grp = jnp.logical_and(lane >= R_GROUP0, lane < R_GROUP0 + MOE_GROUPS)
    gmax, gidx = _first_argmax(jnp.where(is_grp, logits, -jnp.inf), lane)
    g_w = 1.0 / jnp.sum(jnp.where(is_grp, jnp.exp(logits - gmax), 0.0), axis=-1, keepdims=True)
    in_grp = lane // MOE_PER_GROUP == gidx - R_GROUP0
    e1 = jnp.where(in_grp, logits, -jnp.inf)
    v1, i1 = _first_argmax(e1, lane)
    e2 = jnp.where(lane == i1, -jnp.inf, e1)
    v2, i2 = _first_argmax(e2, lane)
    t = jnp.exp(v2 - v1)
    w1 = 1.0 / (1.0 + t)
    comb_ref[...] = (jnp.where(lane == i1, w1 * g_w, 0.0) + jnp.where(lane == i2, t * w1 * g_w, 0.0))


def _post_call(x2, gmix, wg, small, o_cmp, o_slc, o_win, o_moba, o_fox, o_ssd, wbr, wout, gffn,
               wr, br, tm, precise=False):
    n = x2.shape[0]
    row = lambda a: pl.BlockSpec((tm, a.shape[1]), lambda i: (i, 0))
    full = lambda a: pl.BlockSpec(a.shape, lambda i: (0,) * a.ndim)
    big = (lambda a: pl.BlockSpec(a.shape, lambda i: (0,) * a.ndim, pipeline_mode=pl.Buffered(1))
           ) if precise else full
    rows = (x2, small, o_cmp, o_slc, o_win, o_moba, o_fox, o_ssd)
    return pl.pallas_call(
        functools.partial(_post_kernel, precise=precise),
        grid=(n // tm,),
        in_specs=[row(x2), full(gmix), big(wg)] + [row(a) for a in rows[1:]]
        + [big(wbr), big(wout), full(gffn), full(wr), full(br)],
        out_specs=[pl.BlockSpec((tm, D_MODEL), lambda i: (i, 0)),
                   pl.BlockSpec((tm, D_MODEL), lambda i: (i, 0)),
                   pl.BlockSpec((tm, LANE), lambda i: (i, 0))],
        out_shape=[jax.ShapeDtypeStruct((n, D_MODEL), F32),
                   jax.ShapeDtypeStruct((n, D_MODEL), F32 if precise else BF16),
                   jax.ShapeDtypeStruct((n, LANE), F32)],
        compiler_params=_cparams(("parallel",)),
        name="merge_out",
    )(x2, gmix, wg, small, o_cmp, o_slc, o_win, o_moba, o_fox, o_ssd, wbr, wout, gffn, wr, br)


def _moe_kernel(xn_ref, x1_ref, comb_ref, wg_ref, wu_ref, wd_ref, o_ref, acc_ref, *, precise):
    e = pl.program_id(1)

    @pl.when(e == 0)
    def _():
        acc_ref[...] = x1_ref[...]

    lane = lax.broadcasted_iota(I32, (1, LANE), 1)
    cw = jnp.sum(jnp.where(lane == e, comb_ref[...], 0.0), axis=-1, keepdims=True)
    xn = xn_ref[...]
    h = _silu(_mm(xn, wg_ref[0], precise)) * _mm(xn, wu_ref[0], precise)
    acc_ref[...] += cw * _mm(h, wd_ref[0], precise)

    @pl.when(e == MOE_EXPERTS - 1)
    def _():
        o_ref[...] = acc_ref[...]


def _moe_call(xn2, x1, comb, wg, wu, wd, tm, precise=False):
    n = xn2.shape[0]
    return pl.pallas_call(
        functools.partial(_moe_kernel, precise=precise),
        grid=(n // tm, MOE_EXPERTS),
        in_specs=[pl.BlockSpec((tm, D_MODEL), lambda i, e: (i, 0)),
                  pl.BlockSpec((tm, D_MODEL), lambda i, e: (i, 0)),
                  pl.BlockSpec((tm, LANE), lambda i, e: (i, 0)),
                  pl.BlockSpec((1, D_MODEL, MOE_HIDDEN), lambda i, e: (e, 0, 0)),
                  pl.BlockSpec((1, D_MODEL, MOE_HIDDEN), lambda i, e: (e, 0, 0)),
                  pl.BlockSpec((1, MOE_HIDDEN, D_MODEL), lambda i, e: (e, 0, 0))],
        out_specs=pl.BlockSpec((tm, D_MODEL), lambda i, e: (i, 0)),
        out_shape=jax.ShapeDtypeStruct((n, D_MODEL), F32),
        scratch_shapes=[pltpu.VMEM((tm, D_MODEL), F32)],
        compiler_params=_cparams(("parallel", "arbitrary")),
        name="moe_experts",
    )(xn2, x1, comb, wg, wu, wd)


def _layer_weights(prm):
    wp, wgate = _pack_w_in(prm['w_in'])
    wr = jnp.concatenate([prm['w_router_exp'], prm['w_router_grp'],
                          jnp.zeros((D_MODEL, LANE - MOE_EXPERTS - MOE_GROUPS), F32)], axis=1)
    wr_hi = wr.astype(BF16)
    wr_lo = (wr - wr_hi.astype(F32)).astype(BF16)
    br = jnp.concatenate([prm['b_router_exp'], prm['b_router_grp'],
                          jnp.zeros((LANE - MOE_EXPERTS - MOE_GROUPS,), F32)])[None, :]
    pad = lambda v: jnp.concatenate([v.astype(F32), jnp.zeros((LANE - v.shape[0],), F32)])
    hp = jnp.stack([pad(prm['ssd_a_log']), pad(prm['ssd_d'])] + [jnp.zeros((LANE,), F32)] * 6)
    w8 = jnp.concatenate([prm['ssd_conv_w'], jnp.zeros((SUB - SSD_CONV, SSD_CONV_DIM), F32)], axis=0)
    wbr = jnp.concatenate([prm['w_br_nsa'], prm['w_br_moba'], prm['w_br_fox'], prm['w_br_ssd']], axis=0)
    shared = dict(gmix=prm['norm_mix'][None, :], gffn=prm['norm_ffn'][None, :], gains=_proj_gains(prm),
                  biases=_proj_biases(prm), br=br, hp=hp, w8=w8, convw=prm['ssd_conv_w'],
                  convb=prm['ssd_conv_b'][None, :], norm=prm['ssd_norm'][None, :])
    fast = dict(shared, wp=wp.astype(BF16), wgate=wgate.astype(BF16), cmp=_cmp_weights(prm, BF16),
                wbr=wbr.astype(BF16), wout=prm['w_out'].astype(BF16), wr=jnp.stack([wr_hi, wr_lo]),
                weg=prm['w_exp_gate'].astype(BF16), weu=prm['w_exp_up'].astype(BF16),
                wed=prm['w_exp_down'].astype(BF16))
    exact = dict(shared, wp=wp, wgate=wgate, cmp=_cmp_weights(prm, F32), wbr=wbr, wout=prm['w_out'],
                 wr=wr[None], weg=prm['w_exp_gate'], weu=prm['w_exp_up'], wed=prm['w_exp_down'])
    return fast, exact


def _pad_rows(a, rows):
    return jnp.pad(a, ((0, 0), (0, rows - a.shape[1])) + ((0, 0),) * (a.ndim - 2))


def _heads_to_sublanes(logf, tk_pad):
    lt = jnp.swapaxes(logf, 1, 2)
    return jnp.pad(lt, ((0, 0), (0, SUB - lt.shape[1]), (0, tk_pad - lt.shape[2])))


def _layer_prompt(x, w, w_ffn, *, tm, tq, tk, precise):
    bsz, t, _ = x.shape
    n = bsz * t
    x2 = x.reshape(n, D_MODEL)
    tabs = _rope_tables(jnp.arange(t, dtype=I32))
    (qn, cmp_rows, slc_rows, win_rows, qm, moba_rows, qf, fox_rows, z, xbc, small, dt) = _proj_call(
        x2, w['gmix'], w['wp'], w['gains'], tabs, w['biases'], tm, precise)
    b3 = lambda a: a.reshape(bsz, t, a.shape[-1])
    qn, qm, qf = b3(qn), b3(qm), b3(qf)
    kvc = _cmp_mlp_call(cmp_rows.reshape(bsz, t // NSA_CMP_STRIDE, NSA_CMP_STRIDE * LANE), *w['cmp'],
                        precise=precise)
    n_cmp = (t - NSA_CMP_LEN) // NSA_CMP_STRIDE + 1
    n_sel = -(-t // NSA_SEL_BLOCK)
    nselp = -(-n_sel // LANE) * LANE
    o_cmp, sel, _ = _nsa_cmp_call(qn, kvc, tq=tq, q_pos0=0, n_cmp=n_cmp, n_sel=n_sel, nselp=nselp,
                                  precise=precise)
    o_slc = _attn_call(qn, b3(slc_rows), tq=tq, tkb=tk, bm=sel, bm_bs=NSA_SEL_BLOCK, precise=precise,
                       name="nsa_slc")
    o_win = _attn_call(qn, b3(win_rows), tq=tq, tkb=tk, window=NSA_WINDOW, precise=precise,
                       name="nsa_win")
    moba3 = b3(moba_rows)
    kmean = _pad_rows(_kmean_call(moba3), LANE)
    n_blk = -(-t // MOBA_BLOCK)
    bm, _ = _moba_gate_call(qm, kmean, tq=tq, q_pos0=0, k_top=min(MOBA_TOPK, n_blk - 1),
                            precise=precise)
    o_moba = _attn_call(qm, moba3, tq=tq, tkb=tk, bm=bm, bm_bs=MOBA_BLOCK, bm_hs=LANE, precise=precise,
                        name="moba")
    logf = small[:, SM_FOXF:SM_FOXF + N_HEADS].reshape(bsz, t, N_HEADS)
    cum_t = _cum_call(_heads_to_sublanes(logf, t))
    cq = jnp.pad(jnp.swapaxes(cum_t, 1, 2), ((0, 0), (0, 0), (0, LANE - SUB)))
    o_fox = _attn_call(qf, b3(fox_rows), tq=tq, tkb=tk, cq=cq, cum_t=cum_t, precise=precise, name="fox")
    y_ssd, ssd_state = _ssd_call(b3(xbc), b3(z), b3(dt), w['convw'], w['convb'], w['hp'], w['norm'],
                                 precise)
    f2 = lambda a: a.reshape(n, a.shape[-1])
    x1, xn2, comb = _post_call(x2, w['gmix'], w['wgate'], small, f2(o_cmp), f2(o_slc), f2(o_win),
                               f2(o_moba), f2(o_fox), f2(y_ssd), w['wbr'], w['wout'], w['gffn'],
                               w['wr'], w['br'], tm, precise)
    y = _moe_call(xn2, x1, comb, w_ffn['weg'], w_ffn['weu'], w_ffn['wed'], min(n, 4 * tm))
    wb = min(NSA_WINDOW, t)
    state = (cmp_rows.reshape(bsz, t, 2, HEAD_DIM), slc_rows.reshape(bsz, t, 2, HEAD_DIM),
             win_rows.reshape(bsz, t, 2, HEAD_DIM)[:, t - wb:],
             moba_rows.reshape(bsz, t, 2, N_HEADS, HEAD_DIM), fox_rows.reshape(bsz, t, 2, N_HEADS, HEAD_DIM),
             logf, ssd_state, b3(xbc)[:, t - (SSD_CONV - 1):])
    return y.reshape(bsz, t, D_MODEL), state


def _layer_sample(x, w, past, page_table, *, g_pages):
    c_cmp, c_slc, s_win, c_moba, c_fox, c_logf, s_ssd, s_conv = past
    bsz = x.shape[0]
    n_pages = page_table.shape[1]
    pos0 = n_pages * PAGE
    n_pool = c_cmp.shape[0]
    x2 = x.reshape(bsz, D_MODEL)
    tabs = _rope_tables(jnp.full((bsz,), pos0, I32))
    (qn, cmp_rows, slc_rows, win_rows, qm, moba_rows, qf, fox_rows, z, xbc, small, dt) = _proj_call(
        x2, w['gmix'], w['wp'], w['gains'], tabs, w['biases'], bsz, precise=True)
    q8 = lambda a: _pad_rows(a[:, None, :], SUB)
    qn, qm, qf = q8(qn), q8(qm), q8(qf)
    every_head = lambda n: jnp.full((bsz, n), -1, I32)
    assert (pos0 + 1 - NSA_CMP_LEN) // NSA_CMP_STRIDE + 1 == pos0 // NSA_CMP_STRIDE - 1
    rows16 = _paged_gather(c_cmp.reshape(n_pool, PAGE // NSA_CMP_STRIDE, NSA_CMP_STRIDE * LANE),
                           page_table, g_pages)
    kvc = _cmp_mlp_call(rows16, *w['cmp'], precise=True)
    n_cmp = pos0 // NSA_CMP_STRIDE - 1
    n_sel = -(-(pos0 + 1) // NSA_SEL_BLOCK)
    nselp = -(-n_sel // LANE) * LANE
    o_cmp, _, sel_idx = _nsa_cmp_call(qn, kvc, tq=SUB, q_pos0=pos0, n_cmp=n_cmp, n_sel=n_sel,
                                      nselp=nselp, precise=True)
    k_top = min(NSA_SEL_TOPK, n_sel)
    sel_ids = sel_idx[:, 0, :k_top]
    sel_heads = jnp.where(sel_ids < pos0 // NSA_SEL_BLOCK, -1, N_HEADS)
    o_slc = _dec_attn_call(qn, c_slc.reshape(n_pool, PAGE, LANE), page_table, sel_ids, sel_heads,
                           q8(slc_rows), r=NSA_SEL_BLOCK, g=k_top, self_pos=pos0, name="nsa_slc_s")
    wb = s_win.shape[1]
    nwp = wb // PAGE
    win_pages = jnp.arange(bsz * nwp, dtype=I32).reshape(bsz, nwp)
    win_fetch = jnp.tile(jnp.arange(nwp, dtype=I32)[None, :], (bsz, 1))
    o_win = _dec_attn_call(qn, s_win.reshape(bsz * nwp, PAGE, LANE), win_pages, win_fetch, every_head(nwp),
                           q8(win_rows), r=PAGE, g=nwp, lo=wb - NSA_WINDOW + 1, self_pos=wb,
                           name="nsa_win_s")
    moba_pool = c_moba.reshape(n_pool, PAGE, 2 * HEADS_W)
    kmean = _pad_rows(_kmean_call(moba_pool, page_table), LANE)
    n_blk = -(-(pos0 + 1) // MOBA_BLOCK)
    m_top = min(MOBA_TOPK, n_blk - 1)
    _, blk_idx = _moba_gate_call(qm, kmean, tq=SUB, q_pos0=pos0, k_top=m_top, precise=True)
    blk_ids = blk_idx[:, 0].reshape(bsz, N_HEADS, LANE)[:, :, :m_top]
    ppb = MOBA_BLOCK // PAGE
    moba_fetch = (blk_ids[..., None] * ppb + jnp.arange(ppb, dtype=I32)).reshape(bsz, -1)
    head_id = jnp.arange(N_HEADS, dtype=I32)[None, :, None, None]
    moba_heads = jnp.broadcast_to(jnp.where(blk_ids[..., None] >= 0, head_id, N_HEADS),
                                  blk_ids.shape + (ppb,)).reshape(bsz, -1)
    o_moba = _dec_attn_call(qm, moba_pool, page_table, moba_fetch, moba_heads, q8(moba_rows), r=PAGE,
                            g=m_top * ppb, self_pos=pos0, name="moba_s")
    logf_new = small[:, SM_FOXF:SM_FOXF + N_HEADS]
    logf_past = _paged_gather(c_logf.reshape(n_pool, 1, PAGE * N_HEADS), page_table, g_pages)
    logf_all = jnp.concatenate([logf_past.reshape(bsz, pos0, N_HEADS), logf_new[:, None, :]], axis=1)
    cum_t = _cum_call(_heads_to_sublanes(logf_all, pos0 + CUM_CHUNK))
    cq = jnp.pad(jnp.swapaxes(cum_t[:, :, pos0:pos0 + 1], 1, 2), ((0, 0), (0, SUB - 1), (0, LANE - SUB)))
    fox_fetch = jnp.tile(jnp.arange(n_pages, dtype=I32)[None, :], (bsz, 1))
    o_fox = _dec_attn_call(qf, c_fox.reshape(n_pool, PAGE, 2 * HEADS_W), page_table, fox_fetch,
                           every_head(n_pages), q8(fox_rows), r=PAGE, g=g_pages, self_pos=pos0, cq=cq,
                           cum_t=cum_t, name="fox_s")
    full = jnp.concatenate([s_conv, xbc[:, None, :]], axis=1)
    y_ssd, ssd_state = _ssd_step_call(_pad_rows(full, SUB), z[:, None, :], dt[:, None, :], s_ssd,
                                      w['w8'], w['convb'], w['hp'], w['norm'])
    first = lambda a: a[:, 0, :]
    x1, xn2, comb = _post_call(x2, w['gmix'], w['wgate'], small, first(o_cmp), first(o_slc), first(o_win),
                               first(o_moba), first(o_fox), first(y_ssd), w['wbr'], w['wout'],
                               w['gffn'], w['wr'], w['br'], bsz, precise=True)
    y = _moe_call(xn2, x1, comb, w['weg'], w['weu'], w['wed'], bsz, precise=True)
    state = (cmp_rows.reshape(bsz, 1, 2, HEAD_DIM), slc_rows.reshape(bsz, 1, 2, HEAD_DIM),
             jnp.concatenate([s_win[:, 1:], win_rows.reshape(bsz, 1, 2, HEAD_DIM)], axis=1),
             moba_rows.reshape(bsz, 1, 2, N_HEADS, HEAD_DIM), fox_rows.reshape(bsz, 1, 2, N_HEADS, HEAD_DIM),
             logf_new[:, None, :], ssd_state, full[:, 1:])
    return y.reshape(bsz, 1, D_MODEL), state


_PARAM_NAMES = ('norm_mix', 'norm_ffn', 'w_in', 'g_nsa_q', 'g_nsa_k', 'g_nsa_kc', 'pe_cmp_k', 'pe_cmp_v',
                'w_cmp_k1', 'w_cmp_k2', 'w_cmp_v1', 'w_cmp_v2', 'g_moba_q', 'g_moba_k', 'g_fox_q',
                'g_fox_k', 'b_fox_f', 'ssd_conv_w', 'ssd_conv_b', 'ssd_dt_bias', 'ssd_a_log', 'ssd_d',
                'ssd_norm', 'w_br_nsa', 'w_br_moba', 'w_br_fox', 'w_br_ssd', 'w_out', 'w_router_grp',
                'b_router_grp', 'w_router_exp', 'b_router_exp', 'w_exp_gate', 'w_exp_up', 'w_exp_down')


def kernel(x_prompt, x_sample, cache_nsa_cmp, cache_nsa_slc, state_nsa_win, cache_moba, cache_fox, cache_fox_logf, state_ssd, state_ssd_conv, page_table, norm_mix, norm_ffn, w_in, g_nsa_q, g_nsa_k, g_nsa_kc, pe_cmp_k, pe_cmp_v, w_cmp_k1, w_cmp_k2, w_cmp_v1, w_cmp_v2, g_moba_q, g_moba_k, g_fox_q, g_fox_k, b_fox_f, ssd_conv_w, ssd_conv_b, ssd_dt_bias, ssd_a_log, ssd_d, ssd_norm, w_br_nsa, w_br_moba, w_br_fox, w_br_ssd, w_out, w_router_grp, b_router_grp, w_router_exp, b_router_exp, w_exp_gate, w_exp_up, w_exp_down):
    params = dict(zip(_PARAM_NAMES, (
        norm_mix, norm_ffn, w_in, g_nsa_q, g_nsa_k, g_nsa_kc, pe_cmp_k, pe_cmp_v, w_cmp_k1, w_cmp_k2,
        w_cmp_v1, w_cmp_v2, g_moba_q, g_moba_k, g_fox_q, g_fox_k, b_fox_f, ssd_conv_w, ssd_conv_b,
        ssd_dt_bias, ssd_a_log, ssd_d, ssd_norm, w_br_nsa, w_br_moba, w_br_fox, w_br_ssd, w_out,
        w_router_grp, b_router_grp, w_router_exp, b_router_exp, w_exp_gate, w_exp_up, w_exp_down)))
    depth = norm_mix.shape[0]
    caches = (cache_nsa_cmp, cache_nsa_slc, state_nsa_win, cache_moba, cache_fox, cache_fox_logf,
              state_ssd, state_ssd_conv)
    xp, xs = x_prompt, x_sample
    sp, ss = [], []
    for l in range(depth):
        w_fast, w_exact = _layer_weights({k: v[l] for k, v in params.items()})
        feeds_next = l < depth - 1
        xp, st_p = _layer_prompt(xp, w_exact if feeds_next else w_fast, w_fast, tm=256, tq=256, tk=512,
                                 precise=feeds_next)
        xs, st_s = _layer_sample(xs, w_exact, tuple(c[l] for c in caches), page_table, g_pages=16)
        sp.append(st_p)
        ss.append(st_s)
    outs = [xp, xs]
    for i in range(8):
        outs.append(jnp.stack([s[i] for s in sp], axis=0))
        outs.append(jnp.stack([s[i] for s in ss], axis=0))
    return tuple(outs)
```

```python
import functools
import math

import jax
import jax.numpy as jnp
from jax import lax
from jax.experimental import pallas as pl
from jax.experimental.pallas import tpu as pltpu

F32 = jnp.float32
BF16 = jnp.bfloat16
I32 = jnp.int32

D_MODEL = 1024
HEAD_DIM = 64
N_HEADS = 4
HEADS_W = N_HEADS * HEAD_DIM
ROPE_DIM = HEAD_DIM // 4
ROPE_HALF = ROPE_DIM // 2
ROPE_THETA = 500000.0
SCALE = HEAD_DIM ** -0.5
EPS = 1e-6
NEG = -1e30
PAGE = 128

NSA_CMP_LEN = 32
NSA_CMP_STRIDE = 16
NSA_CMP_HIDDEN = 2 * HEAD_DIM
NSA_SEL_BLOCK = 64
NSA_SEL_TOPK = 16
NSA_WINDOW = 512
MOBA_BLOCK = 256
MOBA_TOPK = 3

SSD_HEADS = 8
SSD_HEAD_DIM = 64
SSD_INNER = SSD_HEADS * SSD_HEAD_DIM
SSD_STATE = 64
SSD_GROUPS = 2
SSD_CONV = 4
SSD_CHUNK = 128
SSD_BC = SSD_GROUPS * SSD_STATE
SSD_CONV_DIM = SSD_INNER + 2 * SSD_BC

MOE_GROUPS = 4
MOE_PER_GROUP = 4
MOE_EXPERTS = 16
MOE_HIDDEN = 512

LANE = 128
SUB = 8
VMEM_LIMIT = 56 * 1024 * 1024

_IN_SIZES = (HEADS_W, 6 * HEAD_DIM, 3 * N_HEADS, 3 * HEADS_W, 3 * HEADS_W, N_HEADS,
             2 * SSD_INNER + 2 * SSD_BC + SSD_HEADS, 4 * D_MODEL)
_IN_OFFS = tuple(int(sum(_IN_SIZES[:i])) for i in range(len(_IN_SIZES) + 1))

P_NSA_Q = 0
P_NSA_KV = 256
P_MOBA = 640
P_FOX = 1408
P_SSD = 2176
P_SMALL = 3456
P_DT = 3584
P_WIDTH = 3712
SM_FOXF = 0
SM_NSAG = 4


def _cparams(sem):
    return pltpu.CompilerParams(dimension_semantics=sem, vmem_limit_bytes=VMEM_LIMIT)


def _sigmoid(x):
    return 1.0 / (1.0 + jnp.exp(-x))


def _silu(x):
    return x * _sigmoid(x)


def _softplus(x):
    return jnp.maximum(x, 0.0) + jnp.log(1.0 + jnp.exp(-jnp.abs(x)))


def _dot(a, b):
    return jnp.dot(a, b, preferred_element_type=F32)


def _dot_nt(a, b):
    return lax.dot_general(a, b, (((1,), (1,)), ((), ())), preferred_element_type=F32)


def _split2(x):
    hi = x.astype(BF16)
    lo = (x - hi.astype(F32)).astype(BF16)
    return hi, lo


def _split3(x):
    h1 = x.astype(BF16)
    r = x - h1.astype(F32)
    h2 = r.astype(BF16)
    h3 = (r - h2.astype(F32)).astype(BF16)
    return h1, h2, h3


def _mm(a, b, precise):
    if precise:
        ah, al = _split2(a)
        bh, bl = _split2(b)
        return _dot(ah, bh) + _dot(ah, bl) + _dot(al, bh)
    return _dot(a.astype(BF16), b.astype(BF16))


def _mm_nt(a, b, precise):
    if precise:
        ah, al = _split2(a)
        bh, bl = _split2(b)
        return _dot_nt(ah, bh) + _dot_nt(ah, bl) + _dot_nt(al, bh)
    return _dot_nt(a.astype(BF16), b.astype(BF16))


def _dot_exact(x, e):
    h1, h2, h3 = _split3(x)
    return _dot(h1, e) + _dot(h2, e) + _dot(h3, e)


def _head_sumsq(y, precise=False):
    r = lax.broadcasted_iota(I32, (LANE, LANE), 0) // HEAD_DIM
    c = lax.broadcasted_iota(I32, (LANE, LANE), 1) // HEAD_DIM
    e = jnp.where(r == c, 1.0, 0.0).astype(BF16)
    if precise:
        return _dot_exact(y * y, e)
    hi, lo = _split2(y * y)
    return _dot(hi, e) + _dot(lo, e)


def _rope_chunk(y, cos_t, sin_a, sin_b):
    return (y * cos_t + pltpu.roll(y, LANE - ROPE_HALF, 1) * sin_a
            + pltpu.roll(y, ROPE_HALF, 1) * sin_b)


def _proj_kernel(x_ref, gmix_ref, w_ref, gains_ref, tab_ref, bias_ref,
                 qn_ref, cmp_ref, slc_ref, win_ref, qm_ref, moba_ref, qf_ref, fox_ref,
                 z_ref, xbc_ref, small_ref, dt_ref, *, precise):
    x = x_ref[...]
    ms = jnp.mean(x * x, axis=-1, keepdims=True)
    xn = x * lax.rsqrt(ms + EPS) * gmix_ref[...]
    xn = xn if precise else xn.astype(BF16)
    cos_t = tab_ref[0]
    sin_a = tab_ref[1]
    sin_b = tab_ref[2]
    lane = lax.broadcasted_iota(I32, (1, LANE), 1)
    first = lane < HEAD_DIM

    def proj(off, width):
        return _mm(xn, w_ref[:, off:off + width], precise)

    def normed(y, gain_row, k_only=False):
        g = gains_ref[gain_row:gain_row + 1, :]
        yn = y * lax.rsqrt(_head_sumsq(y, precise) * (1.0 / HEAD_DIM) + EPS) * g
        return jnp.where(first, yn, y) if k_only else yn

    p = proj(P_NSA_Q, HEADS_W)
    for c in range(2):
        y = _rope_chunk(normed(p[:, c * LANE:(c + 1) * LANE], 0), cos_t, sin_a, sin_b)
        qn_ref[:, c * LANE:(c + 1) * LANE] = (y * SCALE).astype(qn_ref.dtype)
    p = proj(P_NSA_KV, 3 * LANE)
    cos_k = jnp.where(first, cos_t, 1.0)
    sin_ak = jnp.where(first, sin_a, 0.0)
    sin_bk = jnp.where(first, sin_b, 0.0)
    for c, ref in enumerate((cmp_ref, slc_ref, win_ref)):
        y = normed(p[:, c * LANE:(c + 1) * LANE], 1 + c, k_only=True)
        ref[...] = _rope_chunk(y, cos_k, sin_ak, sin_bk)
    p = proj(P_MOBA, 3 * HEADS_W)
    for c in range(2):
        y = _rope_chunk(normed(p[:, c * LANE:(c + 1) * LANE], 4), cos_t, sin_a, sin_b)
        qm_ref[:, c * LANE:(c + 1) * LANE] = (y * SCALE).astype(qn_ref.dtype)
        y = _rope_chunk(normed(p[:, HEADS_W + c * LANE:HEADS_W + (c + 1) * LANE], 5),
                        cos_t, sin_a, sin_b)
        moba_ref[:, c * LANE:(c + 1) * LANE] = y
    moba_ref[:, HEADS_W:] = p[:, 2 * HEADS_W:]
    p = proj(P_FOX, 3 * HEADS_W)
    for c in range(2):
        y = normed(p[:, c * LANE:(c + 1) * LANE], 6)
        qf_ref[:, c * LANE:(c + 1) * LANE] = (y * SCALE).astype(qn_ref.dtype)
        fox_ref[:, c * LANE:(c + 1) * LANE] = normed(
            p[:, HEADS_W + c * LANE:HEADS_W + (c + 1) * LANE], 7)
    fox_ref[:, HEADS_W:] = p[:, 2 * HEADS_W:]
    p = proj(P_SSD, 2 * SSD_INNER + 2 * SSD_BC)
    z_ref[...] = p[:, :SSD_INNER]
    xbc_ref[...] = p[:, SSD_INNER:]
    p = proj(P_SMALL, LANE)
    logf = -_softplus(-(p + bias_ref[0:1, :]))
    small_ref[...] = jnp.where(lane < SM_NSAG, logf, p)
    p = proj(P_DT, LANE)
    dt_ref[...] = _softplus(p + bias_ref[1:2, :])


def _proj_call(x2, gmix, wp, gains, tabs, biases, tm, precise=False):
    n = x2.shape[0]
    nt = tabs.shape[1] // tm
    row = lambda w: pl.BlockSpec((tm, w), lambda i: (i, 0))
    full = lambda a: pl.BlockSpec(a.shape, lambda i: (0,) * a.ndim)
    widths = (HEADS_W, LANE, LANE, LANE, HEADS_W, 2 * HEADS_W, HEADS_W, 2 * HEADS_W,
              SSD_INNER, SSD_CONV_DIM, LANE, LANE)
    qd = F32 if precise else BF16
    dtypes = (qd, F32, F32, F32, qd, F32, qd, F32, F32, F32, F32, F32)
    return pl.pallas_call(
        functools.partial(_proj_kernel, precise=precise),
        grid=(n // tm,),
        in_specs=[row(D_MODEL), full(gmix), full(wp), full(gains),
                  pl.BlockSpec((3, tm, LANE), lambda i: (0, i % nt, 0)), full(biases)],
        out_specs=[row(w) for w in widths],
        out_shape=[jax.ShapeDtypeStruct((n, w), d) for w, d in zip(widths, dtypes)],
        compiler_params=_cparams(("parallel",)),
        name="proj",
    )(x2, gmix, wp, gains, tabs, biases)


def _rope_tables(pos):
    inv = jnp.power(ROPE_THETA, -jnp.arange(ROPE_HALF, dtype=F32) / ROPE_HALF)
    ang = pos.astype(F32)[:, None] * inv[None, :]
    cos, sin = jnp.cos(ang), jnp.sin(ang)
    t = pos.shape[0]
    one = jnp.ones((t, HEAD_DIM - ROPE_DIM), F32)
    zero = jnp.zeros((t, HEAD_DIM - ROPE_DIM), F32)
    zh = jnp.zeros((t, ROPE_HALF), F32)
    c = jnp.concatenate([cos, cos, one], axis=1)
    a = jnp.concatenate([-sin, zh, zero], axis=1)
    b = jnp.concatenate([zh, sin, zero], axis=1)
    return jnp.stack([jnp.tile(c, (1, 2)), jnp.tile(a, (1, 2)), jnp.tile(b, (1, 2))], axis=0)


def _pack_w_in(w_in):
    o = _IN_OFFS
    ssd = w_in[:, o[6]:o[7]]
    small = jnp.concatenate([w_in[:, o[5]:o[6]], w_in[:, o[2]:o[3]],
                             jnp.zeros((D_MODEL, LANE - 4 * N_HEADS), F32)], axis=1)
    dt = jnp.concatenate([ssd[:, 2 * SSD_INNER + 2 * SSD_BC:],
                          jnp.zeros((D_MODEL, LANE - SSD_HEADS), F32)], axis=1)
    wp = jnp.concatenate([w_in[:, o[0]:o[2]], w_in[:, o[3]:o[5]],
                          ssd[:, :2 * SSD_INNER + 2 * SSD_BC], small, dt], axis=1)
    return wp, w_in[:, o[7]:o[8]]


def _proj_gains(prm):
    two = lambda g: jnp.tile(g, 2)
    ones = jnp.ones((HEAD_DIM,), F32)
    rows = [two(prm['g_nsa_q'])]
    rows += [jnp.concatenate([prm['g_nsa_k'][i], ones]) for i in range(3)]
    rows += [two(prm['g_moba_q']), two(prm['g_moba_k']), two(prm['g_fox_q']), two(prm['g_fox_k'])]
    return jnp.stack(rows, axis=0)


def _proj_biases(prm):
    pad = lambda v: jnp.concatenate([v.astype(F32), jnp.zeros((LANE - v.shape[0],), F32)])
    rows = [pad(prm['b_fox_f']), pad(prm['ssd_dt_bias'])] + [jnp.zeros((LANE,), F32)] * 6
    return jnp.stack(rows, axis=0)


def _cmp_mlp_kernel(r_ref, wab_ref, pe_ref, w2_ref, g_ref, out_ref, *, precise):
    n16 = r_ref.shape[1]
    hw = 2 * NSA_CMP_HIDDEN
    h = _mm(r_ref[0], wab_ref[...], precise)
    c = _mm(pe_ref[...], wab_ref[...], precise)
    c = c[0:1, :hw] + c[1:2, hw:]
    pre = h[:, :hw] + pltpu.roll(h[:, hw:], n16 - 1, 0) + c
    act = 0.5 * pre * (1.0 + jnp.tanh(math.sqrt(2.0 / math.pi) * (pre + 0.044715 * pre * pre * pre)))
    y = _mm(act, w2_ref[...], precise)
    lane = lax.broadcasted_iota(I32, (1, LANE), 1)
    yn = y * lax.rsqrt(_head_sumsq(y, precise) * (1.0 / HEAD_DIM) + EPS) * g_ref[...]
    out_ref[0] = jnp.where(lane < HEAD_DIM, yn, y)


def _cmp_mlp_call(rows16, wab, pe2, w2, gkc, precise=False):
    bsz, n16, w = rows16.shape
    full = lambda a: pl.BlockSpec(a.shape, lambda b: (0,) * a.ndim)
    return pl.pallas_call(
        functools.partial(_cmp_mlp_kernel, precise=precise),
        grid=(bsz,),
        in_specs=[pl.BlockSpec((1, n16, w), lambda b: (b, 0, 0)), full(wab), full(pe2), full(w2),
                  full(gkc)],
        out_specs=pl.BlockSpec((1, n16, LANE), lambda b: (b, 0, 0)),
        out_shape=jax.ShapeDtypeStruct((bsz, n16, LANE), F32),
        compiler_params=_cparams(("parallel",)),
        name="cmp_mlp",
    )(rows16, wab, pe2, w2, gkc)


def _cmp_weights(prm, dtype):
    def expand(w1, slot):
        w = w1.reshape(2, NSA_CMP_STRIDE, HEAD_DIM, NSA_CMP_HIDDEN)
        z = jnp.zeros_like(w)
        pair = (w, z) if slot == 0 else (z, w)
        return jnp.concatenate(pair, axis=2).reshape(2, NSA_CMP_STRIDE * LANE, NSA_CMP_HIDDEN)
    wk, wv = expand(prm['w_cmp_k1'], 0), expand(prm['w_cmp_v1'], 1)
    wab = jnp.concatenate([wk[0], wv[0], wk[1], wv[1]], axis=1).astype(dtype)
    pe = jnp.concatenate([prm['pe_cmp_k'], prm['pe_cmp_v']], axis=1)
    pe = pe.reshape(2, NSA_CMP_STRIDE * LANE)
    pe2 = jnp.concatenate([pe, jnp.zeros((SUB - 2, NSA_CMP_STRIDE * LANE), F32)], axis=0)
    z = jnp.zeros((NSA_CMP_HIDDEN, HEAD_DIM), F32)
    w2 = jnp.concatenate([jnp.concatenate([prm['w_cmp_k2'], z], axis=1),
                          jnp.concatenate([z, prm['w_cmp_v2']], axis=1)], axis=0).astype(dtype)
    gkc = jnp.concatenate([prm['g_nsa_kc'], jnp.ones((HEAD_DIM,), F32)])[None, :]
    return wab, pe2, w2, gkc


def _first_argmax(score, lane):
    m = jnp.max(score, axis=-1, keepdims=True)
    cand = jnp.where(score == m, lane.astype(F32), float(score.shape[-1]))
    return m, jnp.min(cand, axis=-1, keepdims=True).astype(I32)


def _nsa_cmp_kernel(q_ref, kvc_ref, ocmp_ref, sel_ref, idx_ref, *, tq, q_pos0, n_cmp, n_sel, k_top,
                    precise):
    n16 = kvc_ref.shape[1]
    nselp = sel_ref.shape[2]
    qi = pl.program_id(1)
    qpos = q_pos0 + qi * tq + lax.broadcasted_iota(I32, (tq, 1), 0)
    q = q_ref[0]
    kvc = kvc_ref[0]
    kc = kvc[:, :HEAD_DIM] if precise else kvc[:, :HEAD_DIM].astype(BF16)
    vc = kvc[:, HEAD_DIM:] if precise else kvc[:, HEAD_DIM:].astype(BF16)
    n_idx = lax.broadcasted_iota(I32, (1, n16), 1)
    mask = jnp.logical_and(n_idx * NSA_CMP_STRIDE + (NSA_CMP_LEN - 1) <= qpos, n_idx < n_cmp)
    psum = jnp.zeros((tq, n16), F32)
    outs = []
    for h in range(N_HEADS):
        lg = jnp.where(mask, _mm_nt(q[:, h * HEAD_DIM:(h + 1) * HEAD_DIM], kc, precise), NEG)
        m = jnp.max(lg, axis=-1, keepdims=True)
        p = jnp.where(mask, jnp.exp(lg - m), 0.0)
        p = p / jnp.maximum(jnp.sum(p, axis=-1, keepdims=True), 1e-30)
        outs.append(_mm(p, vc, precise))
        psum = psum + p
    ocmp_ref[0] = jnp.concatenate(outs, axis=-1)
    n_col = lax.broadcasted_iota(I32, (n16, nselp), 0) * NSA_CMP_STRIDE
    s_row = lax.broadcasted_iota(I32, (n16, nselp), 1) * NSA_SEL_BLOCK
    cover = jnp.logical_and(n_col <= s_row + (NSA_SEL_BLOCK - 1), n_col + (NSA_CMP_LEN - 1) >= s_row)
    cover = jnp.where(cover, 1.0, 0.0).astype(BF16)
    imp = _dot_exact(psum, cover)
    blk = lax.broadcasted_iota(I32, (1, nselp), 1)
    cur = qpos // NSA_SEL_BLOCK
    forced = jnp.logical_or(blk == 0, jnp.logical_or(blk == cur, blk == cur - 1))
    score = jnp.where(blk > cur, -1e9, jnp.where(forced, 1e9, imp))
    score = jnp.where(blk < n_sel, score, -jnp.inf)
    sel = jnp.zeros((tq, nselp), F32)
    lane = lax.broadcasted_iota(I32, (1, LANE), 1)
    picks = jnp.full((tq, LANE), -1, I32)
    for i in range(k_top):
        _, idx = _first_argmax(score, blk)
        hit = blk == idx
        sel = jnp.where(hit, 1.0, sel)
        picks = jnp.where(lane == i, idx, picks)
        score = jnp.where(hit, -jnp.inf, score)
    sel_ref[0] = sel
    idx_ref[0] = picks


def _nsa_cmp_call(q, kvc, *, tq, q_pos0, n_cmp, n_sel, nselp, precise=False):
    bsz, t, _ = q.shape
    n16 = kvc.shape[1]
    kern = functools.partial(_nsa_cmp_kernel, tq=tq, q_pos0=q_pos0, n_cmp=n_cmp, n_sel=n_sel,
                             k_top=min(NSA_SEL_TOPK, n_sel), precise=precise)
    return pl.pallas_call(
        kern,
        grid=(bsz, t // tq),
        in_specs=[pl.BlockSpec((1, tq, HEADS_W), lambda b, i: (b, i, 0)),
                  pl.BlockSpec((1, n16, LANE), lambda b, i: (b, 0, 0))],
        out_specs=[pl.BlockSpec((1, tq, HEADS_W), lambda b, i: (b, i, 0)),
                   pl.BlockSpec((1, tq, nselp), lambda b, i: (b, i, 0)),
                   pl.BlockSpec((1, tq, LANE), lambda b, i: (b, i, 0))],
        out_shape=[jax.ShapeDtypeStruct((bsz, t, HEADS_W), F32),
                   jax.ShapeDtypeStruct((bsz, t, nselp), F32),
                   jax.ShapeDtypeStruct((bsz, t, LANE), I32)],
        compiler_params=_cparams(("parallel", "parallel")),
        name="nsa_cmp",
    )(q, kvc)


def _attn_kernel(*refs, cfg):
    tq, tk, nk, kw = cfg['tq'], cfg['tk'], cfg['nk'], cfg['kw']
    refs = list(refs)
    q_ref, kv_ref = refs.pop(0), refs.pop(0)
    bm_ref = refs.pop(0) if cfg['bm_bs'] else None
    if cfg['bias']:
        cq_ref, cum_ref = refs.pop(0), refs.pop(0)
    o_ref, m_ref, l_ref, acc_ref = refs
    qi, ki = pl.program_id(1), pl.program_id(2)

    @pl.when(ki == 0)
    def _():
        m_ref[...] = jnp.full(m_ref.shape, NEG, F32)
        l_ref[...] = jnp.zeros(l_ref.shape, F32)
        acc_ref[...] = jnp.zeros(acc_ref.shape, F32)

    q_start = qi * tq
    k_start = ki * tk
    needed = k_start <= q_start + tq - 1
    if cfg['window']:
        needed = jnp.logical_and(needed, k_start + tk - 1 >= q_start - (cfg['window'] - 1))

    @pl.when(needed)
    def _():
        qpos = q_start + lax.broadcasted_iota(I32, (tq, 1), 0)
        kpos = k_start + lax.broadcasted_iota(I32, (1, tk), 1)
        q = q_ref[0]
        kv = kv_ref[0]
        mask = kpos <= qpos
        if cfg['window']:
            mask = jnp.logical_and(mask, qpos - kpos < cfg['window'])
        neg = jnp.where(mask, 0.0, NEG)
        hs = cfg['bm_hs']
        if bm_ref is not None:
            bmv = bm_ref[0].astype(BF16)
            j = lax.broadcasted_iota(I32, (hs if hs else bmv.shape[1], 1), 0)
            expand = jnp.where(j == kpos // cfg['bm_bs'], 1.0, 0.0).astype(BF16)
            if not hs:
                neg = neg + (_dot(bmv, expand) - 1.0) * (-NEG)
        for h in range(N_HEADS):
            ko = h * HEAD_DIM if kw > HEAD_DIM else 0
            kh = kv[:, ko:ko + HEAD_DIM]
            vh = kv[:, kw + ko:kw + ko + HEAD_DIM]
            s = _mm_nt(q[:, h * HEAD_DIM:(h + 1) * HEAD_DIM], kh, cfg['precise']) + neg
            if cfg['bias']:
                s = s + (cq_ref[0][:, h:h + 1] - cum_ref[0][h:h + 1, :])
            if bm_ref is not None and hs:
                s = s + (_dot(bmv[:, h * hs:(h + 1) * hs], expand) - 1.0) * (-NEG)
            m_old = m_ref[h]
            m_new = jnp.maximum(m_old, jnp.max(s, axis=-1, keepdims=True))
            p = jnp.exp(s - m_new)
            alpha = jnp.exp(m_old - m_new)
            l_ref[h] = alpha * l_ref[h] + jnp.sum(p, axis=-1, keepdims=True)
            acc_ref[h] = alpha * acc_ref[h] + _mm(p, vh, cfg['precise'])
            m_ref[h] = m_new

    @pl.when(ki == nk - 1)
    def _():
        o_ref[0] = jnp.concatenate(
            [acc_ref[h] / jnp.maximum(l_ref[h], 1e-30) for h in range(N_HEADS)], axis=-1)


def _attn_call(q, kv, *, tq, tk, window=0, bm=None, bm_bs=0, bm_hs=0, cq=None, cum_t=None,
               precise=False, name="attn"):
    bsz, t, _ = q.shape
    kw = kv.shape[-1] // 2
    assert t % tq == 0 and t % tk == 0
    nk = t // tk
    cfg = dict(tq=tq, tk=tk, nk=nk, kw=kw, bm_bs=bm_bs, bm_hs=bm_hs, bias=cq is not None,
               window=window, precise=precise)

    def kv_map(b, qi, ki):
        j = jnp.minimum(ki, (qi * tq + tq - 1) // tk)
        if window:
            j = jnp.maximum(j, jnp.maximum(qi * tq - (window - 1), 0) // tk)
        return (b, j, 0)

    q_map = lambda b, qi, ki: (b, qi, 0)
    args = [q, kv]
    specs = [pl.BlockSpec((1, tq, HEADS_W), q_map), pl.BlockSpec((1, tk, 2 * kw), kv_map)]
    if bm is not None:
        args.append(bm)
        specs.append(pl.BlockSpec((1, tq, bm.shape[2]), q_map))
    if cq is not None:
        args += [cq, cum_t]
        specs += [pl.BlockSpec((1, tq, LANE), q_map),
                  pl.BlockSpec((1, SUB, tk), lambda b, qi, ki: (b, 0, ki))]
    return pl.pallas_call(
        functools.partial(_attn_kernel, cfg=cfg),
        grid=(bsz, t // tq, nk), in_specs=specs, out_specs=pl.BlockSpec((1, tq, HEADS_W), q_map),
        scratch_shapes=[pltpu.VMEM((N_HEADS, tq, 1), F32), pltpu.VMEM((N_HEADS, tq, 1), F32),
                        pltpu.VMEM((N_HEADS, tq, HEAD_DIM), F32)],
        out_shape=jax.ShapeDtypeStruct((bsz, t, HEADS_W), F32),
        compiler_params=_cparams(("parallel", "parallel", "arbitrary")), name=name,
    )(*args)


def _col_from_row(row, n):
    eye = lax.broadcasted_iota(I32, (n, n), 0) == lax.broadcasted_iota(I32, (n, n), 1)
    return jnp.sum(jnp.where(eye, jnp.broadcast_to(row, (n, n)), 0.0), axis=1, keepdims=True)


def _row_from_col(col, n):
    eye = lax.broadcasted_iota(I32, (n, n), 0) == lax.broadcasted_iota(I32, (n, n), 1)
    return jnp.sum(jnp.where(eye, jnp.broadcast_to(col, (n, n)), 0.0), axis=0, keepdims=True)


def _sublane_total(x):
    for s in (4, 2, 1):
        x = x + pltpu.roll(x, s, 0)
    return x


def _query_page_logits(qb, kt):
    prod = qb * kt
    return _sublane_total(prod.reshape(HEAD_DIM // SUB, SUB, prod.shape[1]).sum(axis=0))


def _kt_attn_kernel(pt_ref, ft_ref, hd_ref, q_ref, *refs, cfg):
    g, nk, hk, pw, r, lo = cfg['g'], cfg['nk'], cfg['hk'], cfg['pw'], cfg['r'], cfg['lo']
    bias, one_head = cfg['bias'], cfg['one_head']
    refs = list(refs)
    kv_refs = [refs.pop(0) for _ in range(g)]
    self_ref = refs.pop(0)
    if bias:
        lf_refs = [refs.pop(0) for _ in range(g)]
        lfnew_ref = refs.pop(0)
    o_ref, m_ref, l_ref, acc_ref, qb_ref = refs[:5]
    carry_ref = refs[5] if bias else None
    b, ki = pl.program_id(0), pl.program_id(1)
    sub = lax.broadcasted_iota(I32, (SUB, pw), 0)
    lane = lax.broadcasted_iota(I32, (SUB, pw), 1)
    kv_head = lambda h: h if hk == N_HEADS else 0

    @pl.when(ki == 0)
    def _():
        m_ref[...] = jnp.full(m_ref.shape, NEG, F32)
        l_ref[...] = jnp.zeros(l_ref.shape, F32)
        acc_ref[...] = jnp.zeros(acc_ref.shape, F32)
        for h in range(N_HEADS):
            qcol = _col_from_row(q_ref[0][0:1, h * HEAD_DIM:(h + 1) * HEAD_DIM], HEAD_DIM)
            qb_ref[h] = jnp.broadcast_to(qcol, (HEAD_DIM, pw))
        if bias:
            row8 = jnp.broadcast_to(lfnew_ref[0][0:1, :], (SUB, LANE))
            e8 = lax.broadcasted_iota(I32, (SUB, LANE), 0) == lax.broadcasted_iota(I32, (SUB, LANE), 1)
            col8 = jnp.sum(jnp.where(e8, row8, 0.0), axis=1, keepdims=True)
            carry_ref[...] = jnp.broadcast_to(col8, carry_ref.shape)

    carry = carry_ref[...] if bias else None
    logits, masks = [], []
    for i in range(g):
        j = ki * g + i
        blk, hd = ft_ref[b, j], hd_ref[b, j]
        lane0 = (blk * r) % pw
        ok = jnp.logical_and(jnp.logical_and(lane >= lane0, lane < lane0 + r),
                             blk * r + (lane - lane0) >= lo)
        s = jnp.full((SUB, pw), NEG, F32)
        if one_head:
            hc = jnp.minimum(hd, N_HEADS - 1)
            s = jnp.where(sub == hc, _query_page_logits(qb_ref[hc], kv_refs[i][0, 0, 0, hc]), s)
            ok = jnp.logical_and(ok, jnp.logical_and(sub == hd, hd < N_HEADS))
        else:
            for h in range(N_HEADS):
                s = jnp.where(sub == h, _query_page_logits(qb_ref[h], kv_refs[i][0, 0, 0, kv_head(h)]), s)
            ok = jnp.logical_and(ok, jnp.logical_and(sub < N_HEADS, hd < N_HEADS))
        if bias:
            lf = jnp.concatenate([lf_refs[i][0, 0], jnp.zeros((SUB - N_HEADS, pw), F32)], axis=0)
            pre, sft = lf, 1
            while sft < pw:
                pre = pre + jnp.where(lane >= sft, pltpu.roll(pre, sft, 1), 0.0)
                sft *= 2
            tot = jnp.broadcast_to(pre[:, pw - 1:pw], (SUB, pw))
            s = s + (carry + (tot - pre))
            carry = carry + tot
        logits.append(jnp.where(ok, s, NEG))
        masks.append(ok)
    if bias:
        carry_ref[...] = carry

    m_old = m_ref[...]
    m_new = m_old
    for s in logits:
        m_new = jnp.maximum(m_new, jnp.broadcast_to(jnp.max(s, axis=1, keepdims=True), (SUB, pw)))
    alpha = jnp.exp(m_old - m_new)
    l_new = alpha * l_ref[...]
    probs = []
    for s, ok in zip(logits, masks):
        p = jnp.where(ok, jnp.exp(s - m_new), 0.0)
        l_new = l_new + jnp.broadcast_to(jnp.sum(p, axis=1, keepdims=True), (SUB, pw))
        probs.append(p)
    m_ref[...] = m_new
    l_ref[...] = l_new
    for h in range(N_HEADS):
        acc = alpha[h:h + 1, :] * acc_ref[h]
        if not one_head:
            for i in range(g):
                acc = acc + probs[i][h:h + 1, :] * kv_refs[i][0, 0, 1, kv_head(h)]
        acc_ref[h] = acc
    if one_head:
        for i in range(g):
            hc = jnp.minimum(hd_ref[b, ki * g + i], N_HEADS - 1)
            prow = _sublane_total(jnp.where(sub == hc, probs[i], 0.0))[0:1, :]
            acc_ref[hc] = acc_ref[hc] + prow * kv_refs[i][0, 0, 1, hc]

    @pl.when(ki == nk - 1)
    def _():
        w = hk * HEAD_DIM
        own = self_ref[0]
        sub1 = lax.broadcasted_iota(I32, (SUB, 1), 0)
        s_own = jnp.full((SUB, 1), NEG, F32)
        for h in range(N_HEADS):
            kh = own[0:1, kv_head(h) * HEAD_DIM:(kv_head(h) + 1) * HEAD_DIM]
            qh = q_ref[0][0:1, h * HEAD_DIM:(h + 1) * HEAD_DIM]
            s_own = jnp.where(sub1 == h, jnp.sum(qh * kh, axis=1, keepdims=True), s_own)
        m_fin = m_ref[:, 0:1]
        m_tot = jnp.maximum(m_fin, s_own)
        a_fin = jnp.exp(m_fin - m_tot)
        p_own = jnp.exp(s_own - m_tot)
        l_tot = a_fin * l_ref[:, 0:1] + p_own
        outs = []
        for h in range(N_HEADS):
            vcol = _col_from_row(own[0:1, w + kv_head(h) * HEAD_DIM:w + (kv_head(h) + 1) * HEAD_DIM],
                                 HEAD_DIM)
            tot = a_fin[h:h + 1, :] * jnp.sum(acc_ref[h], axis=1, keepdims=True) + p_own[h:h + 1, :] * vcol
            outs.append(_row_from_col(tot / jnp.maximum(l_tot[h:h + 1, :], 1e-30), HEAD_DIM))
        o_ref[0] = jnp.broadcast_to(jnp.concatenate(outs, axis=1), (SUB, HEADS_W))


def _kt_attn_call(q8, cache_t, layer, page_table, fetch, heads, self8, *, r, g, lo=0, one_head=False,
                  logf_t=None, logf_new=None, name="kt_attn"):
    bsz = q8.shape[0]
    _, _, _, hk, _, pw = cache_t.shape
    nf = fetch.shape[1]
    assert nf % g == 0
    nk = nf // g
    n_pages = page_table.shape[1]
    bias = logf_t is not None
    cfg = dict(g=g, nk=nk, hk=hk, pw=pw, r=r, lo=lo, bias=bias, one_head=one_head)

    def page_of(b, ki, i, pt, ft):
        return pt[b, jnp.clip(ft[b, ki * g + i] * r // pw, 0, n_pages - 1)]

    kv_map = lambda i: (lambda b, ki, pt, ft, hd: (layer, page_of(b, ki, i, pt, ft), 0, 0, 0, 0))
    lf_map = lambda i: (lambda b, ki, pt, ft, hd: (layer, page_of(b, ki, i, pt, ft), 0, 0))
    fixed = lambda shape: pl.BlockSpec(shape, lambda b, ki, pt, ft, hd: (b, 0, 0))
    args = [q8] + [cache_t] * g + [self8]
    specs = ([fixed((1, SUB, HEADS_W))]
             + [pl.BlockSpec((1, 1, 2, hk, HEAD_DIM, pw), kv_map(i)) for i in range(g)]
             + [fixed((1, SUB, self8.shape[2]))])
    scratch = [pltpu.VMEM((SUB, pw), F32), pltpu.VMEM((SUB, pw), F32),
               pltpu.VMEM((N_HEADS, HEAD_DIM, pw), F32), pltpu.VMEM((N_HEADS, HEAD_DIM, pw), F32)]
    if bias:
        args += [logf_t] * g + [logf_new]
        specs += [pl.BlockSpec((1, 1, N_HEADS, pw), lf_map(i)) for i in range(g)] + [fixed((1, SUB, LANE))]
        scratch.append(pltpu.VMEM((SUB, pw), F32))
    return pl.pallas_call(
        functools.partial(_kt_attn_kernel, cfg=cfg),
        grid_spec=pltpu.PrefetchScalarGridSpec(
            num_scalar_prefetch=3, grid=(bsz, nk), in_specs=specs,
            out_specs=fixed((1, SUB, HEADS_W)), scratch_shapes=scratch),
        out_shape=jax.ShapeDtypeStruct((bsz, SUB, HEADS_W), F32),
        compiler_params=_cparams(("parallel", "arbitrary")),
        name=name,
    )(page_table, fetch, heads, *args)


def _kt_gate_kernel(pt_ref, q_ref, *refs, g, nk, k_top, n_blk):
    k_refs = refs[:g]
    idx_ref, qb_ref, gate_ref = refs[g:]
    ki = pl.program_id(1)
    sub = lax.broadcasted_iota(I32, (SUB, LANE), 0)
    lane = lax.broadcasted_iota(I32, (SUB, LANE), 1)
    ppb = MOBA_BLOCK // PAGE

    @pl.when(ki == 0)
    def _():
        gate_ref[...] = jnp.zeros(gate_ref.shape, F32)
        for h in range(N_HEADS):
            qcol = _col_from_row(q_ref[0][0:1, h * HEAD_DIM:(h + 1) * HEAD_DIM], HEAD_DIM)
            qb_ref[h] = jnp.broadcast_to(qcol, (HEAD_DIM, LANE))

    gate = gate_ref[...]
    for i in range(g):
        s = jnp.zeros((SUB, LANE), F32)
        for h in range(N_HEADS):
            s = jnp.where(sub == h, _query_page_logits(qb_ref[h], k_refs[i][0, 0, 0, h]), s)
        tot = jnp.sum(s, axis=1, keepdims=True) * (1.0 / MOBA_BLOCK)
        gate = gate + jnp.where(lane == (ki * g + i) // ppb, tot, 0.0)
    gate_ref[...] = gate

    @pl.when(ki == nk - 1)
    def _():
        score = jnp.where(lane < n_blk, gate, NEG)
        picks = jnp.full((SUB, LANE), -1, I32)
        for i in range(k_top):
            m, idx = _first_argmax(score, lane)
            hit = lane == idx
            picks = jnp.where(jnp.logical_and(lane == i, m > 0.5 * NEG), idx, picks)
            score = jnp.where(hit, -jnp.inf, score)
        idx_ref[0] = picks


def _kt_gate_call(q8, cache_t, layer, page_table, *, g, k_top):
    bsz = q8.shape[0]
    n_pages = page_table.shape[1]
    assert n_pages % g == 0 and n_pages * PAGE % MOBA_BLOCK == 0
    nk = n_pages // g
    n_blk = n_pages * PAGE // MOBA_BLOCK
    assert n_blk <= LANE
    k_map = lambda i: (lambda b, ki, pt: (layer, pt[b, ki * g + i], 0, 0, 0, 0))
    return pl.pallas_call(
        functools.partial(_kt_gate_kernel, g=g, nk=nk, k_top=k_top, n_blk=n_blk),
        grid_spec=pltpu.PrefetchScalarGridSpec(
            num_scalar_prefetch=1, grid=(bsz, nk),
            in_specs=[pl.BlockSpec((1, SUB, HEADS_W), lambda b, ki, pt: (b, 0, 0))]
            + [pl.BlockSpec((1, 1, 1, N_HEADS, HEAD_DIM, PAGE), k_map(i)) for i in range(g)],
            out_specs=pl.BlockSpec((1, SUB, LANE), lambda b, ki, pt: (b, 0, 0)),
            scratch_shapes=[pltpu.VMEM((N_HEADS, HEAD_DIM, LANE), F32), pltpu.VMEM((SUB, LANE), F32)]),
        out_shape=jax.ShapeDtypeStruct((bsz, SUB, LANE), I32),
        compiler_params=_cparams(("parallel", "arbitrary")),
        name="moba_gate_s",
    )(page_table, q8, *([cache_t] * g))


def _kt_cmp_kernel(pt_ref, *refs, g, nk):
    kv_refs = refs[:g]
    wc_ref, pe_ref, w2_ref, g_ref, out_ref, rows_ref = refs[g:]
    ki = pl.program_id(1)
    for i in range(g):
        off = pl.multiple_of((ki * g + i) * PAGE, PAGE)
        rows_ref[pl.ds(off, PAGE), :] = kv_refs[i][0, 0].reshape(2 * HEAD_DIM, PAGE).T

    @pl.when(ki == nk - 1)
    def _():
        n16 = rows_ref.shape[0] // NSA_CMP_STRIDE
        hw = 2 * NSA_CMP_HIDDEN
        h = jnp.zeros((n16, 2 * hw), F32)
        c = jnp.zeros((SUB, 2 * hw), F32)
        for j in range(NSA_CMP_STRIDE):
            h = h + _mm(rows_ref[pl.ds(j, n16, stride=NSA_CMP_STRIDE), :], wc_ref[j], True)
            c = c + _mm(pe_ref[j], wc_ref[j], True)
        c = c[0:1, :hw] + c[1:2, hw:]
        pre = h[:, :hw] + pltpu.roll(h[:, hw:], n16 - 1, 0) + c
        act = 0.5 * pre * (1.0 + jnp.tanh(math.sqrt(2.0 / math.pi) * (pre + 0.044715 * pre * pre * pre)))
        y = _mm(act, w2_ref[...], True)
        lane = lax.broadcasted_iota(I32, (1, LANE), 1)
        yn = y * lax.rsqrt(_head_sumsq(y, True) * (1.0 / HEAD_DIM) + EPS) * g_ref[...]
        out_ref[0] = jnp.where(lane < HEAD_DIM, yn, y)


def _kt_cmp_call(cache_t, layer, page_table, wab, pe2, w2, gkc, *, g):
    bsz, n_pages = page_table.shape
    assert n_pages % g == 0
    nk = n_pages // g
    n16 = n_pages * PAGE // NSA_CMP_STRIDE
    wc = wab.reshape(NSA_CMP_STRIDE, LANE, wab.shape[1])
    pe = jnp.swapaxes(pe2[:2].reshape(2, NSA_CMP_STRIDE, LANE), 0, 1)
    pe = jnp.pad(pe, ((0, 0), (0, SUB - 2), (0, 0)))
    full = lambda a: pl.BlockSpec(a.shape, lambda b, ki, pt: (0,) * a.ndim)
    kv_map = lambda i: (lambda b, ki, pt: (layer, pt[b, ki * g + i], 0, 0, 0))
    return pl.pallas_call(
        functools.partial(_kt_cmp_kernel, g=g, nk=nk),
        grid_spec=pltpu.PrefetchScalarGridSpec(
            num_scalar_prefetch=1, grid=(bsz, nk),
            in_specs=[pl.BlockSpec((1, 1, 2, HEAD_DIM, PAGE), kv_map(i)) for i in range(g)]
            + [full(wc), full(pe), full(w2), full(gkc)],
            out_specs=pl.BlockSpec((1, n16, LANE), lambda b, ki, pt: (b, 0, 0)),
            scratch_shapes=[pltpu.VMEM((n_pages * PAGE, LANE), F32)]),
        out_shape=jax.ShapeDtypeStruct((bsz, n16, LANE), F32),
        compiler_params=_cparams(("parallel", "arbitrary")),
        name="cmp_mlp_s",
    )(page_table, *([cache_t] * g), wc, pe, w2, gkc)


MOBA_BLOCKS_PER_STEP = 8


def _kmean_kernel(kv_ref, out_ref):
    row = lax.broadcasted_iota(I32, (MOBA_BLOCKS_PER_STEP, 1), 0)
    acc = jnp.zeros((MOBA_BLOCKS_PER_STEP, HEADS_W), F32)
    for j in range(MOBA_BLOCKS_PER_STEP):
        s = jnp.sum(kv_ref[0, j * MOBA_BLOCK:(j + 1) * MOBA_BLOCK, :], axis=0, keepdims=True)
        acc = acc + jnp.where(row == j, s, 0.0)
    out_ref[0] = acc * (1.0 / MOBA_BLOCK)


def _kmean_call(kv):
    bsz, tk, _ = kv.shape
    span = MOBA_BLOCKS_PER_STEP * MOBA_BLOCK
    assert tk % span == 0
    return pl.pallas_call(
        _kmean_kernel, grid=(bsz, tk // span),
        in_specs=[pl.BlockSpec((1, span, HEADS_W), lambda b, j: (b, j, 0))],
        out_specs=pl.BlockSpec((1, MOBA_BLOCKS_PER_STEP, HEADS_W), lambda b, j: (b, j, 0)),
        out_shape=jax.ShapeDtypeStruct((bsz, tk // MOBA_BLOCK, HEADS_W), F32),
        compiler_params=_cparams(("parallel", "parallel")), name="moba_kmean")(kv)


def _moba_gate_kernel(q_ref, km_ref, bm_ref, *, tq, k_top, precise):
    qi = pl.program_id(1)
    qpos = qi * tq + lax.broadcasted_iota(I32, (tq, 1), 0)
    cur = qpos // MOBA_BLOCK
    blk = lax.broadcasted_iota(I32, (1, LANE), 1)
    valid = blk < cur
    q = q_ref[0]
    km = km_ref[0]
    for h in range(N_HEADS):
        kmh = km[:, h * HEAD_DIM:(h + 1) * HEAD_DIM]
        qh = q[:, h * HEAD_DIM:(h + 1) * HEAD_DIM]
        if precise:
            gate = _mm_nt(qh, kmh, True)
        else:
            hi, lo = _split2(kmh)
            gate = _dot_nt(qh, hi) + _dot_nt(qh, lo)
        score = jnp.where(valid, gate, NEG)
        sel = jnp.where(blk == cur, 1.0, 0.0)
        for _ in range(k_top):
            m, idx = _first_argmax(score, blk)
            hit = blk == idx
            sel = jnp.where(jnp.logical_and(hit, m > 0.5 * NEG), 1.0, sel)
            score = jnp.where(hit, -jnp.inf, score)
        bm_ref[0, :, h * LANE:(h + 1) * LANE] = sel


def _moba_gate_call(q, kmean, *, tq, k_top, precise=False):
    bsz, t, _ = q.shape
    return pl.pallas_call(
        functools.partial(_moba_gate_kernel, tq=tq, k_top=k_top, precise=precise),
        grid=(bsz, t // tq),
        in_specs=[pl.BlockSpec((1, tq, HEADS_W), lambda b, i: (b, i, 0)),
                  pl.BlockSpec((1, LANE, HEADS_W), lambda b, i: (b, 0, 0))],
        out_specs=pl.BlockSpec((1, tq, N_HEADS * LANE), lambda b, i: (b, i, 0)),
        out_shape=jax.ShapeDtypeStruct((bsz, t, N_HEADS * LANE), F32),
        compiler_params=_cparams(("parallel", "parallel")),
        name="moba_gate",
    )(q, kmean)


CUM_CHUNK = 512


def _cum_kernel(l_ref, o_ref, carry_ref):
    @pl.when(pl.program_id(1) == 0)
    def _():
        carry_ref[...] = jnp.zeros(carry_ref.shape, F32)

    r = lax.broadcasted_iota(I32, (CUM_CHUNK, CUM_CHUNK), 0)
    c = lax.broadcasted_iota(I32, (CUM_CHUNK, CUM_CHUNK), 1)
    u = jnp.where(r <= c, 1.0, 0.0).astype(BF16)
    h1, h2, h3 = _split3(l_ref[0])
    cum = _dot(h1, u) + _dot(h2, u) + _dot(h3, u) + carry_ref[:, 0:1]
    o_ref[0] = cum
    carry_ref[...] = jnp.broadcast_to(cum[:, CUM_CHUNK - 1:CUM_CHUNK], carry_ref.shape)


def _cum_call(logf_t):
    bsz, rows, tk = logf_t.shape
    assert rows == SUB and tk % CUM_CHUNK == 0
    return pl.pallas_call(
        _cum_kernel,
        grid=(bsz, tk // CUM_CHUNK),
        in_specs=[pl.BlockSpec((1, SUB, CUM_CHUNK), lambda b, j: (b, 0, j))],
        out_specs=pl.BlockSpec((1, SUB, CUM_CHUNK), lambda b, j: (b, 0, j)),
        out_shape=jax.ShapeDtypeStruct(logf_t.shape, F32),
        scratch_shapes=[pltpu.VMEM((SUB, LANE), F32)],
        compiler_params=_cparams(("parallel", "arbitrary")),
        name="fox_cum",
    )(logf_t)


def _group_norm_gate(y, z, norm):
    y = y * _silu(z)
    gw = SSD_INNER // SSD_GROUPS
    outs = []
    for g in range(SSD_GROUPS):
        yg = y[:, g * gw:(g + 1) * gw]
        ms = jnp.mean(yg * yg, axis=-1, keepdims=True)
        outs.append(yg * lax.rsqrt(ms + EPS) * norm[:, g * gw:(g + 1) * gw])
    return jnp.concatenate(outs, axis=-1)


def _ssd_kernel(xbc_ref, z_ref, dt_ref, convw_ref, convb_ref, hp_ref, norm_ref, y_ref, state_ref,
                ext_ref, h_ref, *, nc, precise):
    q = SSD_CHUNK
    ci = pl.program_id(1)

    @pl.when(ci == 0)
    def _():
        ext_ref[0:SUB, :] = jnp.zeros((SUB, SSD_CONV_DIM), F32)
        h_ref[...] = jnp.zeros(h_ref.shape, F32)

    u = xbc_ref[0]
    ext_ref[SUB:SUB + q, :] = u
    acc = jnp.broadcast_to(convb_ref[...], (q, SSD_CONV_DIM))
    for i in range(SSD_CONV):
        k = SSD_CONV - 1 - i
        acc = acc + ext_ref[SUB - k:SUB - k + q, :] * convw_ref[i:i + 1, :]
    ext_ref[0:SUB, :] = u[q - SUB:, :]
    xbc = _silu(acc)
    xs = xbc[:, :SSD_INNER]
    dt = dt_ref[0]
    a = -jnp.exp(hp_ref[0:1, :])
    d_skip = hp_ref[1:2, :]
    row = lax.broadcasted_iota(I32, (q, 1), 0)
    cum = dt * a
    s = 1
    while s < q:
        cum = cum + jnp.where(row >= s, pltpu.roll(cum, s, 0), 0.0)
        s *= 2
    cum_t = cum.T
    tri = lax.broadcasted_iota(I32, (q, q), 0) >= lax.broadcasted_iota(I32, (q, q), 1)
    ys, xdd = [], []
    cbs = []
    for g in range(SSD_GROUPS):
        bm = xbc[:, SSD_INNER + g * SSD_STATE:SSD_INNER + (g + 1) * SSD_STATE]
        cm = xbc[:, SSD_INNER + SSD_BC + g * SSD_STATE:SSD_INNER + SSD_BC + (g + 1) * SSD_STATE]
        cbs.append((bm, cm, _mm_nt(cm, bm, precise)))
    rep = SSD_HEADS // SSD_GROUPS
    for h in range(SSD_HEADS):
        bm, cm, cb = cbs[h // rep]
        ch = cum[:, h:h + 1]
        lmat = jnp.where(tri, jnp.exp(ch - cum_t[h:h + 1, :]), 0.0)
        xh = xs[:, h * SSD_HEAD_DIM:(h + 1) * SSD_HEAD_DIM]
        xdt = xh * dt[:, h:h + 1]
        y = _mm(cb * lmat, xdt, precise)
        y = y + jnp.exp(ch) * _mm_nt(cm, h_ref[h], precise)
        ys.append(y + d_skip[:, h:h + 1] * xh)
        xdd.append(xdt * jnp.exp(cum[q - 1:q, h:h + 1] - ch))
    xdd_t = jnp.concatenate(xdd, axis=-1).T
    for h in range(SSD_HEADS):
        bm = cbs[h // rep][0]
        upd = _mm(xdd_t[h * SSD_HEAD_DIM:(h + 1) * SSD_HEAD_DIM, :], bm, precise)
        h_ref[h] = jnp.exp(cum[q - 1:q, h:h + 1]) * h_ref[h] + upd
    y_ref[0] = _group_norm_gate(jnp.concatenate(ys, axis=-1), z_ref[0], norm_ref[...])

    @pl.when(ci == nc - 1)
    def _():
        state_ref[0] = h_ref[...]


def _ssd_call(xbc, z, dt, convw, convb, hp, norm, precise=False):
    bsz, t, _ = xbc.shape
    nc = t // SSD_CHUNK
    blk = lambda w: pl.BlockSpec((1, SSD_CHUNK, w), lambda b, c: (b, c, 0))
    full = lambda a: pl.BlockSpec(a.shape, lambda b, c: (0,) * a.ndim)
    return pl.pallas_call(
        functools.partial(_ssd_kernel, nc=nc, precise=precise),
        grid=(bsz, nc),
        in_specs=[blk(SSD_CONV_DIM), blk(SSD_INNER), blk(LANE), full(convw), full(convb), full(hp),
                  full(norm)],
        out_specs=[blk(SSD_INNER),
                   pl.BlockSpec((1, SSD_HEADS, SSD_HEAD_DIM, SSD_STATE), lambda b, c: (b, 0, 0, 0))],
        out_shape=[jax.ShapeDtypeStruct((bsz, t, SSD_INNER), F32),
                   jax.ShapeDtypeStruct((bsz, SSD_HEADS, SSD_HEAD_DIM, SSD_STATE), F32)],
        scratch_shapes=[pltpu.VMEM((SUB + SSD_CHUNK, SSD_CONV_DIM), F32),
                        pltpu.VMEM((SSD_HEADS, SSD_HEAD_DIM, SSD_STATE), F32)],
        compiler_params=_cparams(("parallel", "arbitrary")),
        name="ssd_scan",
    )(xbc, z, dt, convw, convb, hp, norm)


def _ssd_step_kernel(full_ref, z_ref, dt_ref, st_ref, w8_ref, convb_ref, hp_ref, norm_ref,
                     y_ref, state_ref):
    acc = convb_ref[...] + jnp.sum(full_ref[0] * w8_ref[...], axis=0, keepdims=True)
    xbc = _silu(acc)
    xs = xbc[:, :SSD_INNER]
    dt = dt_ref[0]
    a = -jnp.exp(hp_ref[0:1, :])
    d_skip = hp_ref[1:2, :]
    n = SSD_HEAD_DIM
    eye = lax.broadcasted_iota(I32, (n, n), 0) == lax.broadcasted_iota(I32, (n, n), 1)
    rep = SSD_HEADS // SSD_GROUPS
    ys = []
    for h in range(SSD_HEADS):
        g = h // rep
        bm = xbc[:, SSD_INNER + g * SSD_STATE:SSD_INNER + (g + 1) * SSD_STATE]
        cm = xbc[:, SSD_INNER + SSD_BC + g * SSD_STATE:SSD_INNER + SSD_BC + (g + 1) * SSD_STATE]
        dth = dt[:, h:h + 1]
        xh = xs[:, h * n:(h + 1) * n]
        xcol = jnp.sum(jnp.where(eye, jnp.broadcast_to(xh * dth, (n, n)), 0.0), axis=1, keepdims=True)
        hn = jnp.exp(dth * a[:, h:h + 1]) * st_ref[0, h] + xcol * bm
        state_ref[0, h] = hn
        ycol = jnp.sum(hn * cm, axis=1, keepdims=True)
        yrow = jnp.sum(jnp.where(eye, jnp.broadcast_to(ycol, (n, n)), 0.0), axis=0, keepdims=True)
        ys.append(yrow + d_skip[:, h:h + 1] * xh)
    y_ref[0] = _group_norm_gate(jnp.concatenate(ys, axis=-1), z_ref[0], norm_ref[...])


def _ssd_step_call(full8, z, dt, state, w8, convb, hp, norm):
    bsz = full8.shape[0]
    one = lambda w: pl.BlockSpec((1, 1, w), lambda b: (b, 0, 0))
    full = lambda a: pl.BlockSpec(a.shape, lambda b: (0,) * a.ndim)
    st = pl.BlockSpec((1, SSD_HEADS, SSD_HEAD_DIM, SSD_STATE), lambda b: (b, 0, 0, 0))
    return pl.pallas_call(
        _ssd_step_kernel,
        grid=(bsz,),
        in_specs=[pl.BlockSpec((1, SUB, SSD_CONV_DIM), lambda b: (b, 0, 0)), one(SSD_INNER), one(LANE),
                  st, full(w8), full(convb), full(hp), full(norm)],
        out_specs=[one(SSD_INNER), st],
        out_shape=[jax.ShapeDtypeStruct((bsz, 1, SSD_INNER), F32),
                   jax.ShapeDtypeStruct(state.shape, F32)],
        compiler_params=_cparams(("parallel",)),
        name="ssd_step",
    )(full8, z, dt, state, w8, convb, hp, norm)


R_GROUP0 = MOE_EXPERTS


def _post_kernel(x_ref, gmix_ref, wg_ref, small_ref, ocmp_ref, oslc_ref, owin_ref, omoba_ref,
                 ofox_ref, ossd_ref, wbr_ref, wout_ref, gffn_ref, wr_ref, br_ref,
                 x1_ref, xn2_ref, comb_ref, *, precise):
    x = x_ref[...]
    xn = x * lax.rsqrt(jnp.mean(x * x, axis=-1, keepdims=True) + EPS) * gmix_ref[...]
    xn = xn if precise else xn.astype(BF16)
    r = lax.broadcasted_iota(I32, (LANE, 3 * HEADS_W), 0)
    c = lax.broadcasted_iota(I32, (LANE, 3 * HEADS_W), 1)
    pick = r == SM_NSAG + 3 * ((c % HEADS_W) // HEAD_DIM) + c // HEADS_W
    pick = jnp.where(pick, 1.0, 0.0).astype(BF16)
    gexp = _dot_exact(_sigmoid(small_ref[...]), pick)
    o_nsa = (gexp[:, :HEADS_W] * ocmp_ref[...] + gexp[:, HEADS_W:2 * HEADS_W] * oslc_ref[...]
             + gexp[:, 2 * HEADS_W:] * owin_ref[...])
    merged = jnp.zeros(x.shape, F32)
    off = 0
    for b, o in enumerate((o_nsa, omoba_ref[...], ofox_ref[...], ossd_ref[...])):
        w = o.shape[1]
        gate = _sigmoid(_mm(xn, wg_ref[:, b * D_MODEL:(b + 1) * D_MODEL], precise))
        merged = merged + gate * _mm(o, wbr_ref[off:off + w, :], precise)
        off += w
    x1 = x + _mm(merged, wout_ref[...], precise)
    x1_ref[...] = x1
    xn2 = x1 * lax.rsqrt(jnp.mean(x1 * x1, axis=-1, keepdims=True) + EPS) * gffn_ref[...]
    xn2_ref[...] = xn2.astype(xn2_ref.dtype)
    if precise:
        logits = _mm(xn2, wr_ref[0], True) + br_ref[...]
    else:
        hi, lo = _split2(xn2)
        logits = _dot(hi, wr_ref[0]) + _dot(lo, wr_ref[0]) + _dot(hi, wr_ref[1]) + br_ref[...]
    lane = lax.broadcasted_iota(I32, (1, LANE), 1)
    is_grp = jnp.logical_and(lane >= R_GROUP0, lane < R_GROUP0 + MOE_GROUPS)
    gmax, gidx = _first_argmax(jnp.where(is_grp, logits, -jnp.inf), lane)
    g_w = 1.0 / jnp.sum(jnp.where(is_grp, jnp.exp(logits - gmax), 0.0), axis=-1, keepdims=True)
    in_grp = lane // MOE_PER_GROUP == gidx - R_GROUP0
    e1 = jnp.where(in_grp, logits, -jnp.inf)
    v1, i1 = _first_argmax(e1, lane)
    e2 = jnp.where(lane == i1, -jnp.inf, e1)
    v2, i2 = _first_argmax(e2, lane)
    t = jnp.exp(v2 - v1)
    w1 = 1.0 / (1.0 + t)
    comb_ref[...] = (jnp.where(lane == i1, w1 * g_w, 0.0) + jnp.where(lane == i2, t * w1 * g_w, 0.0))


def _post_call(x2, gmix, wg, small, o_cmp, o_slc, o_win, o_moba, o_fox, o_ssd, wbr, wout, gffn,
               wr, br, tm, precise=False):
    n = x2.shape[0]
    row = lambda a: pl.BlockSpec((tm, a.shape[1]), lambda i: (i, 0))
    full = lambda a: pl.BlockSpec(a.shape, lambda i: (0,) * a.ndim)
    big = (lambda a: pl.BlockSpec(a.shape, lambda i: (0,) * a.ndim, pipeline_mode=pl.Buffered(1))
           ) if precise else full
    rows = (x2, small, o_cmp, o_slc, o_win, o_moba, o_fox, o_ssd)
    return pl.pallas_call(
        functools.partial(_post_kernel, precise=precise),
        grid=(n // tm,),
        in_specs=[row(x2), full(gmix), big(wg)] + [row(a) for a in rows[1:]]
        + [big(wbr), big(wout), full(gffn), full(wr), full(br)],
        out_specs=[pl.BlockSpec((tm, D_MODEL), lambda i: (i, 0)),
                   pl.BlockSpec((tm, D_MODEL), lambda i: (i, 0)),
                   pl.BlockSpec((tm, LANE), lambda i: (i, 0))],
        out_shape=[jax.ShapeDtypeStruct((n, D_MODEL), F32),
                   jax.ShapeDtypeStruct((n, D_MODEL), F32 if precise else BF16),
                   jax.ShapeDtypeStruct((n, LANE), F32)],
        compiler_params=_cparams(("parallel",)),
        name="merge_out",
    )(x2, gmix, wg, small, o_cmp, o_slc, o_win, o_moba, o_fox, o_ssd, wbr, wout, gffn, wr, br)


def _moe_kernel(xn_ref, x1_ref, comb_ref, wg_ref, wu_ref, wd_ref, o_ref, acc_ref, *, precise):
    e = pl.program_id(1)

    @pl.when(e == 0)
    def _():
        acc_ref[...] = x1_ref[...]

    lane = lax.broadcasted_iota(I32, (1, LANE), 1)
    cw = jnp.sum(jnp.where(lane == e, comb_ref[...], 0.0), axis=-1, keepdims=True)
    xn = xn_ref[...]
    h = _silu(_mm(xn, wg_ref[0], precise)) * _mm(xn, wu_ref[0], precise)
    acc_ref[...] += cw * _mm(h, wd_ref[0], precise)

    @pl.when(e == MOE_EXPERTS - 1)
    def _():
        o_ref[...] = acc_ref[...]


def _moe_call(xn2, x1, comb, wg, wu, wd, tm, precise=False):
    n = xn2.shape[0]
    return pl.pallas_call(
        functools.partial(_moe_kernel, precise=precise),
        grid=(n // tm, MOE_EXPERTS),
        in_specs=[pl.BlockSpec((tm, D_MODEL), lambda i, e: (i, 0)),
                  pl.BlockSpec((tm, D_MODEL), lambda i, e: (i, 0)),
                  pl.BlockSpec((tm, LANE), lambda i, e: (i, 0)),
                  pl.BlockSpec((1, D_MODEL, MOE_HIDDEN), lambda i, e: (e, 0, 0)),
                  pl.BlockSpec((1, D_MODEL, MOE_HIDDEN), lambda i, e: (e, 0, 0)),
                  pl.BlockSpec((1, MOE_HIDDEN, D_MODEL), lambda i, e: (e, 0, 0))],
        out_specs=pl.BlockSpec((tm, D_MODEL), lambda i, e: (i, 0)),
        out_shape=jax.ShapeDtypeStruct((n, D_MODEL), F32),
        scratch_shapes=[pltpu.VMEM((tm, D_MODEL), F32)],
        compiler_params=_cparams(("parallel", "arbitrary")),
        name="moe_experts",
    )(xn2, x1, comb, wg, wu, wd)


def _layer_weights(prm):
    wp, wgate = _pack_w_in(prm['w_in'])
    wr = jnp.concatenate([prm['w_router_exp'], prm['w_router_grp'],
                          jnp.zeros((D_MODEL, LANE - MOE_EXPERTS - MOE_GROUPS), F32)], axis=1)
    wr_hi = wr.astype(BF16)
    wr_lo = (wr - wr_hi.astype(F32)).astype(BF16)
    br = jnp.concatenate([prm['b_router_exp'], prm['b_router_grp'],
                          jnp.zeros((LANE - MOE_EXPERTS - MOE_GROUPS,), F32)])[None, :]
    pad = lambda v: jnp.concatenate([v.astype(F32), jnp.zeros((LANE - v.shape[0],), F32)])
    hp = jnp.stack([pad(prm['ssd_a_log']), pad(prm['ssd_d'])] + [jnp.zeros((LANE,), F32)] * 6)
    w8 = jnp.concatenate([prm['ssd_conv_w'], jnp.zeros((SUB - SSD_CONV, SSD_CONV_DIM), F32)], axis=0)
    wbr = jnp.concatenate([prm['w_br_nsa'], prm['w_br_moba'], prm['w_br_fox'], prm['w_br_ssd']], axis=0)
    shared = dict(gmix=prm['norm_mix'][None, :], gffn=prm['norm_ffn'][None, :], gains=_proj_gains(prm),
                  biases=_proj_biases(prm), br=br, hp=hp, w8=w8, convw=prm['ssd_conv_w'],
                  convb=prm['ssd_conv_b'][None, :], norm=prm['ssd_norm'][None, :])
    fast = dict(shared, wp=wp.astype(BF16), wgate=wgate.astype(BF16), cmp=_cmp_weights(prm, BF16),
                wbr=wbr.astype(BF16), wout=prm['w_out'].astype(BF16), wr=jnp.stack([wr_hi, wr_lo]),
                weg=prm['w_exp_gate'].astype(BF16), weu=prm['w_exp_up'].astype(BF16),
                wed=prm['w_exp_down'].astype(BF16))
    exact = dict(shared, wp=wp, wgate=wgate, cmp=_cmp_weights(prm, F32), wbr=wbr, wout=prm['w_out'],
                 wr=wr[None], weg=prm['w_exp_gate'], weu=prm['w_exp_up'], wed=prm['w_exp_down'])
    return fast, exact


def _pad_rows(a, rows):
    return jnp.pad(a, ((0, 0), (0, rows - a.shape[1])) + ((0, 0),) * (a.ndim - 2))


def _heads_to_sublanes(logf, tk_pad):
    lt = jnp.swapaxes(logf, 1, 2)
    return jnp.pad(lt, ((0, 0), (0, SUB - lt.shape[1]), (0, tk_pad - lt.shape[2])))


def _layer_prompt(x, w, w_ffn, *, tm, tq, tk, precise):
    bsz, t, _ = x.shape
    n = bsz * t
    x2 = x.reshape(n, D_MODEL)
    tabs = _rope_tables(jnp.arange(t, dtype=I32))
    (qn, cmp_rows, slc_rows, win_rows, qm, moba_rows, qf, fox_rows, z, xbc, small, dt) = _proj_call(
        x2, w['gmix'], w['wp'], w['gains'], tabs, w['biases'], tm, precise)
    b3 = lambda a: a.reshape(bsz, t, a.shape[-1])
    qn, qm, qf = b3(qn), b3(qm), b3(qf)
    kvc = _cmp_mlp_call(cmp_rows.reshape(bsz, t // NSA_CMP_STRIDE, NSA_CMP_STRIDE * LANE), *w['cmp'],
                        precise=precise)
    n_cmp = (t - NSA_CMP_LEN) // NSA_CMP_STRIDE + 1
    n_sel = -(-t // NSA_SEL_BLOCK)
    nselp = -(-n_sel // LANE) * LANE
    o_cmp, sel, _ = _nsa_cmp_call(qn, kvc, tq=tq, q_pos0=0, n_cmp=n_cmp, n_sel=n_sel, nselp=nselp,
                                  precise=precise)
    o_slc = _attn_call(qn, b3(slc_rows), tq=tq, tk=tk, bm=sel, bm_bs=NSA_SEL_BLOCK, precise=precise,
                       name="nsa_slc")
    o_win = _attn_call(qn, b3(win_rows), tq=tq, tk=tk, window=NSA_WINDOW, precise=precise,
                       name="nsa_win")
    moba3 = b3(moba_rows)
    kmean = _pad_rows(_kmean_call(moba3), LANE)
    n_blk = -(-t // MOBA_BLOCK)
    bm = _moba_gate_call(qm, kmean, tq=tq, k_top=min(MOBA_TOPK, n_blk - 1), precise=precise)
    o_moba = _attn_call(qm, moba3, tq=tq, tk=tk, bm=bm, bm_bs=MOBA_BLOCK, bm_hs=LANE, precise=precise,
                        name="moba")
    logf = small[:, SM_FOXF:SM_FOXF + N_HEADS].reshape(bsz, t, N_HEADS)
    cum_t = _cum_call(_heads_to_sublanes(logf, t))
    cq = jnp.pad(jnp.swapaxes(cum_t, 1, 2), ((0, 0), (0, 0), (0, LANE - SUB)))
    o_fox = _attn_call(qf, b3(fox_rows), tq=tq, tk=tk, cq=cq, cum_t=cum_t, precise=precise, name="fox")
    y_ssd, ssd_state = _ssd_call(b3(xbc), b3(z), b3(dt), w['convw'], w['convb'], w['hp'], w['norm'],
                                 precise)
    f2 = lambda a: a.reshape(n, a.shape[-1])
    x1, xn2, comb = _post_call(x2, w['gmix'], w['wgate'], small, f2(o_cmp), f2(o_slc), f2(o_win),
                               f2(o_moba), f2(o_fox), f2(y_ssd), w['wbr'], w['wout'], w['gffn'],
                               w['wr'], w['br'], tm, precise)
    y = _moe_call(xn2, x1, comb, w_ffn['weg'], w_ffn['weu'], w_ffn['wed'], min(n, 4 * tm))
    wb = min(NSA_WINDOW, t)
    state = (cmp_rows.reshape(bsz, t, 2, HEAD_DIM), slc_rows.reshape(bsz, t, 2, HEAD_DIM),
             win_rows.reshape(bsz, t, 2, HEAD_DIM)[:, t - wb:],
             moba_rows.reshape(bsz, t, 2, N_HEADS, HEAD_DIM), fox_rows.reshape(bsz, t, 2, N_HEADS, HEAD_DIM),
             logf, ssd_state, b3(xbc)[:, t - (SSD_CONV - 1):])
    return y.reshape(bsz, t, D_MODEL), state


def _cache_views(cache_nsa_cmp, cache_nsa_slc, state_nsa_win, cache_moba, cache_fox, cache_fox_logf):
    rows_last = lambda c: jnp.moveaxis(c, 2, -1)
    return (rows_last(cache_nsa_cmp), rows_last(cache_nsa_slc)[:, :, :, None],
            rows_last(state_nsa_win)[:, :, :, None], rows_last(cache_moba), rows_last(cache_fox),
            rows_last(cache_fox_logf))


def _layer_sample(x, w, views, layer, s_win, s_ssd, s_conv, page_table, *, g_pages):
    cmp_t, slc_t, win_t, moba_t, fox_t, logf_t = views
    bsz = x.shape[0]
    n_pages = page_table.shape[1]
    pos0 = n_pages * PAGE
    x2 = x.reshape(bsz, D_MODEL)
    tabs = _rope_tables(jnp.full((bsz,), pos0, I32))
    (qn, cmp_rows, slc_rows, win_rows, qm, moba_rows, qf, fox_rows, z, xbc, small, dt) = _proj_call(
        x2, w['gmix'], w['wp'], w['gains'], tabs, w['biases'], bsz, precise=True)
    q8 = lambda a: _pad_rows(a[:, None, :], SUB)
    qn, qm, qf = q8(qn), q8(qm), q8(qf)
    every_head = lambda n: jnp.full((bsz, n), -1, I32)
    assert (pos0 + 1 - NSA_CMP_LEN) // NSA_CMP_STRIDE + 1 == pos0 // NSA_CMP_STRIDE - 1
    kvc = _kt_cmp_call(cmp_t, layer, page_table, *w['cmp'], g=g_pages)
    n_cmp = pos0 // NSA_CMP_STRIDE - 1
    n_sel = -(-(pos0 + 1) // NSA_SEL_BLOCK)
    nselp = -(-n_sel // LANE) * LANE
    o_cmp, _, sel_idx = _nsa_cmp_call(qn, kvc, tq=SUB, q_pos0=pos0, n_cmp=n_cmp, n_sel=n_sel,
                                      nselp=nselp, precise=True)
    k_top = min(NSA_SEL_TOPK, n_sel)
    sel_ids = sel_idx[:, 0, :k_top]
    sel_heads = jnp.where(sel_ids < pos0 // NSA_SEL_BLOCK, -1, N_HEADS)
    o_slc = _kt_attn_call(qn, slc_t, layer, page_table, sel_ids, sel_heads, q8(slc_rows),
                          r=NSA_SEL_BLOCK, g=k_top, name="nsa_slc_s")
    wb = s_win.shape[1]
    o_win = _kt_attn_call(qn, win_t, layer, jnp.arange(bsz, dtype=I32)[:, None], jnp.zeros((bsz, 1), I32),
                          every_head(1), q8(win_rows), r=wb, g=1, lo=wb - NSA_WINDOW + 1, name="nsa_win_s")
    n_blk = -(-(pos0 + 1) // MOBA_BLOCK)
    m_top = min(MOBA_TOPK, n_blk - 1)
    blk_ids = _kt_gate_call(qm, moba_t, layer, page_table, g=g_pages, k_top=m_top)[:, :N_HEADS, :m_top]
    ppb = MOBA_BLOCK // PAGE
    moba_fetch = (blk_ids[..., None] * ppb + jnp.arange(ppb, dtype=I32)).reshape(bsz, -1)
    head_id = jnp.arange(N_HEADS, dtype=I32)[None, :, None, None]
    moba_heads = jnp.broadcast_to(jnp.where(blk_ids[..., None] >= 0, head_id, N_HEADS),
                                  blk_ids.shape + (ppb,)).reshape(bsz, -1)
    o_moba = _kt_attn_call(qm, moba_t, layer, page_table, moba_fetch, moba_heads, q8(moba_rows), r=PAGE,
                           g=m_top * ppb, one_head=True, name="moba_s")
    logf_new = small[:, SM_FOXF:SM_FOXF + N_HEADS]
    fox_fetch = jnp.tile(jnp.arange(n_pages - 1, -1, -1, dtype=I32)[None, :], (bsz, 1))
    o_fox = _kt_attn_call(qf, fox_t, layer, page_table, fox_fetch, every_head(n_pages), q8(fox_rows),
                          r=PAGE, g=g_pages, logf_t=logf_t, logf_new=q8(small), name="fox_s")
    full = jnp.concatenate([s_conv, xbc[:, None, :]], axis=1)
    y_ssd, ssd_state = _ssd_step_call(_pad_rows(full, SUB), z[:, None, :], dt[:, None, :], s_ssd,
                                      w['w8'], w['convb'], w['hp'], w['norm'])
    first = lambda a: a[:, 0, :]
    x1, xn2, comb = _post_call(x2, w['gmix'], w['wgate'], small, first(o_cmp), first(o_slc), first(o_win),
                               first(o_moba), first(o_fox), first(y_ssd), w['wbr'], w['wout'],
                               w['gffn'], w['wr'], w['br'], bsz, precise=True)
    y = _moe_call(xn2, x1, comb, w['weg'], w['weu'], w['wed'], bsz, precise=True)
    state = (cmp_rows.reshape(bsz, 1, 2, HEAD_DIM), slc_rows.reshape(bsz, 1, 2, HEAD_DIM),
             jnp.concatenate([s_win[:, 1:], win_rows.reshape(bsz, 1, 2, HEAD_DIM)], axis=1),
             moba_rows.reshape(bsz, 1, 2, N_HEADS, HEAD_DIM), fox_rows.reshape(bsz, 1, 2, N_HEADS, HEAD_DIM),
             logf_new[:, None, :], ssd_state, full[:, 1:])
    return y.reshape(bsz, 1, D_MODEL), state


_PARAM_NAMES = ('norm_mix', 'norm_ffn', 'w_in', 'g_nsa_q', 'g_nsa_k', 'g_nsa_kc', 'pe_cmp_k', 'pe_cmp_v',
                'w_cmp_k1', 'w_cmp_k2', 'w_cmp_v1', 'w_cmp_v2', 'g_moba_q', 'g_moba_k', 'g_fox_q',
                'g_fox_k', 'b_fox_f', 'ssd_conv_w', 'ssd_conv_b', 'ssd_dt_bias', 'ssd_a_log', 'ssd_d',
                'ssd_norm', 'w_br_nsa', 'w_br_moba', 'w_br_fox', 'w_br_ssd', 'w_out', 'w_router_grp',
                'b_router_grp', 'w_router_exp', 'b_router_exp', 'w_exp_gate', 'w_exp_up', 'w_exp_down')


def kernel(x_prompt, x_sample, cache_nsa_cmp, cache_nsa_slc, state_nsa_win, cache_moba, cache_fox, cache_fox_logf, state_ssd, state_ssd_conv, page_table, norm_mix, norm_ffn, w_in, g_nsa_q, g_nsa_k, g_nsa_kc, pe_cmp_k, pe_cmp_v, w_cmp_k1, w_cmp_k2, w_cmp_v1, w_cmp_v2, g_moba_q, g_moba_k, g_fox_q, g_fox_k, b_fox_f, ssd_conv_w, ssd_conv_b, ssd_dt_bias, ssd_a_log, ssd_d, ssd_norm, w_br_nsa, w_br_moba, w_br_fox, w_br_ssd, w_out, w_router_grp, b_router_grp, w_router_exp, b_router_exp, w_exp_gate, w_exp_up, w_exp_down):
    params = dict(zip(_PARAM_NAMES, (
        norm_mix, norm_ffn, w_in, g_nsa_q, g_nsa_k, g_nsa_kc, pe_cmp_k, pe_cmp_v, w_cmp_k1, w_cmp_k2,
        w_cmp_v1, w_cmp_v2, g_moba_q, g_moba_k, g_fox_q, g_fox_k, b_fox_f, ssd_conv_w, ssd_conv_b,
        ssd_dt_bias, ssd_a_log, ssd_d, ssd_norm, w_br_nsa, w_br_moba, w_br_fox, w_br_ssd, w_out,
        w_router_grp, b_router_grp, w_router_exp, b_router_exp, w_exp_gate, w_exp_up, w_exp_down)))
    depth = norm_mix.shape[0]
    views = _cache_views(cache_nsa_cmp, cache_nsa_slc, state_nsa_win, cache_moba, cache_fox, cache_fox_logf)
    xp, xs = x_prompt, x_sample
    sp, ss = [], []
    for l in range(depth):
        w_fast, w_exact = _layer_weights({k: v[l] for k, v in params.items()})
        feeds_next = l < depth - 1
        xp, st_p = _layer_prompt(xp, w_exact if feeds_next else w_fast, w_fast, tm=256, tq=256, tk=512,
                                 precise=feeds_next)
        xs, st_s = _layer_sample(xs, w_exact, views, l, state_nsa_win[l], state_ssd[l], state_ssd_conv[l],
                                 page_table, g_pages=16)
        sp.append(st_p)
        ss.append(st_s)
    outs = [xp, xs]
    for i in range(8):
        outs.append(jnp.stack([s[i] for s in sp], axis=0))
        outs.append(jnp.stack([s[i] for s in ss], axis=0))
    return tuple(outs)
```

```python
import functools
import math

import jax
import jax.numpy as jnp
import numpy as np
from jax import lax
from jax.experimental import pallas as pl
from jax.experimental.pallas import tpu as pltpu

F32 = jnp.float32
BF16 = jnp.bfloat16
I32 = jnp.int32

D_MODEL = 1024
HEAD_DIM = 64
N_HEADS = 4
HEADS_W = N_HEADS * HEAD_DIM
ROPE_DIM = HEAD_DIM // 4
ROPE_HALF = ROPE_DIM // 2
ROPE_THETA = 500000.0
SCALE = HEAD_DIM ** -0.5
EPS = 1e-6
NEG = -1e30
PAGE = 128

NSA_CMP_LEN = 32
NSA_CMP_STRIDE = 16
NSA_CMP_HIDDEN = 2 * HEAD_DIM
NSA_SEL_BLOCK = 64
NSA_SEL_TOPK = 16
NSA_WINDOW = 512
MOBA_BLOCK = 256
MOBA_TOPK = 3

SSD_HEADS = 8
SSD_HEAD_DIM = 64
SSD_INNER = SSD_HEADS * SSD_HEAD_DIM
SSD_STATE = 64
SSD_GROUPS = 2
SSD_CONV = 4
SSD_CHUNK = 128
SSD_BC = SSD_GROUPS * SSD_STATE
SSD_CONV_DIM = SSD_INNER + 2 * SSD_BC

MOE_GROUPS = 4
MOE_PER_GROUP = 4
MOE_EXPERTS = 16
MOE_HIDDEN = 512

LANE = 128
SUB = 8
VMEM_LIMIT = 56 * 1024 * 1024

_IN_SIZES = (HEADS_W, 6 * HEAD_DIM, 3 * N_HEADS, 3 * HEADS_W, 3 * HEADS_W, N_HEADS,
             2 * SSD_INNER + 2 * SSD_BC + SSD_HEADS, 4 * D_MODEL)
_IN_OFFS = tuple(int(sum(_IN_SIZES[:i])) for i in range(len(_IN_SIZES) + 1))

P_NSA_Q = 0
P_NSA_KV = 256
P_MOBA = 640
P_FOX = 1408
P_SSD = 2176
P_SMALL = 3456
P_DT = 3584
P_WIDTH = 3712
SM_FOXF = 0
SM_NSAG = 4


def _cparams(sem):
    return pltpu.CompilerParams(dimension_semantics=sem, vmem_limit_bytes=VMEM_LIMIT)


def _sigmoid(x):
    return 1.0 / (1.0 + jnp.exp(-x))


def _silu(x):
    return x * _sigmoid(x)


def _softplus(x):
    return jnp.maximum(x, 0.0) + jnp.log(1.0 + jnp.exp(-jnp.abs(x)))


def _dot(a, b):
    return jnp.dot(a, b, preferred_element_type=F32)


def _dot_nt(a, b):
    return lax.dot_general(a, b, (((1,), (1,)), ((), ())), preferred_element_type=F32)


def _split2(x):
    hi = x.astype(BF16)
    lo = (x - hi.astype(F32)).astype(BF16)
    return hi, lo


def _split3(x):
    h1 = x.astype(BF16)
    r = x - h1.astype(F32)
    h2 = r.astype(BF16)
    h3 = (r - h2.astype(F32)).astype(BF16)
    return h1, h2, h3


def _mm(a, b, precise):
    if precise:
        ah, al = _split2(a)
        bh, bl = _split2(b)
        return _dot(ah, bh) + _dot(ah, bl) + _dot(al, bh)
    return _dot(a.astype(BF16), b.astype(BF16))


def _mm_nt(a, b, precise):
    if precise:
        ah, al = _split2(a)
        bh, bl = _split2(b)
        return _dot_nt(ah, bh) + _dot_nt(ah, bl) + _dot_nt(al, bh)
    return _dot_nt(a.astype(BF16), b.astype(BF16))


def _dot_exact(x, e):
    h1, h2, h3 = _split3(x)
    return _dot(h1, e) + _dot(h2, e) + _dot(h3, e)


def _head_sumsq(y, precise=False):
    r = lax.broadcasted_iota(I32, (LANE, LANE), 0) // HEAD_DIM
    c = lax.broadcasted_iota(I32, (LANE, LANE), 1) // HEAD_DIM
    e = jnp.where(r == c, 1.0, 0.0).astype(BF16)
    if precise:
        return _dot_exact(y * y, e)
    hi, lo = _split2(y * y)
    return _dot(hi, e) + _dot(lo, e)


def _rope_chunk(y, cos_t, sin_a, sin_b):
    return (y * cos_t + pltpu.roll(y, LANE - ROPE_HALF, 1) * sin_a
            + pltpu.roll(y, ROPE_HALF, 1) * sin_b)


def _proj_kernel(x_ref, gmix_ref, w_ref, gains_ref, tab_ref, bias_ref,
                 qn_ref, cmp_ref, slc_ref, win_ref, qm_ref, moba_ref, qf_ref, fox_ref,
                 z_ref, xbc_ref, small_ref, dt_ref, *, precise):
    x = x_ref[...]
    ms = jnp.mean(x * x, axis=-1, keepdims=True)
    xn = x * lax.rsqrt(ms + EPS) * gmix_ref[...]
    xn = xn if precise else xn.astype(BF16)
    cos_t = tab_ref[0]
    sin_a = tab_ref[1]
    sin_b = tab_ref[2]
    lane = lax.broadcasted_iota(I32, (1, LANE), 1)
    first = lane < HEAD_DIM

    def proj(off, width):
        return _mm(xn, w_ref[:, off:off + width], precise)

    def normed(y, gain_row, k_only=False):
        g = gains_ref[gain_row:gain_row + 1, :]
        yn = y * lax.rsqrt(_head_sumsq(y, precise) * (1.0 / HEAD_DIM) + EPS) * g
        return jnp.where(first, yn, y) if k_only else yn

    p = proj(P_NSA_Q, HEADS_W)
    for c in range(2):
        y = _rope_chunk(normed(p[:, c * LANE:(c + 1) * LANE], 0), cos_t, sin_a, sin_b)
        qn_ref[:, c * LANE:(c + 1) * LANE] = (y * SCALE).astype(qn_ref.dtype)
    p = proj(P_NSA_KV, 3 * LANE)
    cos_k = jnp.where(first, cos_t, 1.0)
    sin_ak = jnp.where(first, sin_a, 0.0)
    sin_bk = jnp.where(first, sin_b, 0.0)
    for c, ref in enumerate((cmp_ref, slc_ref, win_ref)):
        y = normed(p[:, c * LANE:(c + 1) * LANE], 1 + c, k_only=True)
        ref[...] = _rope_chunk(y, cos_k, sin_ak, sin_bk)
    p = proj(P_MOBA, 3 * HEADS_W)
    for c in range(2):
        y = _rope_chunk(normed(p[:, c * LANE:(c + 1) * LANE], 4), cos_t, sin_a, sin_b)
        qm_ref[:, c * LANE:(c + 1) * LANE] = (y * SCALE).astype(qn_ref.dtype)
        y = _rope_chunk(normed(p[:, HEADS_W + c * LANE:HEADS_W + (c + 1) * LANE], 5),
                        cos_t, sin_a, sin_b)
        moba_ref[:, c * LANE:(c + 1) * LANE] = y
    moba_ref[:, HEADS_W:] = p[:, 2 * HEADS_W:]
    p = proj(P_FOX, 3 * HEADS_W)
    for c in range(2):
        y = normed(p[:, c * LANE:(c + 1) * LANE], 6)
        qf_ref[:, c * LANE:(c + 1) * LANE] = (y * SCALE).astype(qn_ref.dtype)
        fox_ref[:, c * LANE:(c + 1) * LANE] = normed(
            p[:, HEADS_W + c * LANE:HEADS_W + (c + 1) * LANE], 7)
    fox_ref[:, HEADS_W:] = p[:, 2 * HEADS_W:]
    p = proj(P_SSD, 2 * SSD_INNER + 2 * SSD_BC)
    z_ref[...] = p[:, :SSD_INNER]
    xbc_ref[...] = p[:, SSD_INNER:]
    p = proj(P_SMALL, LANE)
    logf = -_softplus(-(p + bias_ref[0:1, :]))
    small_ref[...] = jnp.where(lane < SM_NSAG, logf, p)
    p = proj(P_DT, LANE)
    dt_ref[...] = _softplus(p + bias_ref[1:2, :])


def _proj_call(x2, gmix, wp, gains, tabs, biases, tm, precise=False):
    n = x2.shape[0]
    nt = tabs.shape[1] // tm
    row = lambda w: pl.BlockSpec((tm, w), lambda i: (i, 0))
    full = lambda a: pl.BlockSpec(a.shape, lambda i: (0,) * a.ndim)
    widths = (HEADS_W, LANE, LANE, LANE, HEADS_W, 2 * HEADS_W, HEADS_W, 2 * HEADS_W,
              SSD_INNER, SSD_CONV_DIM, LANE, LANE)
    qd = F32 if precise else BF16
    dtypes = (qd, F32, F32, F32, qd, F32, qd, F32, F32, F32, F32, F32)
    return pl.pallas_call(
        functools.partial(_proj_kernel, precise=precise),
        grid=(n // tm,),
        in_specs=[row(D_MODEL), full(gmix), full(wp), full(gains),
                  pl.BlockSpec((3, tm, LANE), lambda i: (0, i % nt, 0)), full(biases)],
        out_specs=[row(w) for w in widths],
        out_shape=[jax.ShapeDtypeStruct((n, w), d) for w, d in zip(widths, dtypes)],
        compiler_params=_cparams(("parallel",)),
        name="proj",
    )(x2, gmix, wp, gains, tabs, biases)


def _rope_tables(pos):
    inv = jnp.power(ROPE_THETA, -jnp.arange(ROPE_HALF, dtype=F32) / ROPE_HALF)
    ang = pos.astype(F32)[:, None] * inv[None, :]
    cos, sin = jnp.cos(ang), jnp.sin(ang)
    t = pos.shape[0]
    one = jnp.ones((t, HEAD_DIM - ROPE_DIM), F32)
    zero = jnp.zeros((t, HEAD_DIM - ROPE_DIM), F32)
    zh = jnp.zeros((t, ROPE_HALF), F32)
    c = jnp.concatenate([cos, cos, one], axis=1)
    a = jnp.concatenate([-sin, zh, zero], axis=1)
    b = jnp.concatenate([zh, sin, zero], axis=1)
    return jnp.stack([jnp.tile(c, (1, 2)), jnp.tile(a, (1, 2)), jnp.tile(b, (1, 2))], axis=0)


def _pack_w_in(w_in):
    o = _IN_OFFS
    ssd = w_in[:, o[6]:o[7]]
    small = jnp.concatenate([w_in[:, o[5]:o[6]], w_in[:, o[2]:o[3]],
                             jnp.zeros((D_MODEL, LANE - 4 * N_HEADS), F32)], axis=1)
    dt = jnp.concatenate([ssd[:, 2 * SSD_INNER + 2 * SSD_BC:],
                          jnp.zeros((D_MODEL, LANE - SSD_HEADS), F32)], axis=1)
    wp = jnp.concatenate([w_in[:, o[0]:o[2]], w_in[:, o[3]:o[5]],
                          ssd[:, :2 * SSD_INNER + 2 * SSD_BC], small, dt], axis=1)
    return wp, w_in[:, o[7]:o[8]]


def _proj_gains(prm):
    two = lambda g: jnp.tile(g, 2)
    ones = jnp.ones((HEAD_DIM,), F32)
    rows = [two(prm['g_nsa_q'])]
    rows += [jnp.concatenate([prm['g_nsa_k'][i], ones]) for i in range(3)]
    rows += [two(prm['g_moba_q']), two(prm['g_moba_k']), two(prm['g_fox_q']), two(prm['g_fox_k'])]
    return jnp.stack(rows, axis=0)


def _proj_biases(prm):
    pad = lambda v: jnp.concatenate([v.astype(F32), jnp.zeros((LANE - v.shape[0],), F32)])
    rows = [pad(prm['b_fox_f']), pad(prm['ssd_dt_bias'])] + [jnp.zeros((LANE,), F32)] * 6
    return jnp.stack(rows, axis=0)


def _cmp_mlp_kernel(r_ref, wab_ref, pe_ref, w2_ref, g_ref, out_ref, *, precise):
    n16 = r_ref.shape[1]
    hw = 2 * NSA_CMP_HIDDEN
    h = _mm(r_ref[0], wab_ref[...], precise)
    c = _mm(pe_ref[...], wab_ref[...], precise)
    c = c[0:1, :hw] + c[1:2, hw:]
    pre = h[:, :hw] + pltpu.roll(h[:, hw:], n16 - 1, 0) + c
    act = 0.5 * pre * (1.0 + jnp.tanh(math.sqrt(2.0 / math.pi) * (pre + 0.044715 * pre * pre * pre)))
    y = _mm(act, w2_ref[...], precise)
    lane = lax.broadcasted_iota(I32, (1, LANE), 1)
    yn = y * lax.rsqrt(_head_sumsq(y, precise) * (1.0 / HEAD_DIM) + EPS) * g_ref[...]
    out_ref[0] = jnp.where(lane < HEAD_DIM, yn, y)


def _cmp_mlp_call(rows16, wab, pe2, w2, gkc, precise=False):
    bsz, n16, w = rows16.shape
    full = lambda a: pl.BlockSpec(a.shape, lambda b: (0,) * a.ndim)
    return pl.pallas_call(
        functools.partial(_cmp_mlp_kernel, precise=precise),
        grid=(bsz,),
        in_specs=[pl.BlockSpec((1, n16, w), lambda b: (b, 0, 0)), full(wab), full(pe2), full(w2),
                  full(gkc)],
        out_specs=pl.BlockSpec((1, n16, LANE), lambda b: (b, 0, 0)),
        out_shape=jax.ShapeDtypeStruct((bsz, n16, LANE), F32),
        compiler_params=_cparams(("parallel",)),
        name="cmp_mlp",
    )(rows16, wab, pe2, w2, gkc)


def _cmp_weights(prm, dtype):
    def expand(w1, slot):
        w = w1.reshape(2, NSA_CMP_STRIDE, HEAD_DIM, NSA_CMP_HIDDEN)
        z = jnp.zeros_like(w)
        pair = (w, z) if slot == 0 else (z, w)
        return jnp.concatenate(pair, axis=2).reshape(2, NSA_CMP_STRIDE * LANE, NSA_CMP_HIDDEN)
    wk, wv = expand(prm['w_cmp_k1'], 0), expand(prm['w_cmp_v1'], 1)
    wab = jnp.concatenate([wk[0], wv[0], wk[1], wv[1]], axis=1).astype(dtype)
    pe = jnp.concatenate([prm['pe_cmp_k'], prm['pe_cmp_v']], axis=1)
    pe = pe.reshape(2, NSA_CMP_STRIDE * LANE)
    pe2 = jnp.concatenate([pe, jnp.zeros((SUB - 2, NSA_CMP_STRIDE * LANE), F32)], axis=0)
    z = jnp.zeros((NSA_CMP_HIDDEN, HEAD_DIM), F32)
    w2 = jnp.concatenate([jnp.concatenate([prm['w_cmp_k2'], z], axis=1),
                          jnp.concatenate([z, prm['w_cmp_v2']], axis=1)], axis=0).astype(dtype)
    gkc = jnp.concatenate([prm['g_nsa_kc'], jnp.ones((HEAD_DIM,), F32)])[None, :]
    return wab, pe2, w2, gkc


def _first_argmax(score, lane):
    m = jnp.max(score, axis=-1, keepdims=True)
    cand = jnp.where(score == m, lane.astype(F32), float(score.shape[-1]))
    return m, jnp.min(cand, axis=-1, keepdims=True).astype(I32)


def _nsa_cmp_kernel(q_ref, kvc_ref, ocmp_ref, sel_ref, idx_ref, *, tq, q_pos0, n_cmp, n_sel, k_top,
                    precise):
    n16 = kvc_ref.shape[1]
    nselp = sel_ref.shape[2]
    qi = pl.program_id(1)
    qpos = q_pos0 + qi * tq + lax.broadcasted_iota(I32, (tq, 1), 0)
    q = q_ref[0]
    kvc = kvc_ref[0]
    kc = kvc[:, :HEAD_DIM] if precise else kvc[:, :HEAD_DIM].astype(BF16)
    vc = kvc[:, HEAD_DIM:] if precise else kvc[:, HEAD_DIM:].astype(BF16)
    n_idx = lax.broadcasted_iota(I32, (1, n16), 1)
    mask = jnp.logical_and(n_idx * NSA_CMP_STRIDE + (NSA_CMP_LEN - 1) <= qpos, n_idx < n_cmp)
    psum = jnp.zeros((tq, n16), F32)
    outs = []
    for h in range(N_HEADS):
        lg = jnp.where(mask, _mm_nt(q[:, h * HEAD_DIM:(h + 1) * HEAD_DIM], kc, precise), NEG)
        m = jnp.max(lg, axis=-1, keepdims=True)
        p = jnp.where(mask, jnp.exp(lg - m), 0.0)
        p = p / jnp.maximum(jnp.sum(p, axis=-1, keepdims=True), 1e-30)
        outs.append(_mm(p, vc, precise))
        psum = psum + p
    ocmp_ref[0] = jnp.concatenate(outs, axis=-1)
    n_col = lax.broadcasted_iota(I32, (n16, nselp), 0) * NSA_CMP_STRIDE
    s_row = lax.broadcasted_iota(I32, (n16, nselp), 1) * NSA_SEL_BLOCK
    cover = jnp.logical_and(n_col <= s_row + (NSA_SEL_BLOCK - 1), n_col + (NSA_CMP_LEN - 1) >= s_row)
    cover = jnp.where(cover, 1.0, 0.0).astype(BF16)
    imp = _dot_exact(psum, cover)
    blk = lax.broadcasted_iota(I32, (1, nselp), 1)
    cur = qpos // NSA_SEL_BLOCK
    forced = jnp.logical_or(blk == 0, jnp.logical_or(blk == cur, blk == cur - 1))
    score = jnp.where(blk > cur, -1e9, jnp.where(forced, 1e9, imp))
    score = jnp.where(blk < n_sel, score, -jnp.inf)
    sel = jnp.zeros((tq, nselp), F32)
    lane = lax.broadcasted_iota(I32, (1, LANE), 1)
    picks = jnp.full((tq, LANE), -1, I32)
    for i in range(k_top):
        _, idx = _first_argmax(score, blk)
        hit = blk == idx
        sel = jnp.where(hit, 1.0, sel)
        picks = jnp.where(lane == i, idx, picks)
        score = jnp.where(hit, -jnp.inf, score)
    sel_ref[0] = sel
    idx_ref[0] = picks


def _nsa_cmp_call(q, kvc, *, tq, q_pos0, n_cmp, n_sel, nselp, precise=False):
    bsz, t, _ = q.shape
    n16 = kvc.shape[1]
    kern = functools.partial(_nsa_cmp_kernel, tq=tq, q_pos0=q_pos0, n_cmp=n_cmp, n_sel=n_sel,
                             k_top=min(NSA_SEL_TOPK, n_sel), precise=precise)
    return pl.pallas_call(
        kern,
        grid=(bsz, t // tq),
        in_specs=[pl.BlockSpec((1, tq, HEADS_W), lambda b, i: (b, i, 0)),
                  pl.BlockSpec((1, n16, LANE), lambda b, i: (b, 0, 0))],
        out_specs=[pl.BlockSpec((1, tq, HEADS_W), lambda b, i: (b, i, 0)),
                   pl.BlockSpec((1, tq, nselp), lambda b, i: (b, i, 0)),
                   pl.BlockSpec((1, tq, LANE), lambda b, i: (b, i, 0))],
        out_shape=[jax.ShapeDtypeStruct((bsz, t, HEADS_W), F32),
                   jax.ShapeDtypeStruct((bsz, t, nselp), F32),
                   jax.ShapeDtypeStruct((bsz, t, LANE), I32)],
        compiler_params=_cparams(("parallel", "parallel")),
        name="nsa_cmp",
    )(q, kvc)


ATTN_CHUNK = 128


def _attn_pairs(nq, tq, tk, window):
    qi_l, ki_l, first, last = [], [], [], []
    for qi in range(nq):
        lo = max(qi * tq - (window - 1), 0) // tk if window else 0
        hi = (qi * tq + tq - 1) // tk
        for ki in range(lo, hi + 1):
            qi_l.append(qi)
            ki_l.append(ki)
            first.append(int(ki == lo))
            last.append(int(ki == hi))
    return tuple(np.asarray(a, np.int32) for a in (qi_l, ki_l, first, last))


def _mm_tn(a, b, precise):
    dims = (((0,), (0,)), ((), ()))
    dot = lambda x, y: lax.dot_general(x, y, dims, preferred_element_type=F32)
    if precise:
        ah, al = _split2(a)
        bh, bl = _split2(b)
        return dot(ah, bh) + dot(ah, bl) + dot(al, bh)
    return dot(a.astype(BF16), b.astype(BF16))


def _attn_kernel(qi_ref, ki_ref, first_ref, last_ref, *refs, cfg):
    tq, tk, kw, precise = cfg['tq'], cfg['tk'], cfg['kw'], cfg['precise']
    refs = list(refs)
    q_ref, kv_ref = refs.pop(0), refs.pop(0)
    bm_ref = refs.pop(0) if cfg['bm_bs'] else None
    if cfg['bias']:
        cqt_ref, cumk_ref = refs.pop(0), refs.pop(0)
    o_ref, m_ref, l_ref, acc_ref, qt_ref = refs[:5]
    bmt_ref = refs[5] if bm_ref is not None else None
    step = pl.program_id(1)
    qi, ki = qi_ref[step], ki_ref[step]

    @pl.when(first_ref[step] == 1)
    def _():
        m_ref[...] = jnp.full(m_ref.shape, NEG, F32)
        l_ref[...] = jnp.zeros(l_ref.shape, F32)
        acc_ref[...] = jnp.zeros(acc_ref.shape, F32)
        qt = q_ref[0].astype(F32).T
        zero = jnp.zeros((HEAD_DIM, tq), F32)
        for h in range(N_HEADS):
            qh = qt[h * HEAD_DIM:(h + 1) * HEAD_DIM]
            upper = kw == HEAD_DIM or h % 2 == 0
            qpad = jnp.concatenate([qh, zero] if upper else [zero, qh], axis=0)
            hi = qpad.astype(BF16)
            qt_ref[0, h] = hi
            if precise:
                qt_ref[1, h] = (qpad - hi.astype(F32)).astype(BF16)
        if bm_ref is not None:
            bmt_ref[...] = bm_ref[0].T.astype(BF16)

    qpos = qi * tq + lax.broadcasted_iota(I32, (1, tq), 1)
    hs = cfg['bm_hs']
    n_chunks = tk // ATTN_CHUNK
    heads = range(N_HEADS)
    rows, logits = [], []
    for c in range(n_chunks):
        rc = kv_ref[0, c * ATTN_CHUNK:(c + 1) * ATTN_CHUNK, :]
        rows.append(rc)
        kpos = ki * tk + c * ATTN_CHUNK + lax.broadcasted_iota(I32, (ATTN_CHUNK, 1), 0)
        mask = kpos <= qpos
        if cfg['window']:
            mask = jnp.logical_and(mask, qpos - kpos < cfg['window'])
        neg = jnp.where(mask, 0.0, NEG)
        if bm_ref is not None:
            j = lax.broadcasted_iota(I32, (1, hs if hs else bmt_ref.shape[0]), 1)
            expand = jnp.where(j == kpos // cfg['bm_bs'], 1.0, 0.0).astype(BF16)
            if not hs:
                neg = neg + (_dot(expand, bmt_ref[...]) - 1.0) * (-NEG)
        per_head = []
        for h in heads:
            pair = 0 if kw == HEAD_DIM else h // 2
            kp = rc[:, pair * LANE:(pair + 1) * LANE]
            if precise:
                kh, kl = _split2(kp)
                s = _dot(kh, qt_ref[0, h]) + _dot(kh, qt_ref[1, h]) + _dot(kl, qt_ref[0, h])
            else:
                s = _dot(kp.astype(BF16), qt_ref[0, h])
            s = s + neg
            if cfg['bias']:
                s = s + (cqt_ref[0][h:h + 1, :] - cumk_ref[0, c * ATTN_CHUNK:(c + 1) * ATTN_CHUNK, h:h + 1])
            if bm_ref is not None and hs:
                s = s + (_dot(expand, bmt_ref[h * hs:(h + 1) * hs, :]) - 1.0) * (-NEG)
            per_head.append(s)
        logits.append(per_head)
    m = [m_ref[h] for h in heads]
    l = [l_ref[h] for h in heads]
    acc = [acc_ref[h] for h in heads]
    for c in range(n_chunks):
        probs, alphas = [], []
        for h in heads:
            s = logits[c][h]
            m_new = jnp.maximum(m[h], jnp.max(s, axis=0, keepdims=True))
            p = jnp.exp(s - m_new[0:1, :])
            alpha = jnp.exp(m[h] - m_new)
            l[h] = alpha * l[h] + jnp.sum(p, axis=0, keepdims=True)
            m[h] = m_new
            probs.append(p)
            alphas.append(alpha)
        for h in heads:
            ko = h * HEAD_DIM if kw > HEAD_DIM else 0
            v = rows[c][:, kw + ko:kw + ko + HEAD_DIM]
            acc[h] = alphas[h][0:1, :] * acc[h] + _mm_tn(v, probs[h], precise)
    for h in heads:
        m_ref[h] = m[h]
        l_ref[h] = l[h]
        acc_ref[h] = acc[h]

    @pl.when(last_ref[step] == 1)
    def _():
        out_t = jnp.concatenate(
            [acc_ref[h] / jnp.maximum(l_ref[h][0:1, :], 1e-30) for h in range(N_HEADS)], axis=0)
        o_ref[0] = out_t.T


def _attn_call(q, kv, *, tq, tk, window=0, bm=None, bm_bs=0, bm_hs=0, cq=None, cum_t=None,
               precise=False, name="attn"):
    bsz, t, _ = q.shape
    kw = kv.shape[-1] // 2
    assert t % tq == 0 and t % tk == 0 and tk % ATTN_CHUNK == 0
    tabs = _attn_pairs(t // tq, tq, tk, window)
    cfg = dict(tq=tq, tk=tk, kw=kw, bm_bs=bm_bs, bm_hs=bm_hs, bias=cq is not None, window=window,
               precise=precise)
    q_map = lambda b, s, qi, ki, fi, la: (b, qi[s], 0)
    k_map = lambda b, s, qi, ki, fi, la: (b, ki[s], 0)
    args = [q, kv]
    specs = [pl.BlockSpec((1, tq, HEADS_W), q_map), pl.BlockSpec((1, tk, 2 * kw), k_map)]
    scratch = [pltpu.VMEM((N_HEADS, SUB, tq), F32), pltpu.VMEM((N_HEADS, SUB, tq), F32),
               pltpu.VMEM((N_HEADS, HEAD_DIM, tq), F32),
               pltpu.VMEM((2 if precise else 1, N_HEADS, LANE, tq), BF16)]
    if bm is not None:
        args.append(bm)
        specs.append(pl.BlockSpec((1, tq, bm.shape[2]), q_map))
        scratch.append(pltpu.VMEM((bm.shape[2], tq), BF16))
    if cq is not None:
        args += [cum_t, cq]
        specs += [pl.BlockSpec((1, SUB, tq), lambda b, s, qi, ki, fi, la: (b, 0, qi[s])),
                  pl.BlockSpec((1, tk, LANE), k_map)]
    return pl.pallas_call(
        functools.partial(_attn_kernel, cfg=cfg),
        grid_spec=pltpu.PrefetchScalarGridSpec(
            num_scalar_prefetch=4, grid=(bsz, tabs[0].shape[0]), in_specs=specs,
            out_specs=pl.BlockSpec((1, tq, HEADS_W), q_map), scratch_shapes=scratch),
        out_shape=jax.ShapeDtypeStruct((bsz, t, HEADS_W), F32),
        compiler_params=_cparams(("parallel", "arbitrary")), name=name,
    )(*tabs, *args)


def _col_from_row(row, n):
    eye = lax.broadcasted_iota(I32, (n, n), 0) == lax.broadcasted_iota(I32, (n, n), 1)
    return jnp.sum(jnp.where(eye, jnp.broadcast_to(row, (n, n)), 0.0), axis=1, keepdims=True)


def _row_from_col(col, n):
    eye = lax.broadcasted_iota(I32, (n, n), 0) == lax.broadcasted_iota(I32, (n, n), 1)
    return jnp.sum(jnp.where(eye, jnp.broadcast_to(col, (n, n)), 0.0), axis=0, keepdims=True)


def _sublane_total(x):
    for s in (4, 2, 1):
        x = x + pltpu.roll(x, s, 0)
    return x


def _query_page_logits(qb, kt):
    prod = qb * kt
    return _sublane_total(prod.reshape(HEAD_DIM // SUB, SUB, prod.shape[1]).sum(axis=0))


def _kt_attn_kernel(pt_ref, ft_ref, hd_ref, q_ref, *refs, cfg):
    g, nk, hk, pw, r, lo = cfg['g'], cfg['nk'], cfg['hk'], cfg['pw'], cfg['r'], cfg['lo']
    bias, one_head = cfg['bias'], cfg['one_head']
    refs = list(refs)
    kv_refs = [refs.pop(0) for _ in range(g)]
    self_ref = refs.pop(0)
    if bias:
        lf_refs = [refs.pop(0) for _ in range(g)]
        lfnew_ref = refs.pop(0)
    o_ref, m_ref, l_ref, acc_ref, qb_ref = refs[:5]
    carry_ref = refs[5] if bias else None
    b, ki = pl.program_id(0), pl.program_id(1)
    sub = lax.broadcasted_iota(I32, (SUB, pw), 0)
    lane = lax.broadcasted_iota(I32, (SUB, pw), 1)
    kv_head = lambda h: h if hk == N_HEADS else 0

    @pl.when(ki == 0)
    def _():
        m_ref[...] = jnp.full(m_ref.shape, NEG, F32)
        l_ref[...] = jnp.zeros(l_ref.shape, F32)
        acc_ref[...] = jnp.zeros(acc_ref.shape, F32)
        for h in range(N_HEADS):
            qcol = _col_from_row(q_ref[0][0:1, h * HEAD_DIM:(h + 1) * HEAD_DIM], HEAD_DIM)
            qb_ref[h] = jnp.broadcast_to(qcol, (HEAD_DIM, pw))
        if bias:
            row8 = jnp.broadcast_to(lfnew_ref[0][0:1, :], (SUB, LANE))
            e8 = lax.broadcasted_iota(I32, (SUB, LANE), 0) == lax.broadcasted_iota(I32, (SUB, LANE), 1)
            col8 = jnp.sum(jnp.where(e8, row8, 0.0), axis=1, keepdims=True)
            carry_ref[...] = jnp.broadcast_to(col8, carry_ref.shape)

    carry = carry_ref[...] if bias else None
    logits, masks = [], []
    for i in range(g):
        j = ki * g + i
        blk, hd = ft_ref[b, j], hd_ref[b, j]
        lane0 = (blk * r) % pw
        ok = jnp.logical_and(jnp.logical_and(lane >= lane0, lane < lane0 + r),
                             blk * r + (lane - lane0) >= lo)
        s = jnp.full((SUB, pw), NEG, F32)
        if one_head:
            hc = jnp.minimum(hd, N_HEADS - 1)
            s = jnp.where(sub == hc, _query_page_logits(qb_ref[hc], kv_refs[i][0, 0, 0, hc]), s)
            ok = jnp.logical_and(ok, jnp.logical_and(sub == hd, hd < N_HEADS))
        else:
            for h in range(N_HEADS):
                s = jnp.where(sub == h, _query_page_logits(qb_ref[h], kv_refs[i][0, 0, 0, kv_head(h)]), s)
            ok = jnp.logical_and(ok, jnp.logical_and(sub < N_HEADS, hd < N_HEADS))
        if bias:
            lf = jnp.concatenate([lf_refs[i][0, 0], jnp.zeros((SUB - N_HEADS, pw), F32)], axis=0)
            pre, sft = lf, 1
            while sft < pw:
                pre = pre + jnp.where(lane >= sft, pltpu.roll(pre, sft, 1), 0.0)
                sft *= 2
            tot = jnp.broadcast_to(pre[:, pw - 1:pw], (SUB, pw))
            s = s + (carry + (tot - pre))
            carry = carry + tot
        logits.append(jnp.where(ok, s, NEG))
        masks.append(ok)
    if bias:
        carry_ref[...] = carry

    m_old = m_ref[...]
    m_new = m_old
    for s in logits:
        m_new = jnp.maximum(m_new, jnp.broadcast_to(jnp.max(s, axis=1, keepdims=True), (SUB, pw)))
    alpha = jnp.exp(m_old - m_new)
    l_new = alpha * l_ref[...]
    probs = []
    for s, ok in zip(logits, masks):
        p = jnp.where(ok, jnp.exp(s - m_new), 0.0)
        l_new = l_new + jnp.broadcast_to(jnp.sum(p, axis=1, keepdims=True), (SUB, pw))
        probs.append(p)
    m_ref[...] = m_new
    l_ref[...] = l_new
    for h in range(N_HEADS):
        acc = alpha[h:h + 1, :] * acc_ref[h]
        if not one_head:
            for i in range(g):
                acc = acc + probs[i][h:h + 1, :] * kv_refs[i][0, 0, 1, kv_head(h)]
        acc_ref[h] = acc
    if one_head:
        for i in range(g):
            hc = jnp.minimum(hd_ref[b, ki * g + i], N_HEADS - 1)
            prow = _sublane_total(jnp.where(sub == hc, probs[i], 0.0))[0:1, :]
            acc_ref[hc] = acc_ref[hc] + prow * kv_refs[i][0, 0, 1, hc]

    @pl.when(ki == nk - 1)
    def _():
        w = hk * HEAD_DIM
        own = self_ref[0]
        sub1 = lax.broadcasted_iota(I32, (SUB, 1), 0)
        s_own = jnp.full((SUB, 1), NEG, F32)
        for h in range(N_HEADS):
            kh = own[0:1, kv_head(h) * HEAD_DIM:(kv_head(h) + 1) * HEAD_DIM]
            qh = q_ref[0][0:1, h * HEAD_DIM:(h + 1) * HEAD_DIM]
            s_own = jnp.where(sub1 == h, jnp.sum(qh * kh, axis=1, keepdims=True), s_own)
        m_fin = m_ref[:, 0:1]
        m_tot = jnp.maximum(m_fin, s_own)
        a_fin = jnp.exp(m_fin - m_tot)
        p_own = jnp.exp(s_own - m_tot)
        l_tot = a_fin * l_ref[:, 0:1] + p_own
        outs = []
        for h in range(N_HEADS):
            vcol = _col_from_row(own[0:1, w + kv_head(h) * HEAD_DIM:w + (kv_head(h) + 1) * HEAD_DIM],
                                 HEAD_DIM)
            tot = a_fin[h:h + 1, :] * jnp.sum(acc_ref[h], axis=1, keepdims=True) + p_own[h:h + 1, :] * vcol
            outs.append(_row_from_col(tot / jnp.maximum(l_tot[h:h + 1, :], 1e-30), HEAD_DIM))
        o_ref[0] = jnp.broadcast_to(jnp.concatenate(outs, axis=1), (SUB, HEADS_W))


def _kt_attn_call(q8, cache_t, layer, page_table, fetch, heads, self8, *, r, g, lo=0, one_head=False,
                  logf_t=None, logf_new=None, name="kt_attn"):
    bsz = q8.shape[0]
    _, _, _, hk, _, pw = cache_t.shape
    nf = fetch.shape[1]
    assert nf % g == 0
    nk = nf // g
    n_pages = page_table.shape[1]
    bias = logf_t is not None
    cfg = dict(g=g, nk=nk, hk=hk, pw=pw, r=r, lo=lo, bias=bias, one_head=one_head)

    def page_of(b, ki, i, pt, ft):
        return pt[b, jnp.clip(ft[b, ki * g + i] * r // pw, 0, n_pages - 1)]

    kv_map = lambda i: (lambda b, ki, pt, ft, hd: (layer, page_of(b, ki, i, pt, ft), 0, 0, 0, 0))
    lf_map = lambda i: (lambda b, ki, pt, ft, hd: (layer, page_of(b, ki, i, pt, ft), 0, 0))
    fixed = lambda shape: pl.BlockSpec(shape, lambda b, ki, pt, ft, hd: (b, 0, 0))
    args = [q8] + [cache_t] * g + [self8]
    specs = ([fixed((1, SUB, HEADS_W))]
             + [pl.BlockSpec((1, 1, 2, hk, HEAD_DIM, pw), kv_map(i)) for i in range(g)]
             + [fixed((1, SUB, self8.shape[2]))])
    scratch = [pltpu.VMEM((SUB, pw), F32), pltpu.VMEM((SUB, pw), F32),
               pltpu.VMEM((N_HEADS, HEAD_DIM, pw), F32), pltpu.VMEM((N_HEADS, HEAD_DIM, pw), F32)]
    if bias:
        args += [logf_t] * g + [logf_new]
        specs += [pl.BlockSpec((1, 1, N_HEADS, pw), lf_map(i)) for i in range(g)] + [fixed((1, SUB, LANE))]
        scratch.append(pltpu.VMEM((SUB, pw), F32))
    return pl.pallas_call(
        functools.partial(_kt_attn_kernel, cfg=cfg),
        grid_spec=pltpu.PrefetchScalarGridSpec(
            num_scalar_prefetch=3, grid=(bsz, nk), in_specs=specs,
            out_specs=fixed((1, SUB, HEADS_W)), scratch_shapes=scratch),
        out_shape=jax.ShapeDtypeStruct((bsz, SUB, HEADS_W), F32),
        compiler_params=_cparams(("parallel", "arbitrary")),
        name=name,
    )(page_table, fetch, heads, *args)


def _kt_gate_kernel(pt_ref, q_ref, *refs, g, nk, k_top, n_blk):
    k_refs = refs[:g]
    idx_ref, qb_ref, gate_ref = refs[g:]
    ki = pl.program_id(1)
    sub = lax.broadcasted_iota(I32, (SUB, LANE), 0)
    lane = lax.broadcasted_iota(I32, (SUB, LANE), 1)
    ppb = MOBA_BLOCK // PAGE

    @pl.when(ki == 0)
    def _():
        gate_ref[...] = jnp.zeros(gate_ref.shape, F32)
        for h in range(N_HEADS):
            qcol = _col_from_row(q_ref[0][0:1, h * HEAD_DIM:(h + 1) * HEAD_DIM], HEAD_DIM)
            qb_ref[h] = jnp.broadcast_to(qcol, (HEAD_DIM, LANE))

    gate = gate_ref[...]
    for i in range(g):
        s = jnp.zeros((SUB, LANE), F32)
        for h in range(N_HEADS):
            s = jnp.where(sub == h, _query_page_logits(qb_ref[h], k_refs[i][0, 0, 0, h]), s)
        tot = jnp.sum(s, axis=1, keepdims=True) * (1.0 / MOBA_BLOCK)
        gate = gate + jnp.where(lane == (ki * g + i) // ppb, tot, 0.0)
    gate_ref[...] = gate

    @pl.when(ki == nk - 1)
    def _():
        score = jnp.where(lane < n_blk, gate, NEG)
        picks = jnp.full((SUB, LANE), -1, I32)
        for i in range(k_top):
            m, idx = _first_argmax(score, lane)
            hit = lane == idx
            picks = jnp.where(jnp.logical_and(lane == i, m > 0.5 * NEG), idx, picks)
            score = jnp.where(hit, -jnp.inf, score)
        idx_ref[0] = picks


def _kt_gate_call(q8, cache_t, layer, page_table, *, g, k_top):
    bsz = q8.shape[0]
    n_pages = page_table.shape[1]
    assert n_pages % g == 0 and n_pages * PAGE % MOBA_BLOCK == 0
    nk = n_pages // g
    n_blk = n_pages * PAGE // MOBA_BLOCK
    assert n_blk <= LANE
    k_map = lambda i: (lambda b, ki, pt: (layer, pt[b, ki * g + i], 0, 0, 0, 0))
    return pl.pallas_call(
        functools.partial(_kt_gate_kernel, g=g, nk=nk, k_top=k_top, n_blk=n_blk),
        grid_spec=pltpu.PrefetchScalarGridSpec(
            num_scalar_prefetch=1, grid=(bsz, nk),
            in_specs=[pl.BlockSpec((1, SUB, HEADS_W), lambda b, ki, pt: (b, 0, 0))]
            + [pl.BlockSpec((1, 1, 1, N_HEADS, HEAD_DIM, PAGE), k_map(i)) for i in range(g)],
            out_specs=pl.BlockSpec((1, SUB, LANE), lambda b, ki, pt: (b, 0, 0)),
            scratch_shapes=[pltpu.VMEM((N_HEADS, HEAD_DIM, LANE), F32), pltpu.VMEM((SUB, LANE), F32)]),
        out_shape=jax.ShapeDtypeStruct((bsz, SUB, LANE), I32),
        compiler_params=_cparams(("parallel", "arbitrary")),
        name="moba_gate_s",
    )(page_table, q8, *([cache_t] * g))


def _kt_cmp_kernel(pt_ref, *refs, g, nk):
    kv_refs = refs[:g]
    wc_ref, pe_ref, w2_ref, g_ref, out_ref, rows_ref = refs[g:]
    ki = pl.program_id(1)
    for i in range(g):
        off = pl.multiple_of((ki * g + i) * PAGE, PAGE)
        rows_ref[pl.ds(off, PAGE), :] = kv_refs[i][0, 0].reshape(2 * HEAD_DIM, PAGE).T

    @pl.when(ki == nk - 1)
    def _():
        n16 = rows_ref.shape[0] // NSA_CMP_STRIDE
        hw = 2 * NSA_CMP_HIDDEN
        h = jnp.zeros((n16, 2 * hw), F32)
        c = jnp.zeros((SUB, 2 * hw), F32)
        for j in range(NSA_CMP_STRIDE):
            h = h + _mm(rows_ref[pl.ds(j, n16, stride=NSA_CMP_STRIDE), :], wc_ref[j], True)
            c = c + _mm(pe_ref[j], wc_ref[j], True)
        c = c[0:1, :hw] + c[1:2, hw:]
        pre = h[:, :hw] + pltpu.roll(h[:, hw:], n16 - 1, 0) + c
        act = 0.5 * pre * (1.0 + jnp.tanh(math.sqrt(2.0 / math.pi) * (pre + 0.044715 * pre * pre * pre)))
        y = _mm(act, w2_ref[...], True)
        lane = lax.broadcasted_iota(I32, (1, LANE), 1)
        yn = y * lax.rsqrt(_head_sumsq(y, True) * (1.0 / HEAD_DIM) + EPS) * g_ref[...]
        out_ref[0] = jnp.where(lane < HEAD_DIM, yn, y)


def _kt_cmp_call(cache_t, layer, page_table, wab, pe2, w2, gkc, *, g):
    bsz, n_pages = page_table.shape
    assert n_pages % g == 0
    nk = n_pages // g
    n16 = n_pages * PAGE // NSA_CMP_STRIDE
    wc = wab.reshape(NSA_CMP_STRIDE, LANE, wab.shape[1])
    pe = jnp.swapaxes(pe2[:2].reshape(2, NSA_CMP_STRIDE, LANE), 0, 1)
    pe = jnp.pad(pe, ((0, 0), (0, SUB - 2), (0, 0)))
    full = lambda a: pl.BlockSpec(a.shape, lambda b, ki, pt: (0,) * a.ndim)
    kv_map = lambda i: (lambda b, ki, pt: (layer, pt[b, ki * g + i], 0, 0, 0))
    return pl.pallas_call(
        functools.partial(_kt_cmp_kernel, g=g, nk=nk),
        grid_spec=pltpu.PrefetchScalarGridSpec(
            num_scalar_prefetch=1, grid=(bsz, nk),
            in_specs=[pl.BlockSpec((1, 1, 2, HEAD_DIM, PAGE), kv_map(i)) for i in range(g)]
            + [full(wc), full(pe), full(w2), full(gkc)],
            out_specs=pl.BlockSpec((1, n16, LANE), lambda b, ki, pt: (b, 0, 0)),
            scratch_shapes=[pltpu.VMEM((n_pages * PAGE, LANE), F32)]),
        out_shape=jax.ShapeDtypeStruct((bsz, n16, LANE), F32),
        compiler_params=_cparams(("parallel", "arbitrary")),
        name="cmp_mlp_s",
    )(page_table, *([cache_t] * g), wc, pe, w2, gkc)


MOBA_BLOCKS_PER_STEP = 8


def _kmean_kernel(kv_ref, out_ref):
    row = lax.broadcasted_iota(I32, (MOBA_BLOCKS_PER_STEP, 1), 0)
    acc = jnp.zeros((MOBA_BLOCKS_PER_STEP, HEADS_W), F32)
    for j in range(MOBA_BLOCKS_PER_STEP):
        s = jnp.sum(kv_ref[0, j * MOBA_BLOCK:(j + 1) * MOBA_BLOCK, :], axis=0, keepdims=True)
        acc = acc + jnp.where(row == j, s, 0.0)
    out_ref[0] = acc * (1.0 / MOBA_BLOCK)


def _kmean_call(kv):
    bsz, tk, _ = kv.shape
    span = MOBA_BLOCKS_PER_STEP * MOBA_BLOCK
    assert tk % span == 0
    return pl.pallas_call(
        _kmean_kernel, grid=(bsz, tk // span),
        in_specs=[pl.BlockSpec((1, span, HEADS_W), lambda b, j: (b, j, 0))],
        out_specs=pl.BlockSpec((1, MOBA_BLOCKS_PER_STEP, HEADS_W), lambda b, j: (b, j, 0)),
        out_shape=jax.ShapeDtypeStruct((bsz, tk // MOBA_BLOCK, HEADS_W), F32),
        compiler_params=_cparams(("parallel", "parallel")), name="moba_kmean")(kv)


def _moba_gate_kernel(q_ref, km_ref, bm_ref, *, tq, k_top, precise):
    qi = pl.program_id(1)
    qpos = qi * tq + lax.broadcasted_iota(I32, (tq, 1), 0)
    cur = qpos // MOBA_BLOCK
    blk = lax.broadcasted_iota(I32, (1, LANE), 1)
    valid = blk < cur
    q = q_ref[0]
    km = km_ref[0]
    for h in range(N_HEADS):
        kmh = km[:, h * HEAD_DIM:(h + 1) * HEAD_DIM]
        qh = q[:, h * HEAD_DIM:(h + 1) * HEAD_DIM]
        if precise:
            gate = _mm_nt(qh, kmh, True)
        else:
            hi, lo = _split2(kmh)
            gate = _dot_nt(qh, hi) + _dot_nt(qh, lo)
        score = jnp.where(valid, gate, NEG)
        sel = jnp.where(blk == cur, 1.0, 0.0)
        for _ in range(k_top):
            m, idx = _first_argmax(score, blk)
            hit = blk == idx
            sel = jnp.where(jnp.logical_and(hit, m > 0.5 * NEG), 1.0, sel)
            score = jnp.where(hit, -jnp.inf, score)
        bm_ref[0, :, h * LANE:(h + 1) * LANE] = sel


def _moba_gate_call(q, kmean, *, tq, k_top, precise=False):
    bsz, t, _ = q.shape
    return pl.pallas_call(
        functools.partial(_moba_gate_kernel, tq=tq, k_top=k_top, precise=precise),
        grid=(bsz, t // tq),
        in_specs=[pl.BlockSpec((1, tq, HEADS_W), lambda b, i: (b, i, 0)),
                  pl.BlockSpec((1, LANE, HEADS_W), lambda b, i: (b, 0, 0))],
        out_specs=pl.BlockSpec((1, tq, N_HEADS * LANE), lambda b, i: (b, i, 0)),
        out_shape=jax.ShapeDtypeStruct((bsz, t, N_HEADS * LANE), F32),
        compiler_params=_cparams(("parallel", "parallel")),
        name="moba_gate",
    )(q, kmean)


CUM_CHUNK = 512


def _cum_kernel(l_ref, o_ref, carry_ref):
    @pl.when(pl.program_id(1) == 0)
    def _():
        carry_ref[...] = jnp.zeros(carry_ref.shape, F32)

    r = lax.broadcasted_iota(I32, (CUM_CHUNK, CUM_CHUNK), 0)
    c = lax.broadcasted_iota(I32, (CUM_CHUNK, CUM_CHUNK), 1)
    u = jnp.where(r <= c, 1.0, 0.0).astype(BF16)
    h1, h2, h3 = _split3(l_ref[0])
    cum = _dot(h1, u) + _dot(h2, u) + _dot(h3, u) + carry_ref[:, 0:1]
    o_ref[0] = cum
    carry_ref[...] = jnp.broadcast_to(cum[:, CUM_CHUNK - 1:CUM_CHUNK], carry_ref.shape)


def _cum_call(logf_t):
    bsz, rows, tk = logf_t.shape
    assert rows == SUB and tk % CUM_CHUNK == 0
    return pl.pallas_call(
        _cum_kernel,
        grid=(bsz, tk // CUM_CHUNK),
        in_specs=[pl.BlockSpec((1, SUB, CUM_CHUNK), lambda b, j: (b, 0, j))],
        out_specs=pl.BlockSpec((1, SUB, CUM_CHUNK), lambda b, j: (b, 0, j)),
        out_shape=jax.ShapeDtypeStruct(logf_t.shape, F32),
        scratch_shapes=[pltpu.VMEM((SUB, LANE), F32)],
        compiler_params=_cparams(("parallel", "arbitrary")),
        name="fox_cum",
    )(logf_t)


def _group_norm_gate(y, z, norm):
    y = y * _silu(z)
    gw = SSD_INNER // SSD_GROUPS
    outs = []
    for g in range(SSD_GROUPS):
        yg = y[:, g * gw:(g + 1) * gw]
        ms = jnp.mean(yg * yg, axis=-1, keepdims=True)
        outs.append(yg * lax.rsqrt(ms + EPS) * norm[:, g * gw:(g + 1) * gw])
    return jnp.concatenate(outs, axis=-1)


def _ssd_kernel(xbc_ref, z_ref, dt_ref, convw_ref, convb_ref, hp_ref, norm_ref, y_ref, state_ref,
                ext_ref, h_ref, *, nc, precise):
    q = SSD_CHUNK
    ci = pl.program_id(1)

    @pl.when(ci == 0)
    def _():
        ext_ref[0:SUB, :] = jnp.zeros((SUB, SSD_CONV_DIM), F32)
        h_ref[...] = jnp.zeros(h_ref.shape, F32)

    u = xbc_ref[0]
    ext_ref[SUB:SUB + q, :] = u
    acc = jnp.broadcast_to(convb_ref[...], (q, SSD_CONV_DIM))
    for i in range(SSD_CONV):
        k = SSD_CONV - 1 - i
        acc = acc + ext_ref[SUB - k:SUB - k + q, :] * convw_ref[i:i + 1, :]
    ext_ref[0:SUB, :] = u[q - SUB:, :]
    xbc = _silu(acc)
    xs = xbc[:, :SSD_INNER]
    dt = dt_ref[0]
    a = -jnp.exp(hp_ref[0:1, :])
    d_skip = hp_ref[1:2, :]
    row = lax.broadcasted_iota(I32, (q, 1), 0)
    cum = dt * a
    s = 1
    while s < q:
        cum = cum + jnp.where(row >= s, pltpu.roll(cum, s, 0), 0.0)
        s *= 2
    cum_t = cum.T
    tri = lax.broadcasted_iota(I32, (q, q), 0) >= lax.broadcasted_iota(I32, (q, q), 1)
    ys, xdd = [], []
    cbs = []
    for g in range(SSD_GROUPS):
        bm = xbc[:, SSD_INNER + g * SSD_STATE:SSD_INNER + (g + 1) * SSD_STATE]
        cm = xbc[:, SSD_INNER + SSD_BC + g * SSD_STATE:SSD_INNER + SSD_BC + (g + 1) * SSD_STATE]
        cbs.append((bm, cm, _mm_nt(cm, bm, precise)))
    rep = SSD_HEADS // SSD_GROUPS
    for h in range(SSD_HEADS):
        bm, cm, cb = cbs[h // rep]
        ch = cum[:, h:h + 1]
        lmat = jnp.where(tri, jnp.exp(ch - cum_t[h:h + 1, :]), 0.0)
        xh = xs[:, h * SSD_HEAD_DIM:(h + 1) * SSD_HEAD_DIM]
        xdt = xh * dt[:, h:h + 1]
        y = _mm(cb * lmat, xdt, precise)
        y = y + jnp.exp(ch) * _mm_nt(cm, h_ref[h], precise)
        ys.append(y + d_skip[:, h:h + 1] * xh)
        xdd.append(xdt * jnp.exp(cum[q - 1:q, h:h + 1] - ch))
    xdd_t = jnp.concatenate(xdd, axis=-1).T
    for h in range(SSD_HEADS):
        bm = cbs[h // rep][0]
        upd = _mm(xdd_t[h * SSD_HEAD_DIM:(h + 1) * SSD_HEAD_DIM, :], bm, precise)
        h_ref[h] = jnp.exp(cum[q - 1:q, h:h + 1]) * h_ref[h] + upd
    y_ref[0] = _group_norm_gate(jnp.concatenate(ys, axis=-1), z_ref[0], norm_ref[...])

    @pl.when(ci == nc - 1)
    def _():
        state_ref[0] = h_ref[...]


def _ssd_call(xbc, z, dt, convw, convb, hp, norm, precise=False):
    bsz, t, _ = xbc.shape
    nc = t // SSD_CHUNK
    blk = lambda w: pl.BlockSpec((1, SSD_CHUNK, w), lambda b, c: (b, c, 0))
    full = lambda a: pl.BlockSpec(a.shape, lambda b, c: (0,) * a.ndim)
    return pl.pallas_call(
        functools.partial(_ssd_kernel, nc=nc, precise=precise),
        grid=(bsz, nc),
        in_specs=[blk(SSD_CONV_DIM), blk(SSD_INNER), blk(LANE), full(convw), full(convb), full(hp),
                  full(norm)],
        out_specs=[blk(SSD_INNER),
                   pl.BlockSpec((1, SSD_HEADS, SSD_HEAD_DIM, SSD_STATE), lambda b, c: (b, 0, 0, 0))],
        out_shape=[jax.ShapeDtypeStruct((bsz, t, SSD_INNER), F32),
                   jax.ShapeDtypeStruct((bsz, SSD_HEADS, SSD_HEAD_DIM, SSD_STATE), F32)],
        scratch_shapes=[pltpu.VMEM((SUB + SSD_CHUNK, SSD_CONV_DIM), F32),
                        pltpu.VMEM((SSD_HEADS, SSD_HEAD_DIM, SSD_STATE), F32)],
        compiler_params=_cparams(("parallel", "arbitrary")),
        name="ssd_scan",
    )(xbc, z, dt, convw, convb, hp, norm)


def _ssd_step_kernel(full_ref, z_ref, dt_ref, st_ref, w8_ref, convb_ref, hp_ref, norm_ref,
                     y_ref, state_ref):
    acc = convb_ref[...] + jnp.sum(full_ref[0] * w8_ref[...], axis=0, keepdims=True)
    xbc = _silu(acc)
    xs = xbc[:, :SSD_INNER]
    dt = dt_ref[0]
    a = -jnp.exp(hp_ref[0:1, :])
    d_skip = hp_ref[1:2, :]
    n = SSD_HEAD_DIM
    eye = lax.broadcasted_iota(I32, (n, n), 0) == lax.broadcasted_iota(I32, (n, n), 1)
    rep = SSD_HEADS // SSD_GROUPS
    ys = []
    for h in range(SSD_HEADS):
        g = h // rep
        bm = xbc[:, SSD_INNER + g * SSD_STATE:SSD_INNER + (g + 1) * SSD_STATE]
        cm = xbc[:, SSD_INNER + SSD_BC + g * SSD_STATE:SSD_INNER + SSD_BC + (g + 1) * SSD_STATE]
        dth = dt[:, h:h + 1]
        xh = xs[:, h * n:(h + 1) * n]
        xcol = jnp.sum(jnp.where(eye, jnp.broadcast_to(xh * dth, (n, n)), 0.0), axis=1, keepdims=True)
        hn = jnp.exp(dth * a[:, h:h + 1]) * st_ref[0, h] + xcol * bm
        state_ref[0, h] = hn
        ycol = jnp.sum(hn * cm, axis=1, keepdims=True)
        yrow = jnp.sum(jnp.where(eye, jnp.broadcast_to(ycol, (n, n)), 0.0), axis=0, keepdims=True)
        ys.append(yrow + d_skip[:, h:h + 1] * xh)
    y_ref[0] = _group_norm_gate(jnp.concatenate(ys, axis=-1), z_ref[0], norm_ref[...])


def _ssd_step_call(full8, z, dt, state, w8, convb, hp, norm):
    bsz = full8.shape[0]
    one = lambda w: pl.BlockSpec((1, 1, w), lambda b: (b, 0, 0))
    full = lambda a: pl.BlockSpec(a.shape, lambda b: (0,) * a.ndim)
    st = pl.BlockSpec((1, SSD_HEADS, SSD_HEAD_DIM, SSD_STATE), lambda b: (b, 0, 0, 0))
    return pl.pallas_call(
        _ssd_step_kernel,
        grid=(bsz,),
        in_specs=[pl.BlockSpec((1, SUB, SSD_CONV_DIM), lambda b: (b, 0, 0)), one(SSD_INNER), one(LANE),
                  st, full(w8), full(convb), full(hp), full(norm)],
        out_specs=[one(SSD_INNER), st],
        out_shape=[jax.ShapeDtypeStruct((bsz, 1, SSD_INNER), F32),
                   jax.ShapeDtypeStruct(state.shape, F32)],
        compiler_params=_cparams(("parallel",)),
        name="ssd_step",
    )(full8, z, dt, state, w8, convb, hp, norm)


R_GROUP0 = MOE_EXPERTS


def _post_kernel(x_ref, gmix_ref, wg_ref, small_ref, ocmp_ref, oslc_ref, owin_ref, omoba_ref,
                 ofox_ref, ossd_ref, wbr_ref, wout_ref, gffn_ref, wr_ref, br_ref,
                 x1_ref, xn2_ref, comb_ref, *, precise):
    x = x_ref[...]
    xn = x * lax.rsqrt(jnp.mean(x * x, axis=-1, keepdims=True) + EPS) * gmix_ref[...]
    xn = xn if precise else xn.astype(BF16)
    r = lax.broadcasted_iota(I32, (LANE, 3 * HEADS_W), 0)
    c = lax.broadcasted_iota(I32, (LANE, 3 * HEADS_W), 1)
    pick = r == SM_NSAG + 3 * ((c % HEADS_W) // HEAD_DIM) + c // HEADS_W
    pick = jnp.where(pick, 1.0, 0.0).astype(BF16)
    gexp = _dot_exact(_sigmoid(small_ref[...]), pick)
    o_nsa = (gexp[:, :HEADS_W] * ocmp_ref[...] + gexp[:, HEADS_W:2 * HEADS_W] * oslc_ref[...]
             + gexp[:, 2 * HEADS_W:] * owin_ref[...])
    merged = jnp.zeros(x.shape, F32)
    off = 0
    for b, o in enumerate((o_nsa, omoba_ref[...], ofox_ref[...], ossd_ref[...])):
        w = o.shape[1]
        gate = _sigmoid(_mm(xn, wg_ref[:, b * D_MODEL:(b + 1) * D_MODEL], precise))
        merged = merged + gate * _mm(o, wbr_ref[off:off + w, :], precise)
        off += w
    x1 = x + _mm(merged, wout_ref[...], precise)
    x1_ref[...] = x1
    xn2 = x1 * lax.rsqrt(jnp.mean(x1 * x1, axis=-1, keepdims=True) + EPS) * gffn_ref[...]
    xn2_ref[...] = xn2.astype(xn2_ref.dtype)
    if precise:
        logits = _mm(xn2, wr_ref[0], True) + br_ref[...]
    else:
        hi, lo = _split2(xn2)
        logits = _dot(hi, wr_ref[0]) + _dot(lo, wr_ref[0]) + _dot(hi, wr_ref[1]) + br_ref[...]
    lane = lax.broadcasted_iota(I32, (1, LANE), 1)
    is_grp = jnp.logical_and(lane >= R_GROUP0, lane < R_GROUP0 + MOE_GROUPS)
    gmax, gidx = _first_argmax(jnp.where(is_grp, logits, -jnp.inf), lane)
    g_w = 1.0 / jnp.sum(jnp.where(is_grp, jnp.exp(logits - gmax), 0.0), axis=-1, keepdims=True)
    in_grp = lane // MOE_PER_GROUP == gidx - R_GROUP0
    e1 = jnp.where(in_grp, logits, -jnp.inf)
    v1, i1 = _first_argmax(e1, lane)
    e2 = jnp.where(lane == i1, -jnp.inf, e1)
    v2, i2 = _first_argmax(e2, lane)
    t = jnp.exp(v2 - v1)
    w1 = 1.0 / (1.0 + t)
    comb_ref[...] = (jnp.where(lane == i1, w1 * g_w, 0.0) + jnp.where(lane == i2, t * w1 * g_w, 0.0))


def _post_call(x2, gmix, wg, small, o_cmp, o_slc, o_win, o_moba, o_fox, o_ssd, wbr, wout, gffn,
               wr, br, tm, precise=False):
    n = x2.shape[0]
    row = lambda a: pl.BlockSpec((tm, a.shape[1]), lambda i: (i, 0))
    full = lambda a: pl.BlockSpec(a.shape, lambda i: (0,) * a.ndim)
    big = (lambda a: pl.BlockSpec(a.shape, lambda i: (0,) * a.ndim, pipeline_mode=pl.Buffered(1))
           ) if precise else full
    rows = (x2, small, o_cmp, o_slc, o_win, o_moba, o_fox, o_ssd)
    return pl.pallas_call(
        functools.partial(_post_kernel, precise=precise),
        grid=(n // tm,),
        in_specs=[row(x2), full(gmix), big(wg)] + [row(a) for a in rows[1:]]
        + [big(wbr), big(wout), full(gffn), full(wr), full(br)],
        out_specs=[pl.BlockSpec((tm, D_MODEL), lambda i: (i, 0)),
                   pl.BlockSpec((tm, D_MODEL), lambda i: (i, 0)),
                   pl.BlockSpec((tm, LANE), lambda i: (i, 0))],
        out_shape=[jax.ShapeDtypeStruct((n, D_MODEL), F32),
                   jax.ShapeDtypeStruct((n, D_MODEL), F32 if precise else BF16),
                   jax.ShapeDtypeStruct((n, LANE), F32)],
        compiler_params=_cparams(("parallel",)),
        name="merge_out",
    )(x2, gmix, wg, small, o_cmp, o_slc, o_win, o_moba, o_fox, o_ssd, wbr, wout, gffn, wr, br)


def _moe_kernel(xn_ref, x1_ref, comb_ref, wg_ref, wu_ref, wd_ref, o_ref, acc_ref, *, precise):
    e = pl.program_id(1)

    @pl.when(e == 0)
    def _():
        acc_ref[...] = x1_ref[...]

    lane = lax.broadcasted_iota(I32, (1, LANE), 1)
    cw = jnp.sum(jnp.where(lane == e, comb_ref[...], 0.0), axis=-1, keepdims=True)
    xn = xn_ref[...]
    h = _silu(_mm(xn, wg_ref[0], precise)) * _mm(xn, wu_ref[0], precise)
    acc_ref[...] += cw * _mm(h, wd_ref[0], precise)

    @pl.when(e == MOE_EXPERTS - 1)
    def _():
        o_ref[...] = acc_ref[...]


def _moe_call(xn2, x1, comb, wg, wu, wd, tm, precise=False):
    n = xn2.shape[0]
    return pl.pallas_call(
        functools.partial(_moe_kernel, precise=precise),
        grid=(n // tm, MOE_EXPERTS),
        in_specs=[pl.BlockSpec((tm, D_MODEL), lambda i, e: (i, 0)),
                  pl.BlockSpec((tm, D_MODEL), lambda i, e: (i, 0)),
                  pl.BlockSpec((tm, LANE), lambda i, e: (i, 0)),
                  pl.BlockSpec((1, D_MODEL, MOE_HIDDEN), lambda i, e: (e, 0, 0)),
                  pl.BlockSpec((1, D_MODEL, MOE_HIDDEN), lambda i, e: (e, 0, 0)),
                  pl.BlockSpec((1, MOE_HIDDEN, D_MODEL), lambda i, e: (e, 0, 0))],
        out_specs=pl.BlockSpec((tm, D_MODEL), lambda i, e: (i, 0)),
        out_shape=jax.ShapeDtypeStruct((n, D_MODEL), F32),
        scratch_shapes=[pltpu.VMEM((tm, D_MODEL), F32)],
        compiler_params=_cparams(("parallel", "arbitrary")),
        name="moe_experts",
    )(xn2, x1, comb, wg, wu, wd)


def _layer_weights(prm):
    wp, wgate = _pack_w_in(prm['w_in'])
    wr = jnp.concatenate([prm['w_router_exp'], prm['w_router_grp'],
                          jnp.zeros((D_MODEL, LANE - MOE_EXPERTS - MOE_GROUPS), F32)], axis=1)
    wr_hi = wr.astype(BF16)
    wr_lo = (wr - wr_hi.astype(F32)).astype(BF16)
    br = jnp.concatenate([prm['b_router_exp'], prm['b_router_grp'],
                          jnp.zeros((LANE - MOE_EXPERTS - MOE_GROUPS,), F32)])[None, :]
    pad = lambda v: jnp.concatenate([v.astype(F32), jnp.zeros((LANE - v.shape[0],), F32)])
    hp = jnp.stack([pad(prm['ssd_a_log']), pad(prm['ssd_d'])] + [jnp.zeros((LANE,), F32)] * 6)
    w8 = jnp.concatenate([prm['ssd_conv_w'], jnp.zeros((SUB - SSD_CONV, SSD_CONV_DIM), F32)], axis=0)
    wbr = jnp.concatenate([prm['w_br_nsa'], prm['w_br_moba'], prm['w_br_fox'], prm['w_br_ssd']], axis=0)
    shared = dict(gmix=prm['norm_mix'][None, :], gffn=prm['norm_ffn'][None, :], gains=_proj_gains(prm),
                  biases=_proj_biases(prm), br=br, hp=hp, w8=w8, convw=prm['ssd_conv_w'],
                  convb=prm['ssd_conv_b'][None, :], norm=prm['ssd_norm'][None, :])
    fast = dict(shared, wp=wp.astype(BF16), wgate=wgate.astype(BF16), cmp=_cmp_weights(prm, BF16),
                wbr=wbr.astype(BF16), wout=prm['w_out'].astype(BF16), wr=jnp.stack([wr_hi, wr_lo]),
                weg=prm['w_exp_gate'].astype(BF16), weu=prm['w_exp_up'].astype(BF16),
                wed=prm['w_exp_down'].astype(BF16))
    exact = dict(shared, wp=wp, wgate=wgate, cmp=_cmp_weights(prm, F32), wbr=wbr, wout=prm['w_out'],
                 wr=wr[None], weg=prm['w_exp_gate'], weu=prm['w_exp_up'], wed=prm['w_exp_down'])
    return fast, exact


def _pad_rows(a, rows):
    return jnp.pad(a, ((0, 0), (0, rows - a.shape[1])) + ((0, 0),) * (a.ndim - 2))


def _heads_to_sublanes(logf, tk_pad):
    lt = jnp.swapaxes(logf, 1, 2)
    return jnp.pad(lt, ((0, 0), (0, SUB - lt.shape[1]), (0, tk_pad - lt.shape[2])))


def _layer_prompt(x, w, w_ffn, *, tm, tq, tk, precise):
    bsz, t, _ = x.shape
    n = bsz * t
    x2 = x.reshape(n, D_MODEL)
    tabs = _rope_tables(jnp.arange(t, dtype=I32))
    (qn, cmp_rows, slc_rows, win_rows, qm, moba_rows, qf, fox_rows, z, xbc, small, dt) = _proj_call(
        x2, w['gmix'], w['wp'], w['gains'], tabs, w['biases'], tm, precise)
    b3 = lambda a: a.reshape(bsz, t, a.shape[-1])
    qn, qm, qf = b3(qn), b3(qm), b3(qf)
    kvc = _cmp_mlp_call(cmp_rows.reshape(bsz, t // NSA_CMP_STRIDE, NSA_CMP_STRIDE * LANE), *w['cmp'],
                        precise=precise)
    n_cmp = (t - NSA_CMP_LEN) // NSA_CMP_STRIDE + 1
    n_sel = -(-t // NSA_SEL_BLOCK)
    nselp = -(-n_sel // LANE) * LANE
    o_cmp, sel, _ = _nsa_cmp_call(qn, kvc, tq=tq, q_pos0=0, n_cmp=n_cmp, n_sel=n_sel, nselp=nselp,
                                  precise=precise)
    o_slc = _attn_call(qn, b3(slc_rows), tq=tq, tk=tk, bm=sel, bm_bs=NSA_SEL_BLOCK, precise=precise,
                       name="nsa_slc")
    tw = min(tq, tk, NSA_WINDOW // 2)
    o_win = _attn_call(qn, b3(win_rows), tq=tw, tk=tw, window=NSA_WINDOW, precise=precise,
                       name="nsa_win")
    moba3 = b3(moba_rows)
    kmean = _pad_rows(_kmean_call(moba3), LANE)
    n_blk = -(-t // MOBA_BLOCK)
    bm = _moba_gate_call(qm, kmean, tq=tq, k_top=min(MOBA_TOPK, n_blk - 1), precise=precise)
    o_moba = _attn_call(qm, moba3, tq=tq, tk=tk, bm=bm, bm_bs=MOBA_BLOCK, bm_hs=LANE, precise=precise,
                        name="moba")
    logf = small[:, SM_FOXF:SM_FOXF + N_HEADS].reshape(bsz, t, N_HEADS)
    cum_t = _cum_call(_heads_to_sublanes(logf, t))
    cq = jnp.pad(jnp.swapaxes(cum_t, 1, 2), ((0, 0), (0, 0), (0, LANE - SUB)))
    o_fox = _attn_call(qf, b3(fox_rows), tq=tq, tk=tk, cq=cq, cum_t=cum_t, precise=precise, name="fox")
    y_ssd, ssd_state = _ssd_call(b3(xbc), b3(z), b3(dt), w['convw'], w['convb'], w['hp'], w['norm'],
                                 precise)
    f2 = lambda a: a.reshape(n, a.shape[-1])
    x1, xn2, comb = _post_call(x2, w['gmix'], w['wgate'], small, f2(o_cmp), f2(o_slc), f2(o_win),
                               f2(o_moba), f2(o_fox), f2(y_ssd), w['wbr'], w['wout'], w['gffn'],
                               w['wr'], w['br'], tm, precise)
    y = _moe_call(xn2, x1, comb, w_ffn['weg'], w_ffn['weu'], w_ffn['wed'], min(n, 4 * tm))
    wb = min(NSA_WINDOW, t)
    state = (cmp_rows.reshape(bsz, t, 2, HEAD_DIM), slc_rows.reshape(bsz, t, 2, HEAD_DIM),
             win_rows.reshape(bsz, t, 2, HEAD_DIM)[:, t - wb:],
             moba_rows.reshape(bsz, t, 2, N_HEADS, HEAD_DIM), fox_rows.reshape(bsz, t, 2, N_HEADS, HEAD_DIM),
             logf, ssd_state, b3(xbc)[:, t - (SSD_CONV - 1):])
    return y.reshape(bsz, t, D_MODEL), state


def _cache_views(cache_nsa_cmp, cache_nsa_slc, state_nsa_win, cache_moba, cache_fox, cache_fox_logf):
    rows_last = lambda c: jnp.moveaxis(c, 2, -1)
    return (rows_last(cache_nsa_cmp), rows_last(cache_nsa_slc)[:, :, :, None],
            rows_last(state_nsa_win)[:, :, :, None], rows_last(cache_moba), rows_last(cache_fox),
            rows_last(cache_fox_logf))


def _layer_sample(x, w, views, layer, s_win, s_ssd, s_conv, page_table, *, g_pages):
    cmp_t, slc_t, win_t, moba_t, fox_t, logf_t = views
    bsz = x.shape[0]
    n_pages = page_table.shape[1]
    pos0 = n_pages * PAGE
    x2 = x.reshape(bsz, D_MODEL)
    tabs = _rope_tables(jnp.full((bsz,), pos0, I32))
    (qn, cmp_rows, slc_rows, win_rows, qm, moba_rows, qf, fox_rows, z, xbc, small, dt) = _proj_call(
        x2, w['gmix'], w['wp'], w['gains'], tabs, w['biases'], bsz, precise=True)
    q8 = lambda a: _pad_rows(a[:, None, :], SUB)
    qn, qm, qf = q8(qn), q8(qm), q8(qf)
    every_head = lambda n: jnp.full((bsz, n), -1, I32)
    assert (pos0 + 1 - NSA_CMP_LEN) // NSA_CMP_STRIDE + 1 == pos0 // NSA_CMP_STRIDE - 1
    kvc = _kt_cmp_call(cmp_t, layer, page_table, *w['cmp'], g=g_pages)
    n_cmp = pos0 // NSA_CMP_STRIDE - 1
    n_sel = -(-(pos0 + 1) // NSA_SEL_BLOCK)
    nselp = -(-n_sel // LANE) * LANE
    o_cmp, _, sel_idx = _nsa_cmp_call(qn, kvc, tq=SUB, q_pos0=pos0, n_cmp=n_cmp, n_sel=n_sel,
                                      nselp=nselp, precise=True)
    k_top = min(NSA_SEL_TOPK, n_sel)
    sel_ids = sel_idx[:, 0, :k_top]
    sel_heads = jnp.where(sel_ids < pos0 // NSA_SEL_BLOCK, -1, N_HEADS)
    o_slc = _kt_attn_call(qn, slc_t, layer, page_table, sel_ids, sel_heads, q8(slc_rows),
                          r=NSA_SEL_BLOCK, g=k_top, name="nsa_slc_s")
    wb = s_win.shape[1]
    o_win = _kt_attn_call(qn, win_t, layer, jnp.arange(bsz, dtype=I32)[:, None], jnp.zeros((bsz, 1), I32),
                          every_head(1), q8(win_rows), r=wb, g=1, lo=wb - NSA_WINDOW + 1, name="nsa_win_s")
    n_blk = -(-(pos0 + 1) // MOBA_BLOCK)
    m_top = min(MOBA_TOPK, n_blk - 1)
    blk_ids = _kt_gate_call(qm, moba_t, layer, page_table, g=g_pages, k_top=m_top)[:, :N_HEADS, :m_top]
    ppb = MOBA_BLOCK // PAGE
    moba_fetch = (blk_ids[..., None] * ppb + jnp.arange(ppb, dtype=I32)).reshape(bsz, -1)
    head_id = jnp.arange(N_HEADS, dtype=I32)[None, :, None, None]
    moba_heads = jnp.broadcast_to(jnp.where(blk_ids[..., None] >= 0, head_id, N_HEADS),
                                  blk_ids.shape + (ppb,)).reshape(bsz, -1)
    o_moba = _kt_attn_call(qm, moba_t, layer, page_table, moba_fetch, moba_heads, q8(moba_rows), r=PAGE,
                           g=m_top * ppb, one_head=True, name="moba_s")
    logf_new = small[:, SM_FOXF:SM_FOXF + N_HEADS]
    fox_fetch = jnp.tile(jnp.arange(n_pages - 1, -1, -1, dtype=I32)[None, :], (bsz, 1))
    o_fox = _kt_attn_call(qf, fox_t, layer, page_table, fox_fetch, every_head(n_pages), q8(fox_rows),
                          r=PAGE, g=g_pages, logf_t=logf_t, logf_new=q8(small), name="fox_s")
    full = jnp.concatenate([s_conv, xbc[:, None, :]], axis=1)
    y_ssd, ssd_state = _ssd_step_call(_pad_rows(full, SUB), z[:, None, :], dt[:, None, :], s_ssd,
                                      w['w8'], w['convb'], w['hp'], w['norm'])
    first = lambda a: a[:, 0, :]
    x1, xn2, comb = _post_call(x2, w['gmix'], w['wgate'], small, first(o_cmp), first(o_slc), first(o_win),
                               first(o_moba), first(o_fox), first(y_ssd), w['wbr'], w['wout'],
                               w['gffn'], w['wr'], w['br'], bsz, precise=True)
    y = _moe_call(xn2, x1, comb, w['weg'], w['weu'], w['wed'], bsz, precise=True)
    state = (cmp_rows.reshape(bsz, 1, 2, HEAD_DIM), slc_rows.reshape(bsz, 1, 2, HEAD_DIM),
             jnp.concatenate([s_win[:, 1:], win_rows.reshape(bsz, 1, 2, HEAD_DIM)], axis=1),
             moba_rows.reshape(bsz, 1, 2, N_HEADS, HEAD_DIM), fox_rows.reshape(bsz, 1, 2, N_HEADS, HEAD_DIM),
             logf_new[:, None, :], ssd_state, full[:, 1:])
    return y.reshape(bsz, 1, D_MODEL), state


_PARAM_NAMES = ('norm_mix', 'norm_ffn', 'w_in', 'g_nsa_q', 'g_nsa_k', 'g_nsa_kc', 'pe_cmp_k', 'pe_cmp_v',
                'w_cmp_k1', 'w_cmp_k2', 'w_cmp_v1', 'w_cmp_v2', 'g_moba_q', 'g_moba_k', 'g_fox_q',
                'g_fox_k', 'b_fox_f', 'ssd_conv_w', 'ssd_conv_b', 'ssd_dt_bias', 'ssd_a_log', 'ssd_d',
                'ssd_norm', 'w_br_nsa', 'w_br_moba', 'w_br_fox', 'w_br_ssd', 'w_out', 'w_router_grp',
                'b_router_grp', 'w_router_exp', 'b_router_exp', 'w_exp_gate', 'w_exp_up', 'w_exp_down')


def kernel(x_prompt, x_sample, cache_nsa_cmp, cache_nsa_slc, state_nsa_win, cache_moba, cache_fox, cache_fox_logf, state_ssd, state_ssd_conv, page_table, norm_mix, norm_ffn, w_in, g_nsa_q, g_nsa_k, g_nsa_kc, pe_cmp_k, pe_cmp_v, w_cmp_k1, w_cmp_k2, w_cmp_v1, w_cmp_v2, g_moba_q, g_moba_k, g_fox_q, g_fox_k, b_fox_f, ssd_conv_w, ssd_conv_b, ssd_dt_bias, ssd_a_log, ssd_d, ssd_norm, w_br_nsa, w_br_moba, w_br_fox, w_br_ssd, w_out, w_router_grp, b_router_grp, w_router_exp, b_router_exp, w_exp_gate, w_exp_up, w_exp_down):
    params = dict(zip(_PARAM_NAMES, (
        norm_mix, norm_ffn, w_in, g_nsa_q, g_nsa_k, g_nsa_kc, pe_cmp_k, pe_cmp_v, w_cmp_k1, w_cmp_k2,
        w_cmp_v1, w_cmp_v2, g_moba_q, g_moba_k, g_fox_q, g_fox_k, b_fox_f, ssd_conv_w, ssd_conv_b,
        ssd_dt_bias, ssd_a_log, ssd_d, ssd_norm, w_br_nsa, w_br_moba, w_br_fox, w_br_ssd, w_out,
        w_router_grp, b_router_grp, w_router_exp, b_router_exp, w_exp_gate, w_exp_up, w_exp_down)))
    depth = norm_mix.shape[0]
    views = _cache_views(cache_nsa_cmp, cache_nsa_slc, state_nsa_win, cache_moba, cache_fox, cache_fox_logf)
    xp, xs = x_prompt, x_sample
    sp, ss = [], []
    for l in range(depth):
        w_fast, w_exact = _layer_weights({k: v[l] for k, v in params.items()})
        feeds_next = l < depth - 1
        xp, st_p = _layer_prompt(xp, w_exact if feeds_next else w_fast, w_fast, tm=256, tq=512, tk=512,
                                 precise=feeds_next)
        xs, st_s = _layer_sample(xs, w_exact, views, l, state_nsa_win[l], state_ssd[l], state_ssd_conv[l],
                                 page_table, g_pages=16)
        sp.append(st_p)
        ss.append(st_s)
    outs = [xp, xs]
    for i in range(8):
        outs.append(jnp.stack([s[i] for s in sp], axis=0))
        outs.append(jnp.stack([s[i] for s in ss], axis=0))
    return tuple(outs)
```

```python
import functools
import math

import jax
import jax.numpy as jnp
import numpy as np
from jax import lax
from jax.experimental import pallas as pl
from jax.experimental.pallas import tpu as pltpu

F32 = jnp.float32
BF16 = jnp.bfloat16
I32 = jnp.int32

D_MODEL = 1024
HEAD_DIM = 64
N_HEADS = 4
HEADS_W = N_HEADS * HEAD_DIM
ROPE_DIM = HEAD_DIM // 4
ROPE_HALF = ROPE_DIM // 2
ROPE_THETA = 500000.0
SCALE = HEAD_DIM ** -0.5
EPS = 1e-6
NEG = -1e30
PAGE = 128

NSA_CMP_LEN = 32
NSA_CMP_STRIDE = 16
NSA_CMP_HIDDEN = 2 * HEAD_DIM
NSA_SEL_BLOCK = 64
NSA_SEL_TOPK = 16
NSA_WINDOW = 512
MOBA_BLOCK = 256
MOBA_TOPK = 3

SSD_HEADS = 8
SSD_HEAD_DIM = 64
SSD_INNER = SSD_HEADS * SSD_HEAD_DIM
SSD_STATE = 64
SSD_GROUPS = 2
SSD_CONV = 4
SSD_CHUNK = 128
SSD_BC = SSD_GROUPS * SSD_STATE
SSD_CONV_DIM = SSD_INNER + 2 * SSD_BC

MOE_GROUPS = 4
MOE_PER_GROUP = 4
MOE_EXPERTS = 16
MOE_HIDDEN = 512

LANE = 128
SUB = 8
VMEM_LIMIT = 56 * 1024 * 1024

_IN_SIZES = (HEADS_W, 6 * HEAD_DIM, 3 * N_HEADS, 3 * HEADS_W, 3 * HEADS_W, N_HEADS,
             2 * SSD_INNER + 2 * SSD_BC + SSD_HEADS, 4 * D_MODEL)
_IN_OFFS = tuple(int(sum(_IN_SIZES[:i])) for i in range(len(_IN_SIZES) + 1))

P_NSA_Q = 0
P_NSA_KV = 256
P_MOBA = 640
P_FOX = 1408
P_SSD = 2176
P_SMALL = 3456
P_DT = 3584
P_WIDTH = 3712
SM_FOXF = 0
SM_NSAG = 4


def _cparams(sem):
    return pltpu.CompilerParams(dimension_semantics=sem, vmem_limit_bytes=VMEM_LIMIT)


def _sigmoid(x):
    return 1.0 / (1.0 + jnp.exp(-x))


def _silu(x):
    return x * _sigmoid(x)


def _softplus(x):
    return jnp.maximum(x, 0.0) + jnp.log(1.0 + jnp.exp(-jnp.abs(x)))


def _dot(a, b):
    return jnp.dot(a, b, preferred_element_type=F32)


def _dot_nt(a, b):
    return lax.dot_general(a, b, (((1,), (1,)), ((), ())), preferred_element_type=F32)


def _split2(x):
    hi = x.astype(BF16)
    lo = (x - hi.astype(F32)).astype(BF16)
    return hi, lo


def _split3(x):
    h1 = x.astype(BF16)
    r = x - h1.astype(F32)
    h2 = r.astype(BF16)
    h3 = (r - h2.astype(F32)).astype(BF16)
    return h1, h2, h3


def _mm(a, b, precise):
    if precise:
        ah, al = _split2(a)
        bh, bl = _split2(b)
        return _dot(ah, bh) + _dot(ah, bl) + _dot(al, bh)
    return _dot(a.astype(BF16), b.astype(BF16))


def _wmm(a, w_ref, idx, precise):
    w_hi = w_ref[(0,) + idx]
    if precise:
        ah, al = _split2(a)
        return _dot(ah, w_hi) + _dot(ah, w_ref[(1,) + idx]) + _dot(al, w_hi)
    return _dot(a.astype(BF16), w_hi)


def _split_w(w, precise):
    if not precise:
        return w.astype(BF16)[None]
    bits = lax.bitcast_convert_type(w, jnp.uint32)
    bits = (bits + jnp.uint32(0x7FFF) + ((bits >> 16) & jnp.uint32(1))) & jnp.uint32(0xFFFF0000)
    hi = lax.bitcast_convert_type(bits, F32)
    return jnp.stack([hi.astype(BF16), (w - hi).astype(BF16)])


def _by_tile(mode, is_tail, body):
    if mode == 'tail':
        pl.when(is_tail)(lambda: body(True))
        pl.when(jnp.logical_not(is_tail))(lambda: body(False))
    else:
        body(mode == 'precise')


def _mm_nt(a, b, precise):
    if precise:
        ah, al = _split2(a)
        bh, bl = _split2(b)
        return _dot_nt(ah, bh) + _dot_nt(ah, bl) + _dot_nt(al, bh)
    return _dot_nt(a.astype(BF16), b.astype(BF16))


def _dot_exact(x, e):
    h1, h2, h3 = _split3(x)
    return _dot(h1, e) + _dot(h2, e) + _dot(h3, e)


def _head_sumsq(y, precise=False):
    r = lax.broadcasted_iota(I32, (LANE, LANE), 0) // HEAD_DIM
    c = lax.broadcasted_iota(I32, (LANE, LANE), 1) // HEAD_DIM
    e = jnp.where(r == c, 1.0, 0.0).astype(BF16)
    if precise:
        return _dot_exact(y * y, e)
    hi, lo = _split2(y * y)
    return _dot(hi, e) + _dot(lo, e)


def _rope_chunk(y, cos_t, sin_a, sin_b):
    return (y * cos_t + pltpu.roll(y, LANE - ROPE_HALF, 1) * sin_a
            + pltpu.roll(y, ROPE_HALF, 1) * sin_b)


def _proj_kernel(*refs, mode, nt):
    is_tail = (pl.program_id(0) % nt) == nt - 1
    _by_tile(mode, is_tail,
             lambda precise: _proj_body(*refs, precise=precise, keys_precise=mode != 'fast'))


def _proj_body(x_ref, gmix_ref, w_ref, gains_ref, tab_ref, bias_ref,
               qn_ref, cmp_ref, slc_ref, win_ref, qm_ref, moba_ref, qf_ref, fox_ref,
               z_ref, xbc_ref, small_ref, dt_ref, *, precise, keys_precise):
    x = x_ref[...]
    ms = jnp.mean(x * x, axis=-1, keepdims=True)
    xn = x * lax.rsqrt(ms + EPS) * gmix_ref[...]
    xn_fast = xn.astype(BF16)
    cos_t = tab_ref[0]
    sin_a = tab_ref[1]
    sin_b = tab_ref[2]
    lane = lax.broadcasted_iota(I32, (1, LANE), 1)
    first = lane < HEAD_DIM

    def proj(off, width, shared=False):
        prec = precise or (shared and keys_precise)
        return _wmm(xn if prec else xn_fast, w_ref, (slice(None), slice(off, off + width)), prec)

    def normed(y, gain_row, k_only=False, shared=False):
        g = gains_ref[gain_row:gain_row + 1, :]
        prec = precise or (shared and keys_precise)
        yn = y * lax.rsqrt(_head_sumsq(y, prec) * (1.0 / HEAD_DIM) + EPS) * g
        return jnp.where(first, yn, y) if k_only else yn

    p = proj(P_NSA_Q, HEADS_W)
    for c in range(2):
        y = _rope_chunk(normed(p[:, c * LANE:(c + 1) * LANE], 0), cos_t, sin_a, sin_b)
        qn_ref[:, c * LANE:(c + 1) * LANE] = (y * SCALE).astype(qn_ref.dtype)
    p = proj(P_NSA_KV, 3 * LANE, shared=True)
    cos_k = jnp.where(first, cos_t, 1.0)
    sin_ak = jnp.where(first, sin_a, 0.0)
    sin_bk = jnp.where(first, sin_b, 0.0)
    for c, ref in enumerate((cmp_ref, slc_ref, win_ref)):
        y = normed(p[:, c * LANE:(c + 1) * LANE], 1 + c, k_only=True, shared=True)
        ref[...] = _rope_chunk(y, cos_k, sin_ak, sin_bk)
    p = proj(P_MOBA, HEADS_W)
    pkv = proj(P_MOBA + HEADS_W, 2 * HEADS_W, shared=True)
    for c in range(2):
        y = _rope_chunk(normed(p[:, c * LANE:(c + 1) * LANE], 4), cos_t, sin_a, sin_b)
        qm_ref[:, c * LANE:(c + 1) * LANE] = (y * SCALE).astype(qn_ref.dtype)
        y = _rope_chunk(normed(pkv[:, c * LANE:(c + 1) * LANE], 5, shared=True), cos_t, sin_a, sin_b)
        moba_ref[:, c * LANE:(c + 1) * LANE] = y
    moba_ref[:, HEADS_W:] = pkv[:, HEADS_W:]
    p = proj(P_FOX, HEADS_W)
    pkv = proj(P_FOX + HEADS_W, 2 * HEADS_W, shared=True)
    for c in range(2):
        y = normed(p[:, c * LANE:(c + 1) * LANE], 6)
        qf_ref[:, c * LANE:(c + 1) * LANE] = (y * SCALE).astype(qn_ref.dtype)
        fox_ref[:, c * LANE:(c + 1) * LANE] = normed(pkv[:, c * LANE:(c + 1) * LANE], 7, shared=True)
    fox_ref[:, HEADS_W:] = pkv[:, HEADS_W:]
    z_ref[...] = proj(P_SSD, SSD_INNER)
    xbc_ref[...] = proj(P_SSD + SSD_INNER, SSD_CONV_DIM, shared=True)
    p = proj(P_SMALL, LANE, shared=True)
    logf = -_softplus(-(p + bias_ref[0:1, :]))
    small_ref[...] = jnp.where(lane < SM_NSAG, logf, p)
    p = proj(P_DT, LANE, shared=True)
    dt_ref[...] = _softplus(p + bias_ref[1:2, :])


def _proj_call(x2, gmix, wp, gains, tabs, biases, tm, mode):
    n = x2.shape[0]
    nt = tabs.shape[1] // tm
    row = lambda w: pl.BlockSpec((tm, w), lambda i: (i, 0))
    full = lambda a: pl.BlockSpec(a.shape, lambda i: (0,) * a.ndim)
    widths = (HEADS_W, LANE, LANE, LANE, HEADS_W, 2 * HEADS_W, HEADS_W, 2 * HEADS_W,
              SSD_INNER, SSD_CONV_DIM, LANE, LANE)
    qd = BF16 if mode == 'fast' else F32
    dtypes = (qd, F32, F32, F32, qd, F32, qd, F32, F32, F32, F32, F32)
    return pl.pallas_call(
        functools.partial(_proj_kernel, mode=mode, nt=nt),
        grid=(n // tm,),
        in_specs=[row(D_MODEL), full(gmix), full(wp), full(gains),
                  pl.BlockSpec((3, tm, LANE), lambda i: (0, i % nt, 0)), full(biases)],
        out_specs=[row(w) for w in widths],
        out_shape=[jax.ShapeDtypeStruct((n, w), d) for w, d in zip(widths, dtypes)],
        compiler_params=_cparams(("parallel",)),
        name="proj",
    )(x2, gmix, wp, gains, tabs, biases)


def _rope_tables(pos):
    inv = jnp.power(ROPE_THETA, -jnp.arange(ROPE_HALF, dtype=F32) / ROPE_HALF)
    ang = pos.astype(F32)[:, None] * inv[None, :]
    cos, sin = jnp.cos(ang), jnp.sin(ang)
    t = pos.shape[0]
    one = jnp.ones((t, HEAD_DIM - ROPE_DIM), F32)
    zero = jnp.zeros((t, HEAD_DIM - ROPE_DIM), F32)
    zh = jnp.zeros((t, ROPE_HALF), F32)
    c = jnp.concatenate([cos, cos, one], axis=1)
    a = jnp.concatenate([-sin, zh, zero], axis=1)
    b = jnp.concatenate([zh, sin, zero], axis=1)
    return jnp.stack([jnp.tile(c, (1, 2)), jnp.tile(a, (1, 2)), jnp.tile(b, (1, 2))], axis=0)


def _pack_w_in(w_in):
    o = _IN_OFFS
    ssd = w_in[:, o[6]:o[7]]
    small = jnp.concatenate([w_in[:, o[5]:o[6]], w_in[:, o[2]:o[3]],
                             jnp.zeros((D_MODEL, LANE - 4 * N_HEADS), F32)], axis=1)
    dt = jnp.concatenate([ssd[:, 2 * SSD_INNER + 2 * SSD_BC:],
                          jnp.zeros((D_MODEL, LANE - SSD_HEADS), F32)], axis=1)
    wp = jnp.concatenate([w_in[:, o[0]:o[2]], w_in[:, o[3]:o[5]],
                          ssd[:, :2 * SSD_INNER + 2 * SSD_BC], small, dt], axis=1)
    return wp, w_in[:, o[7]:o[8]]


def _proj_gains(prm):
    two = lambda g: jnp.tile(g, 2)
    ones = jnp.ones((HEAD_DIM,), F32)
    rows = [two(prm['g_nsa_q'])]
    rows += [jnp.concatenate([prm['g_nsa_k'][i], ones]) for i in range(3)]
    rows += [two(prm['g_moba_q']), two(prm['g_moba_k']), two(prm['g_fox_q']), two(prm['g_fox_k'])]
    return jnp.stack(rows, axis=0)


def _proj_biases(prm):
    pad = lambda v: jnp.concatenate([v.astype(F32), jnp.zeros((LANE - v.shape[0],), F32)])
    rows = [pad(prm['b_fox_f']), pad(prm['ssd_dt_bias'])] + [jnp.zeros((LANE,), F32)] * 6
    return jnp.stack(rows, axis=0)


def _cmp_mlp_kernel(r_ref, wab_ref, pe_ref, w2_ref, g_ref, out_ref, *, precise):
    n16 = r_ref.shape[1]
    hw = 2 * NSA_CMP_HIDDEN
    h = _mm(r_ref[0], wab_ref[...], precise)
    c = _mm(pe_ref[...], wab_ref[...], precise)
    c = c[0:1, :hw] + c[1:2, hw:]
    pre = h[:, :hw] + pltpu.roll(h[:, hw:], n16 - 1, 0) + c
    act = 0.5 * pre * (1.0 + jnp.tanh(math.sqrt(2.0 / math.pi) * (pre + 0.044715 * pre * pre * pre)))
    y = _mm(act, w2_ref[...], precise)
    lane = lax.broadcasted_iota(I32, (1, LANE), 1)
    yn = y * lax.rsqrt(_head_sumsq(y, precise) * (1.0 / HEAD_DIM) + EPS) * g_ref[...]
    out_ref[0] = jnp.where(lane < HEAD_DIM, yn, y)


def _cmp_mlp_call(rows16, wab, pe2, w2, gkc, precise=False):
    bsz, n16, w = rows16.shape
    full = lambda a: pl.BlockSpec(a.shape, lambda b: (0,) * a.ndim)
    return pl.pallas_call(
        functools.partial(_cmp_mlp_kernel, precise=precise),
        grid=(bsz,),
        in_specs=[pl.BlockSpec((1, n16, w), lambda b: (b, 0, 0)), full(wab), full(pe2), full(w2),
                  full(gkc)],
        out_specs=pl.BlockSpec((1, n16, LANE), lambda b: (b, 0, 0)),
        out_shape=jax.ShapeDtypeStruct((bsz, n16, LANE), F32),
        compiler_params=_cparams(("parallel",)),
        name="cmp_mlp",
    )(rows16, wab, pe2, w2, gkc)


def _cmp_weights(prm, dtype):
    def expand(w1, slot):
        w = w1.reshape(2, NSA_CMP_STRIDE, HEAD_DIM, NSA_CMP_HIDDEN)
        z = jnp.zeros_like(w)
        pair = (w, z) if slot == 0 else (z, w)
        return jnp.concatenate(pair, axis=2).reshape(2, NSA_CMP_STRIDE * LANE, NSA_CMP_HIDDEN)
    wk, wv = expand(prm['w_cmp_k1'], 0), expand(prm['w_cmp_v1'], 1)
    wab = jnp.concatenate([wk[0], wv[0], wk[1], wv[1]], axis=1).astype(dtype)
    pe = jnp.concatenate([prm['pe_cmp_k'], prm['pe_cmp_v']], axis=1)
    pe = pe.reshape(2, NSA_CMP_STRIDE * LANE)
    pe2 = jnp.concatenate([pe, jnp.zeros((SUB - 2, NSA_CMP_STRIDE * LANE), F32)], axis=0)
    z = jnp.zeros((NSA_CMP_HIDDEN, HEAD_DIM), F32)
    w2 = jnp.concatenate([jnp.concatenate([prm['w_cmp_k2'], z], axis=1),
                          jnp.concatenate([z, prm['w_cmp_v2']], axis=1)], axis=0).astype(dtype)
    gkc = jnp.concatenate([prm['g_nsa_kc'], jnp.ones((HEAD_DIM,), F32)])[None, :]
    return wab, pe2, w2, gkc


def _first_argmax(score, lane):
    m = jnp.max(score, axis=-1, keepdims=True)
    cand = jnp.where(score == m, lane.astype(F32), float(score.shape[-1]))
    return m, jnp.min(cand, axis=-1, keepdims=True).astype(I32)


def _nsa_cmp_kernel(q_ref, kvc_ref, ocmp_ref, sel_ref, idx_ref, *, tq, q_pos0, n_cmp, n_sel, k_top,
                    precise):
    n16 = kvc_ref.shape[1]
    nselp = sel_ref.shape[2]
    qi = pl.program_id(1)
    qpos = q_pos0 + qi * tq + lax.broadcasted_iota(I32, (tq, 1), 0)
    q = q_ref[0]
    kvc = kvc_ref[0]
    kc = kvc[:, :HEAD_DIM] if precise else kvc[:, :HEAD_DIM].astype(BF16)
    vc = kvc[:, HEAD_DIM:] if precise else kvc[:, HEAD_DIM:].astype(BF16)
    n_idx = lax.broadcasted_iota(I32, (1, n16), 1)
    mask = jnp.logical_and(n_idx * NSA_CMP_STRIDE + (NSA_CMP_LEN - 1) <= qpos, n_idx < n_cmp)
    psum = jnp.zeros((tq, n16), F32)
    outs = []
    for h in range(N_HEADS):
        lg = jnp.where(mask, _mm_nt(q[:, h * HEAD_DIM:(h + 1) * HEAD_DIM], kc, precise), NEG)
        m = jnp.max(lg, axis=-1, keepdims=True)
        p = jnp.where(mask, jnp.exp(lg - m), 0.0)
        p = p / jnp.maximum(jnp.sum(p, axis=-1, keepdims=True), 1e-30)
        outs.append(_mm(p, vc, precise))
        psum = psum + p
    ocmp_ref[0] = jnp.concatenate(outs, axis=-1)
    n_col = lax.broadcasted_iota(I32, (n16, nselp), 0) * NSA_CMP_STRIDE
    s_row = lax.broadcasted_iota(I32, (n16, nselp), 1) * NSA_SEL_BLOCK
    cover = jnp.logical_and(n_col <= s_row + (NSA_SEL_BLOCK - 1), n_col + (NSA_CMP_LEN - 1) >= s_row)
    cover = jnp.where(cover, 1.0, 0.0).astype(BF16)
    imp = _dot_exact(psum, cover)
    blk = lax.broadcasted_iota(I32, (1, nselp), 1)
    cur = qpos // NSA_SEL_BLOCK
    forced = jnp.logical_or(blk == 0, jnp.logical_or(blk == cur, blk == cur - 1))
    score = jnp.where(blk > cur, -1e9, jnp.where(forced, 1e9, imp))
    score = jnp.where(blk < n_sel, score, -jnp.inf)
    sel = jnp.zeros((tq, nselp), F32)
    lane = lax.broadcasted_iota(I32, (1, LANE), 1)
    picks = jnp.full((tq, LANE), -1, I32)
    for i in range(k_top):
        _, idx = _first_argmax(score, blk)
        hit = blk == idx
        sel = jnp.where(hit, 1.0, sel)
        picks = jnp.where(lane == i, idx, picks)
        score = jnp.where(hit, -jnp.inf, score)
    sel_ref[0] = sel
    idx_ref[0] = picks


def _nsa_cmp_call(q, kvc, *, tq, q_pos0, n_cmp, n_sel, nselp, precise=False):
    bsz, t, _ = q.shape
    n16 = kvc.shape[1]
    kern = functools.partial(_nsa_cmp_kernel, tq=tq, q_pos0=q_pos0, n_cmp=n_cmp, n_sel=n_sel,
                             k_top=min(NSA_SEL_TOPK, n_sel), precise=precise)
    return pl.pallas_call(
        kern,
        grid=(bsz, t // tq),
        in_specs=[pl.BlockSpec((1, tq, HEADS_W), lambda b, i: (b, i, 0)),
                  pl.BlockSpec((1, n16, LANE), lambda b, i: (b, 0, 0))],
        out_specs=[pl.BlockSpec((1, tq, HEADS_W), lambda b, i: (b, i, 0)),
                   pl.BlockSpec((1, tq, nselp), lambda b, i: (b, i, 0)),
                   pl.BlockSpec((1, tq, LANE), lambda b, i: (b, i, 0))],
        out_shape=[jax.ShapeDtypeStruct((bsz, t, HEADS_W), F32),
                   jax.ShapeDtypeStruct((bsz, t, nselp), F32),
                   jax.ShapeDtypeStruct((bsz, t, LANE), I32)],
        compiler_params=_cparams(("parallel", "parallel")),
        name="nsa_cmp",
    )(q, kvc)


ATTN_CHUNK = 128


def _attn_pairs(nq, tq, tk, window):
    qi_l, ki_l, first, last = [], [], [], []
    for qi in range(nq):
        lo = max(qi * tq - (window - 1), 0) // tk if window else 0
        hi = (qi * tq + tq - 1) // tk
        for ki in range(lo, hi + 1):
            qi_l.append(qi)
            ki_l.append(ki)
            first.append(int(ki == lo))
            last.append(int(ki == hi))
    return tuple(np.asarray(a, np.int32) for a in (qi_l, ki_l, first, last))


def _mm_tn(a, b, precise):
    dims = (((0,), (0,)), ((), ()))
    dot = lambda x, y: lax.dot_general(x, y, dims, preferred_element_type=F32)
    if precise:
        ah, al = _split2(a)
        bh, bl = _split2(b)
        return dot(ah, bh) + dot(ah, bl) + dot(al, bh)
    return dot(a.astype(BF16), b.astype(BF16))


def _attn_kernel(qi_ref, ki_ref, first_ref, last_ref, *refs, cfg):
    tq, tk, kw = cfg['tq'], cfg['tk'], cfg['kw']
    refs = list(refs)
    q_ref, kv_ref = refs.pop(0), refs.pop(0)
    bm_ref = refs.pop(0) if cfg['bm_bs'] else None
    if cfg['bias']:
        cqt_ref, cumk_ref = refs.pop(0), refs.pop(0)
    o_ref, m_ref, l_ref, acc_ref, qt_ref = refs[:5]
    bmt_ref = refs[5] if bm_ref is not None else None
    step = pl.program_id(1)
    qi, ki = qi_ref[step], ki_ref[step]

    @pl.when(first_ref[step] == 1)
    def _():
        m_ref[...] = jnp.full(m_ref.shape, NEG, F32)
        l_ref[...] = jnp.zeros(l_ref.shape, F32)
        acc_ref[...] = jnp.zeros(acc_ref.shape, F32)
        qt = q_ref[0].astype(F32).T
        zero = jnp.zeros((HEAD_DIM, tq), F32)
        for h in range(N_HEADS):
            qh = qt[h * HEAD_DIM:(h + 1) * HEAD_DIM]
            upper = kw == HEAD_DIM or h % 2 == 0
            qpad = jnp.concatenate([qh, zero] if upper else [zero, qh], axis=0)
            hi = qpad.astype(BF16)
            qt_ref[0, h] = hi
            if cfg['mode'] != 'fast':
                qt_ref[1, h] = (qpad - hi.astype(F32)).astype(BF16)
        if bm_ref is not None:
            bmt_ref[...] = bm_ref[0].T.astype(BF16)

    def compute(precise):
        qpos = qi * tq + lax.broadcasted_iota(I32, (1, tq), 1)
        hs = cfg['bm_hs']
        n_chunks = tk // ATTN_CHUNK
        heads = range(N_HEADS)
        rows, logits = [], []
        for c in range(n_chunks):
            rc = kv_ref[0, c * ATTN_CHUNK:(c + 1) * ATTN_CHUNK, :]
            rows.append(rc)
            kpos = ki * tk + c * ATTN_CHUNK + lax.broadcasted_iota(I32, (ATTN_CHUNK, 1), 0)
            mask = kpos <= qpos
            if cfg['window']:
                mask = jnp.logical_and(mask, qpos - kpos < cfg['window'])
            neg = jnp.where(mask, 0.0, NEG)
            if bm_ref is not None:
                j = lax.broadcasted_iota(I32, (1, hs if hs else bmt_ref.shape[0]), 1)
                expand = jnp.where(j == kpos // cfg['bm_bs'], 1.0, 0.0).astype(BF16)
                if not hs:
                    neg = neg + (_dot(expand, bmt_ref[...]) - 1.0) * (-NEG)
            per_head = []
            for h in heads:
                pair = 0 if kw == HEAD_DIM else h // 2
                kp = rc[:, pair * LANE:(pair + 1) * LANE]
                if precise:
                    kh, kl = _split2(kp)
                    s = _dot(kh, qt_ref[0, h]) + _dot(kh, qt_ref[1, h]) + _dot(kl, qt_ref[0, h])
                else:
                    s = _dot(kp.astype(BF16), qt_ref[0, h])
                s = s + neg
                if cfg['bias']:
                    s = s + (cqt_ref[0][h:h + 1, :]
                             - cumk_ref[0, c * ATTN_CHUNK:(c + 1) * ATTN_CHUNK, h:h + 1])
                if bm_ref is not None and hs:
                    s = s + (_dot(expand, bmt_ref[h * hs:(h + 1) * hs, :]) - 1.0) * (-NEG)
                per_head.append(s)
            logits.append(per_head)
        m = [m_ref[h] for h in heads]
        l = [l_ref[h] for h in heads]
        acc = [acc_ref[h] for h in heads]
        for c in range(n_chunks):
            probs, alphas = [], []
            for h in heads:
                s = logits[c][h]
                m_new = jnp.maximum(m[h], jnp.max(s, axis=0, keepdims=True))
                p = jnp.exp(s - m_new[0:1, :])
                alpha = jnp.exp(m[h] - m_new)
                l[h] = alpha * l[h] + jnp.sum(p, axis=0, keepdims=True)
                m[h] = m_new
                probs.append(p)
                alphas.append(alpha)
            for h in heads:
                ko = h * HEAD_DIM if kw > HEAD_DIM else 0
                v = rows[c][:, kw + ko:kw + ko + HEAD_DIM]
                acc[h] = alphas[h][0:1, :] * acc[h] + _mm_tn(v, probs[h], precise)
        for h in heads:
            m_ref[h] = m[h]
            l_ref[h] = l[h]
            acc_ref[h] = acc[h]

    _by_tile(cfg['mode'], qi == cfg['nq'] - 1, compute)

    @pl.when(last_ref[step] == 1)
    def _():
        out_t = jnp.concatenate(
            [acc_ref[h] / jnp.maximum(l_ref[h][0:1, :], 1e-30) for h in range(N_HEADS)], axis=0)
        o_ref[0] = out_t.T


def _attn_call(q, kv, *, tq, tk, window=0, bm=None, bm_bs=0, bm_hs=0, cq=None, cum_t=None,
               mode='fast', name="attn"):
    bsz, t, _ = q.shape
    kw = kv.shape[-1] // 2
    assert t % tq == 0 and t % tk == 0 and tk % ATTN_CHUNK == 0
    tabs = _attn_pairs(t // tq, tq, tk, window)
    cfg = dict(tq=tq, tk=tk, kw=kw, bm_bs=bm_bs, bm_hs=bm_hs, bias=cq is not None, window=window,
               mode=mode, nq=t // tq)
    q_map = lambda b, s, qi, ki, fi, la: (b, qi[s], 0)
    k_map = lambda b, s, qi, ki, fi, la: (b, ki[s], 0)
    args = [q, kv]
    specs = [pl.BlockSpec((1, tq, HEADS_W), q_map), pl.BlockSpec((1, tk, 2 * kw), k_map)]
    scratch = [pltpu.VMEM((N_HEADS, SUB, tq), F32), pltpu.VMEM((N_HEADS, SUB, tq), F32),
               pltpu.VMEM((N_HEADS, HEAD_DIM, tq), F32),
               pltpu.VMEM((1 if mode == 'fast' else 2, N_HEADS, LANE, tq), BF16)]
    if bm is not None:
        args.append(bm)
        specs.append(pl.BlockSpec((1, tq, bm.shape[2]), q_map))
        scratch.append(pltpu.VMEM((bm.shape[2], tq), BF16))
    if cq is not None:
        args += [cum_t, cq]
        specs += [pl.BlockSpec((1, SUB, tq), lambda b, s, qi, ki, fi, la: (b, 0, qi[s])),
                  pl.BlockSpec((1, tk, LANE), k_map)]
    return pl.pallas_call(
        functools.partial(_attn_kernel, cfg=cfg),
        grid_spec=pltpu.PrefetchScalarGridSpec(
            num_scalar_prefetch=4, grid=(bsz, tabs[0].shape[0]), in_specs=specs,
            out_specs=pl.BlockSpec((1, tq, HEADS_W), q_map), scratch_shapes=scratch),
        out_shape=jax.ShapeDtypeStruct((bsz, t, HEADS_W), F32),
        compiler_params=_cparams(("parallel", "arbitrary")), name=name,
    )(*tabs, *args)


def _col_from_row(row, n):
    eye = lax.broadcasted_iota(I32, (n, n), 0) == lax.broadcasted_iota(I32, (n, n), 1)
    return jnp.sum(jnp.where(eye, jnp.broadcast_to(row, (n, n)), 0.0), axis=1, keepdims=True)


def _row_from_col(col, n):
    eye = lax.broadcasted_iota(I32, (n, n), 0) == lax.broadcasted_iota(I32, (n, n), 1)
    return jnp.sum(jnp.where(eye, jnp.broadcast_to(col, (n, n)), 0.0), axis=0, keepdims=True)


def _sublane_total(x):
    for s in (4, 2, 1):
        x = x + pltpu.roll(x, s, 0)
    return x


def _query_page_logits(qb, kt):
    prod = qb * kt
    return _sublane_total(prod.reshape(HEAD_DIM // SUB, SUB, prod.shape[1]).sum(axis=0))


def _kt_attn_kernel(pt_ref, ft_ref, hd_ref, q_ref, *refs, cfg):
    g, nk, hk, pw, r, lo = cfg['g'], cfg['nk'], cfg['hk'], cfg['pw'], cfg['r'], cfg['lo']
    bias, one_head = cfg['bias'], cfg['one_head']
    refs = list(refs)
    kv_refs = [refs.pop(0) for _ in range(g)]
    self_ref = refs.pop(0)
    if bias:
        lf_refs = [refs.pop(0) for _ in range(g)]
        lfnew_ref = refs.pop(0)
    o_ref, m_ref, l_ref, acc_ref, qb_ref = refs[:5]
    carry_ref = refs[5] if bias else None
    b, ki = pl.program_id(0), pl.program_id(1)
    sub = lax.broadcasted_iota(I32, (SUB, pw), 0)
    lane = lax.broadcasted_iota(I32, (SUB, pw), 1)
    kv_head = lambda h: h if hk == N_HEADS else 0

    @pl.when(ki == 0)
    def _():
        m_ref[...] = jnp.full(m_ref.shape, NEG, F32)
        l_ref[...] = jnp.zeros(l_ref.shape, F32)
        acc_ref[...] = jnp.zeros(acc_ref.shape, F32)
        for h in range(N_HEADS):
            qcol = _col_from_row(q_ref[0][0:1, h * HEAD_DIM:(h + 1) * HEAD_DIM], HEAD_DIM)
            qb_ref[h] = jnp.broadcast_to(qcol, (HEAD_DIM, pw))
        if bias:
            row8 = jnp.broadcast_to(lfnew_ref[0][0:1, :], (SUB, LANE))
            e8 = lax.broadcasted_iota(I32, (SUB, LANE), 0) == lax.broadcasted_iota(I32, (SUB, LANE), 1)
            col8 = jnp.sum(jnp.where(e8, row8, 0.0), axis=1, keepdims=True)
            carry_ref[...] = jnp.broadcast_to(col8, carry_ref.shape)

    if bias:
        carry = carry_ref[...]
        pad = jnp.zeros((SUB - N_HEADS, pw), F32)
        stack = jnp.concatenate([x for i in range(g) for x in (lf_refs[i][0, 0], pad)], axis=0)
        upper = (lax.broadcasted_iota(I32, (pw, pw), 0) <= lax.broadcasted_iota(I32, (pw, pw), 1))
        prefix = _dot_exact(stack, jnp.where(upper, 1.0, 0.0).astype(BF16))
    logits, masks = [], []
    for i in range(g):
        j = ki * g + i
        blk, hd = ft_ref[b, j], hd_ref[b, j]
        lane0 = (blk * r) % pw
        ok = jnp.logical_and(jnp.logical_and(lane >= lane0, lane < lane0 + r),
                             blk * r + (lane - lane0) >= lo)
        s = jnp.full((SUB, pw), NEG, F32)
        if one_head:
            hc = jnp.minimum(hd, N_HEADS - 1)
            s = jnp.where(sub == hc, _query_page_logits(qb_ref[hc], kv_refs[i][0, 0, 0, hc]), s)
            ok = jnp.logical_and(ok, jnp.logical_and(sub == hd, hd < N_HEADS))
        else:
            for h in range(N_HEADS):
                s = jnp.where(sub == h, _query_page_logits(qb_ref[h], kv_refs[i][0, 0, 0, kv_head(h)]), s)
            ok = jnp.logical_and(ok, jnp.logical_and(sub < N_HEADS, hd < N_HEADS))
        if bias:
            pre = prefix[i * SUB:(i + 1) * SUB]
            tot = jnp.broadcast_to(pre[:, pw - 1:pw], (SUB, pw))
            s = s + (carry + (tot - pre))
            carry = carry + tot
        logits.append(jnp.where(ok, s, NEG))
        masks.append(ok)
    if bias:
        carry_ref[...] = carry

    m_old = m_ref[...]
    m_new = m_old
    for s in logits:
        m_new = jnp.maximum(m_new, jnp.broadcast_to(jnp.max(s, axis=1, keepdims=True), (SUB, pw)))
    alpha = jnp.exp(m_old - m_new)
    l_new = alpha * l_ref[...]
    probs = []
    for s, ok in zip(logits, masks):
        p = jnp.where(ok, jnp.exp(s - m_new), 0.0)
        l_new = l_new + jnp.broadcast_to(jnp.sum(p, axis=1, keepdims=True), (SUB, pw))
        probs.append(p)
    m_ref[...] = m_new
    l_ref[...] = l_new
    for h in range(N_HEADS):
        acc = alpha[h:h + 1, :] * acc_ref[h]
        if not one_head:
            for i in range(g):
                acc = acc + probs[i][h:h + 1, :] * kv_refs[i][0, 0, 1, kv_head(h)]
        acc_ref[h] = acc
    if one_head:
        for i in range(g):
            hc = jnp.minimum(hd_ref[b, ki * g + i], N_HEADS - 1)
            prow = _sublane_total(jnp.where(sub == hc, probs[i], 0.0))[0:1, :]
            acc_ref[hc] = acc_ref[hc] + prow * kv_refs[i][0, 0, 1, hc]

    @pl.when(ki == nk - 1)
    def _():
        w = hk * HEAD_DIM
        own = self_ref[0]
        sub1 = lax.broadcasted_iota(I32, (SUB, 1), 0)
        s_own = jnp.full((SUB, 1), NEG, F32)
        for h in range(N_HEADS):
            kh = own[0:1, kv_head(h) * HEAD_DIM:(kv_head(h) + 1) * HEAD_DIM]
            qh = q_ref[0][0:1, h * HEAD_DIM:(h + 1) * HEAD_DIM]
            s_own = jnp.where(sub1 == h, jnp.sum(qh * kh, axis=1, keepdims=True), s_own)
        m_fin = m_ref[:, 0:1]
        m_tot = jnp.maximum(m_fin, s_own)
        a_fin = jnp.exp(m_fin - m_tot)
        p_own = jnp.exp(s_own - m_tot)
        l_tot = a_fin * l_ref[:, 0:1] + p_own
        outs = []
        for h in range(N_HEADS):
            vcol = _col_from_row(own[0:1, w + kv_head(h) * HEAD_DIM:w + (kv_head(h) + 1) * HEAD_DIM],
                                 HEAD_DIM)
            tot = a_fin[h:h + 1, :] * jnp.sum(acc_ref[h], axis=1, keepdims=True) + p_own[h:h + 1, :] * vcol
            outs.append(_row_from_col(tot / jnp.maximum(l_tot[h:h + 1, :], 1e-30), HEAD_DIM))
        o_ref[0] = jnp.broadcast_to(jnp.concatenate(outs, axis=1), (SUB, HEADS_W))


def _kt_attn_call(q8, cache_t, layer, page_table, fetch, heads, self8, *, r, g, lo=0, one_head=False,
                  logf_t=None, logf_new=None, name="kt_attn"):
    bsz = q8.shape[0]
    _, _, _, hk, _, pw = cache_t.shape
    nf = fetch.shape[1]
    assert nf % g == 0
    nk = nf // g
    n_pages = page_table.shape[1]
    bias = logf_t is not None
    cfg = dict(g=g, nk=nk, hk=hk, pw=pw, r=r, lo=lo, bias=bias, one_head=one_head)

    pages = jnp.take_along_axis(page_table, jnp.clip(fetch * r // pw, 0, n_pages - 1), axis=1)
    kv_map = lambda i: (lambda b, ki, pg, ft, hd: (layer, pg[b, ki * g + i], 0, 0, 0, 0))
    lf_map = lambda i: (lambda b, ki, pg, ft, hd: (layer, pg[b, ki * g + i], 0, 0))
    fixed = lambda shape: pl.BlockSpec(shape, lambda b, ki, pt, ft, hd: (b, 0, 0))
    args = [q8] + [cache_t] * g + [self8]
    specs = ([fixed((1, SUB, HEADS_W))]
             + [pl.BlockSpec((1, 1, 2, hk, HEAD_DIM, pw), kv_map(i)) for i in range(g)]
             + [fixed((1, SUB, self8.shape[2]))])
    scratch = [pltpu.VMEM((SUB, pw), F32), pltpu.VMEM((SUB, pw), F32),
               pltpu.VMEM((N_HEADS, HEAD_DIM, pw), F32), pltpu.VMEM((N_HEADS, HEAD_DIM, pw), F32)]
    if bias:
        args += [logf_t] * g + [logf_new]
        specs += [pl.BlockSpec((1, 1, N_HEADS, pw), lf_map(i)) for i in range(g)] + [fixed((1, SUB, LANE))]
        scratch.append(pltpu.VMEM((SUB, pw), F32))
    return pl.pallas_call(
        functools.partial(_kt_attn_kernel, cfg=cfg),
        grid_spec=pltpu.PrefetchScalarGridSpec(
            num_scalar_prefetch=3, grid=(bsz, nk), in_specs=specs,
            out_specs=fixed((1, SUB, HEADS_W)), scratch_shapes=scratch),
        out_shape=jax.ShapeDtypeStruct((bsz, SUB, HEADS_W), F32),
        compiler_params=_cparams(("parallel", "arbitrary")),
        name=name,
    )(pages, fetch, heads, *args)


def _kt_gate_kernel(pt_ref, q_ref, *refs, g, nk, k_top, n_blk):
    k_refs = refs[:g]
    idx_ref, qb_ref, gate_ref = refs[g:]
    ki = pl.program_id(1)
    sub = lax.broadcasted_iota(I32, (SUB, LANE), 0)
    lane = lax.broadcasted_iota(I32, (SUB, LANE), 1)
    ppb = MOBA_BLOCK // PAGE

    @pl.when(ki == 0)
    def _():
        gate_ref[...] = jnp.zeros(gate_ref.shape, F32)
        for h in range(N_HEADS):
            qcol = _col_from_row(q_ref[0][0:1, h * HEAD_DIM:(h + 1) * HEAD_DIM], HEAD_DIM)
            qb_ref[h] = jnp.broadcast_to(qcol, (HEAD_DIM, LANE))

    gate = gate_ref[...]
    for i in range(g):
        s = jnp.zeros((SUB, LANE), F32)
        for h in range(N_HEADS):
            s = jnp.where(sub == h, _query_page_logits(qb_ref[h], k_refs[i][0, 0, 0, h]), s)
        tot = jnp.sum(s, axis=1, keepdims=True) * (1.0 / MOBA_BLOCK)
        gate = gate + jnp.where(lane == (ki * g + i) // ppb, tot, 0.0)
    gate_ref[...] = gate

    @pl.when(ki == nk - 1)
    def _():
        score = jnp.where(lane < n_blk, gate, NEG)
        picks = jnp.full((SUB, LANE), -1, I32)
        for i in range(k_top):
            m, idx = _first_argmax(score, lane)
            hit = lane == idx
            picks = jnp.where(jnp.logical_and(lane == i, m > 0.5 * NEG), idx, picks)
            score = jnp.where(hit, -jnp.inf, score)
        idx_ref[0] = picks


def _kt_gate_call(q8, cache_t, layer, page_table, *, g, k_top):
    bsz = q8.shape[0]
    n_pages = page_table.shape[1]
    assert n_pages % g == 0 and n_pages * PAGE % MOBA_BLOCK == 0
    nk = n_pages // g
    n_blk = n_pages * PAGE // MOBA_BLOCK
    assert n_blk <= LANE
    k_map = lambda i: (lambda b, ki, pt: (layer, pt[b, ki * g + i], 0, 0, 0, 0))
    return pl.pallas_call(
        functools.partial(_kt_gate_kernel, g=g, nk=nk, k_top=k_top, n_blk=n_blk),
        grid_spec=pltpu.PrefetchScalarGridSpec(
            num_scalar_prefetch=1, grid=(bsz, nk),
            in_specs=[pl.BlockSpec((1, SUB, HEADS_W), lambda b, ki, pt: (b, 0, 0))]
            + [pl.BlockSpec((1, 1, 1, N_HEADS, HEAD_DIM, PAGE), k_map(i)) for i in range(g)],
            out_specs=pl.BlockSpec((1, SUB, LANE), lambda b, ki, pt: (b, 0, 0)),
            scratch_shapes=[pltpu.VMEM((N_HEADS, HEAD_DIM, LANE), F32), pltpu.VMEM((SUB, LANE), F32)]),
        out_shape=jax.ShapeDtypeStruct((bsz, SUB, LANE), I32),
        compiler_params=_cparams(("parallel", "arbitrary")),
        name="moba_gate_s",
    )(page_table, q8, *([cache_t] * g))


def _kt_cmp_kernel(pt_ref, *refs, g, nk):
    kv_refs = refs[:g]
    wc_ref, pe_ref, w2_ref, g_ref, out_ref, rows_ref = refs[g:]
    ki = pl.program_id(1)
    for i in range(g):
        off = pl.multiple_of((ki * g + i) * PAGE, PAGE)
        rows_ref[pl.ds(off, PAGE), :] = kv_refs[i][0, 0].reshape(2 * HEAD_DIM, PAGE).T

    @pl.when(ki == nk - 1)
    def _():
        n16 = rows_ref.shape[0] // NSA_CMP_STRIDE
        hw = 2 * NSA_CMP_HIDDEN
        h = jnp.zeros((n16, 2 * hw), F32)
        c = jnp.zeros((SUB, 2 * hw), F32)
        for j in range(NSA_CMP_STRIDE):
            h = h + _mm(rows_ref[pl.ds(j, n16, stride=NSA_CMP_STRIDE), :], wc_ref[j], True)
            c = c + _mm(pe_ref[j], wc_ref[j], True)
        c = c[0:1, :hw] + c[1:2, hw:]
        pre = h[:, :hw] + pltpu.roll(h[:, hw:], n16 - 1, 0) + c
        act = 0.5 * pre * (1.0 + jnp.tanh(math.sqrt(2.0 / math.pi) * (pre + 0.044715 * pre * pre * pre)))
        y = _mm(act, w2_ref[...], True)
        lane = lax.broadcasted_iota(I32, (1, LANE), 1)
        yn = y * lax.rsqrt(_head_sumsq(y, True) * (1.0 / HEAD_DIM) + EPS) * g_ref[...]
        out_ref[0] = jnp.where(lane < HEAD_DIM, yn, y)


def _kt_cmp_call(cache_t, layer, page_table, wab, pe2, w2, gkc, *, g):
    bsz, n_pages = page_table.shape
    assert n_pages % g == 0
    nk = n_pages // g
    n16 = n_pages * PAGE // NSA_CMP_STRIDE
    wc = wab.reshape(NSA_CMP_STRIDE, LANE, wab.shape[1])
    pe = jnp.swapaxes(pe2[:2].reshape(2, NSA_CMP_STRIDE, LANE), 0, 1)
    pe = jnp.pad(pe, ((0, 0), (0, SUB - 2), (0, 0)))
    full = lambda a: pl.BlockSpec(a.shape, lambda b, ki, pt: (0,) * a.ndim)
    kv_map = lambda i: (lambda b, ki, pt: (layer, pt[b, ki * g + i], 0, 0, 0))
    return pl.pallas_call(
        functools.partial(_kt_cmp_kernel, g=g, nk=nk),
        grid_spec=pltpu.PrefetchScalarGridSpec(
            num_scalar_prefetch=1, grid=(bsz, nk),
            in_specs=[pl.BlockSpec((1, 1, 2, HEAD_DIM, PAGE), kv_map(i)) for i in range(g)]
            + [full(wc), full(pe), full(w2), full(gkc)],
            out_specs=pl.BlockSpec((1, n16, LANE), lambda b, ki, pt: (b, 0, 0)),
            scratch_shapes=[pltpu.VMEM((n_pages * PAGE, LANE), F32)]),
        out_shape=jax.ShapeDtypeStruct((bsz, n16, LANE), F32),
        compiler_params=_cparams(("parallel", "arbitrary")),
        name="cmp_mlp_s",
    )(page_table, *([cache_t] * g), wc, pe, w2, gkc)


MOBA_BLOCKS_PER_STEP = 8


def _kmean_kernel(kv_ref, out_ref):
    row = lax.broadcasted_iota(I32, (MOBA_BLOCKS_PER_STEP, 1), 0)
    acc = jnp.zeros((MOBA_BLOCKS_PER_STEP, HEADS_W), F32)
    for j in range(MOBA_BLOCKS_PER_STEP):
        s = jnp.sum(kv_ref[0, j * MOBA_BLOCK:(j + 1) * MOBA_BLOCK, :], axis=0, keepdims=True)
        acc = acc + jnp.where(row == j, s, 0.0)
    out_ref[0] = acc * (1.0 / MOBA_BLOCK)


def _kmean_call(kv):
    bsz, tk, _ = kv.shape
    span = MOBA_BLOCKS_PER_STEP * MOBA_BLOCK
    assert tk % span == 0
    return pl.pallas_call(
        _kmean_kernel, grid=(bsz, tk // span),
        in_specs=[pl.BlockSpec((1, span, HEADS_W), lambda b, j: (b, j, 0))],
        out_specs=pl.BlockSpec((1, MOBA_BLOCKS_PER_STEP, HEADS_W), lambda b, j: (b, j, 0)),
        out_shape=jax.ShapeDtypeStruct((bsz, tk // MOBA_BLOCK, HEADS_W), F32),
        compiler_params=_cparams(("parallel", "parallel")), name="moba_kmean")(kv)


def _moba_gate_kernel(q_ref, km_ref, bm_ref, *, tq, k_top, precise):
    qi = pl.program_id(1)
    qpos = qi * tq + lax.broadcasted_iota(I32, (tq, 1), 0)
    cur = qpos // MOBA_BLOCK
    blk = lax.broadcasted_iota(I32, (1, LANE), 1)
    valid = blk < cur
    q = q_ref[0]
    km = km_ref[0]
    for h in range(N_HEADS):
        kmh = km[:, h * HEAD_DIM:(h + 1) * HEAD_DIM]
        qh = q[:, h * HEAD_DIM:(h + 1) * HEAD_DIM]
        if precise:
            gate = _mm_nt(qh, kmh, True)
        else:
            hi, lo = _split2(kmh)
            gate = _dot_nt(qh, hi) + _dot_nt(qh, lo)
        score = jnp.where(valid, gate, NEG)
        sel = jnp.where(blk == cur, 1.0, 0.0)
        for _ in range(k_top):
            m, idx = _first_argmax(score, blk)
            hit = blk == idx
            sel = jnp.where(jnp.logical_and(hit, m > 0.5 * NEG), 1.0, sel)
            score = jnp.where(hit, -jnp.inf, score)
        bm_ref[0, :, h * LANE:(h + 1) * LANE] = sel


def _moba_gate_call(q, kmean, *, tq, k_top, precise=False):
    bsz, t, _ = q.shape
    return pl.pallas_call(
        functools.partial(_moba_gate_kernel, tq=tq, k_top=k_top, precise=precise),
        grid=(bsz, t // tq),
        in_specs=[pl.BlockSpec((1, tq, HEADS_W), lambda b, i: (b, i, 0)),
                  pl.BlockSpec((1, LANE, HEADS_W), lambda b, i: (b, 0, 0))],
        out_specs=pl.BlockSpec((1, tq, N_HEADS * LANE), lambda b, i: (b, i, 0)),
        out_shape=jax.ShapeDtypeStruct((bsz, t, N_HEADS * LANE), F32),
        compiler_params=_cparams(("parallel", "parallel")),
        name="moba_gate",
    )(q, kmean)


CUM_CHUNK = 512


def _cum_kernel(l_ref, o_ref, carry_ref):
    @pl.when(pl.program_id(1) == 0)
    def _():
        carry_ref[...] = jnp.zeros(carry_ref.shape, F32)

    r = lax.broadcasted_iota(I32, (CUM_CHUNK, CUM_CHUNK), 0)
    c = lax.broadcasted_iota(I32, (CUM_CHUNK, CUM_CHUNK), 1)
    u = jnp.where(r <= c, 1.0, 0.0).astype(BF16)
    h1, h2, h3 = _split3(l_ref[0])
    cum = _dot(h1, u) + _dot(h2, u) + _dot(h3, u) + carry_ref[:, 0:1]
    o_ref[0] = cum
    carry_ref[...] = jnp.broadcast_to(cum[:, CUM_CHUNK - 1:CUM_CHUNK], carry_ref.shape)


def _cum_call(logf_t):
    bsz, rows, tk = logf_t.shape
    assert rows == SUB and tk % CUM_CHUNK == 0
    return pl.pallas_call(
        _cum_kernel,
        grid=(bsz, tk // CUM_CHUNK),
        in_specs=[pl.BlockSpec((1, SUB, CUM_CHUNK), lambda b, j: (b, 0, j))],
        out_specs=pl.BlockSpec((1, SUB, CUM_CHUNK), lambda b, j: (b, 0, j)),
        out_shape=jax.ShapeDtypeStruct(logf_t.shape, F32),
        scratch_shapes=[pltpu.VMEM((SUB, LANE), F32)],
        compiler_params=_cparams(("parallel", "arbitrary")),
        name="fox_cum",
    )(logf_t)


def _group_norm_gate(y, z, norm):
    y = y * _silu(z)
    gw = SSD_INNER // SSD_GROUPS
    outs = []
    for g in range(SSD_GROUPS):
        yg = y[:, g * gw:(g + 1) * gw]
        ms = jnp.mean(yg * yg, axis=-1, keepdims=True)
        outs.append(yg * lax.rsqrt(ms + EPS) * norm[:, g * gw:(g + 1) * gw])
    return jnp.concatenate(outs, axis=-1)


def _ssd_kernel(xbc_ref, z_ref, dt_ref, convw_ref, convb_ref, hp_ref, norm_ref, y_ref, state_ref,
                ext_ref, h_ref, *, nc, precise):
    q = SSD_CHUNK
    ci = pl.program_id(1)

    @pl.when(ci == 0)
    def _():
        ext_ref[0:SUB, :] = jnp.zeros((SUB, SSD_CONV_DIM), F32)
        h_ref[...] = jnp.zeros(h_ref.shape, F32)

    u = xbc_ref[0]
    ext_ref[SUB:SUB + q, :] = u
    acc = jnp.broadcast_to(convb_ref[...], (q, SSD_CONV_DIM))
    for i in range(SSD_CONV):
        k = SSD_CONV - 1 - i
        acc = acc + ext_ref[SUB - k:SUB - k + q, :] * convw_ref[i:i + 1, :]
    ext_ref[0:SUB, :] = u[q - SUB:, :]
    xbc = _silu(acc)
    xs = xbc[:, :SSD_INNER]
    dt = dt_ref[0]
    a = -jnp.exp(hp_ref[0:1, :])
    d_skip = hp_ref[1:2, :]
    row = lax.broadcasted_iota(I32, (q, 1), 0)
    cum = dt * a
    s = 1
    while s < q:
        cum = cum + jnp.where(row >= s, pltpu.roll(cum, s, 0), 0.0)
        s *= 2
    cum_t = cum.T
    tri = lax.broadcasted_iota(I32, (q, q), 0) >= lax.broadcasted_iota(I32, (q, q), 1)
    ys, xdd = [], []
    cbs = []
    for g in range(SSD_GROUPS):
        bm = xbc[:, SSD_INNER + g * SSD_STATE:SSD_INNER + (g + 1) * SSD_STATE]
        cm = xbc[:, SSD_INNER + SSD_BC + g * SSD_STATE:SSD_INNER + SSD_BC + (g + 1) * SSD_STATE]
        cbs.append((bm, cm, _mm_nt(cm, bm, precise)))
    rep = SSD_HEADS // SSD_GROUPS
    for h in range(SSD_HEADS):
        bm, cm, cb = cbs[h // rep]
        ch = cum[:, h:h + 1]
        lmat = jnp.where(tri, jnp.exp(ch - cum_t[h:h + 1, :]), 0.0)
        xh = xs[:, h * SSD_HEAD_DIM:(h + 1) * SSD_HEAD_DIM]
        xdt = xh * dt[:, h:h + 1]
        y = _mm(cb * lmat, xdt, precise)
        y = y + jnp.exp(ch) * _mm_nt(cm, h_ref[h], precise)
        ys.append(y + d_skip[:, h:h + 1] * xh)
        xdd.append(xdt * jnp.exp(cum[q - 1:q, h:h + 1] - ch))
    xdd_t = jnp.concatenate(xdd, axis=-1).T
    for h in range(SSD_HEADS):
        bm = cbs[h // rep][0]
        upd = _mm(xdd_t[h * SSD_HEAD_DIM:(h + 1) * SSD_HEAD_DIM, :], bm, precise)
        h_ref[h] = jnp.exp(cum[q - 1:q, h:h + 1]) * h_ref[h] + upd
    y_ref[0] = _group_norm_gate(jnp.concatenate(ys, axis=-1), z_ref[0], norm_ref[...])

    @pl.when(ci == nc - 1)
    def _():
        state_ref[0] = h_ref[...]


def _ssd_call(xbc, z, dt, convw, convb, hp, norm, precise=False):
    bsz, t, _ = xbc.shape
    nc = t // SSD_CHUNK
    blk = lambda w: pl.BlockSpec((1, SSD_CHUNK, w), lambda b, c: (b, c, 0))
    full = lambda a: pl.BlockSpec(a.shape, lambda b, c: (0,) * a.ndim)
    return pl.pallas_call(
        functools.partial(_ssd_kernel, nc=nc, precise=precise),
        grid=(bsz, nc),
        in_specs=[blk(SSD_CONV_DIM), blk(SSD_INNER), blk(LANE), full(convw), full(convb), full(hp),
                  full(norm)],
        out_specs=[blk(SSD_INNER),
                   pl.BlockSpec((1, SSD_HEADS, SSD_HEAD_DIM, SSD_STATE), lambda b, c: (b, 0, 0, 0))],
        out_shape=[jax.ShapeDtypeStruct((bsz, t, SSD_INNER), F32),
                   jax.ShapeDtypeStruct((bsz, SSD_HEADS, SSD_HEAD_DIM, SSD_STATE), F32)],
        scratch_shapes=[pltpu.VMEM((SUB + SSD_CHUNK, SSD_CONV_DIM), F32),
                        pltpu.VMEM((SSD_HEADS, SSD_HEAD_DIM, SSD_STATE), F32)],
        compiler_params=_cparams(("parallel", "arbitrary")),
        name="ssd_scan",
    )(xbc, z, dt, convw, convb, hp, norm)


def _ssd_step_kernel(full_ref, z_ref, dt_ref, st_ref, w8_ref, convb_ref, hp_ref, norm_ref,
                     y_ref, state_ref):
    acc = convb_ref[...] + jnp.sum(full_ref[0] * w8_ref[...], axis=0, keepdims=True)
    xbc = _silu(acc)
    xs = xbc[:, :SSD_INNER]
    dt = dt_ref[0]
    a = -jnp.exp(hp_ref[0:1, :])
    d_skip = hp_ref[1:2, :]
    n = SSD_HEAD_DIM
    eye = lax.broadcasted_iota(I32, (n, n), 0) == lax.broadcasted_iota(I32, (n, n), 1)
    rep = SSD_HEADS // SSD_GROUPS
    ys = []
    for h in range(SSD_HEADS):
        g = h // rep
        bm = xbc[:, SSD_INNER + g * SSD_STATE:SSD_INNER + (g + 1) * SSD_STATE]
        cm = xbc[:, SSD_INNER + SSD_BC + g * SSD_STATE:SSD_INNER + SSD_BC + (g + 1) * SSD_STATE]
        dth = dt[:, h:h + 1]
        xh = xs[:, h * n:(h + 1) * n]
        xcol = jnp.sum(jnp.where(eye, jnp.broadcast_to(xh * dth, (n, n)), 0.0), axis=1, keepdims=True)
        hn = jnp.exp(dth * a[:, h:h + 1]) * st_ref[0, h] + xcol * bm
        state_ref[0, h] = hn
        ycol = jnp.sum(hn * cm, axis=1, keepdims=True)
        yrow = jnp.sum(jnp.where(eye, jnp.broadcast_to(ycol, (n, n)), 0.0), axis=0, keepdims=True)
        ys.append(yrow + d_skip[:, h:h + 1] * xh)
    y_ref[0] = _group_norm_gate(jnp.concatenate(ys, axis=-1), z_ref[0], norm_ref[...])


def _ssd_step_call(full8, z, dt, state, w8, convb, hp, norm):
    bsz = full8.shape[0]
    one = lambda w: pl.BlockSpec((1, 1, w), lambda b: (b, 0, 0))
    full = lambda a: pl.BlockSpec(a.shape, lambda b: (0,) * a.ndim)
    st = pl.BlockSpec((1, SSD_HEADS, SSD_HEAD_DIM, SSD_STATE), lambda b: (b, 0, 0, 0))
    return pl.pallas_call(
        _ssd_step_kernel,
        grid=(bsz,),
        in_specs=[pl.BlockSpec((1, SUB, SSD_CONV_DIM), lambda b: (b, 0, 0)), one(SSD_INNER), one(LANE),
                  st, full(w8), full(convb), full(hp), full(norm)],
        out_specs=[one(SSD_INNER), st],
        out_shape=[jax.ShapeDtypeStruct((bsz, 1, SSD_INNER), F32),
                   jax.ShapeDtypeStruct(state.shape, F32)],
        compiler_params=_cparams(("parallel",)),
        name="ssd_step",
    )(full8, z, dt, state, w8, convb, hp, norm)


R_GROUP0 = MOE_EXPERTS


def _post_kernel(*refs, mode, nt):
    is_tail = (pl.program_id(0) % nt) == nt - 1
    _by_tile(mode, is_tail, lambda precise: _post_body(*refs, precise=precise))


def _post_body(x_ref, gmix_ref, wg_ref, small_ref, ocmp_ref, oslc_ref, owin_ref, omoba_ref,
               ofox_ref, ossd_ref, wbr_ref, wout_ref, gffn_ref, wr_ref, br_ref,
               x1_ref, xn2_ref, comb_ref, *, precise):
    x = x_ref[...]
    xn = x * lax.rsqrt(jnp.mean(x * x, axis=-1, keepdims=True) + EPS) * gmix_ref[...]
    xn = xn if precise else xn.astype(BF16)
    every = slice(None)
    r = lax.broadcasted_iota(I32, (LANE, 3 * HEADS_W), 0)
    c = lax.broadcasted_iota(I32, (LANE, 3 * HEADS_W), 1)
    pick = r == SM_NSAG + 3 * ((c % HEADS_W) // HEAD_DIM) + c // HEADS_W
    pick = jnp.where(pick, 1.0, 0.0).astype(BF16)
    gexp = _dot_exact(_sigmoid(small_ref[...]), pick)
    o_nsa = (gexp[:, :HEADS_W] * ocmp_ref[...] + gexp[:, HEADS_W:2 * HEADS_W] * oslc_ref[...]
             + gexp[:, 2 * HEADS_W:] * owin_ref[...])
    merged = jnp.zeros(x.shape, F32)
    off = 0
    for b, o in enumerate((o_nsa, omoba_ref[...], ofox_ref[...], ossd_ref[...])):
        w = o.shape[1]
        gate = _sigmoid(_wmm(xn, wg_ref, (every, slice(b * D_MODEL, (b + 1) * D_MODEL)), precise))
        merged = merged + gate * _wmm(o, wbr_ref, (slice(off, off + w), every), precise)
        off += w
    x1 = x + _wmm(merged, wout_ref, (every, every), precise)
    x1_ref[...] = x1
    xn2 = x1 * lax.rsqrt(jnp.mean(x1 * x1, axis=-1, keepdims=True) + EPS) * gffn_ref[...]
    xn2_ref[...] = xn2.astype(xn2_ref.dtype)
    logits = _wmm(xn2, wr_ref, (every, every), True) + br_ref[...]
    lane = lax.broadcasted_iota(I32, (1, LANE), 1)
    is_grp = jnp.logical_and(lane >= R_GROUP0, lane < R_GROUP0 + MOE_GROUPS)
    gmax, gidx = _first_argmax(jnp.where(is_grp, logits, -jnp.inf), lane)
    g_w = 1.0 / jnp.sum(jnp.where(is_grp, jnp.exp(logits - gmax), 0.0), axis=-1, keepdims=True)
    in_grp = lane // MOE_PER_GROUP == gidx - R_GROUP0
    e1 = jnp.where(in_grp, logits, -jnp.inf)
    v1, i1 = _first_argmax(e1, lane)
    e2 = jnp.where(lane == i1, -jnp.inf, e1)
    v2, i2 = _first_argmax(e2, lane)
    t = jnp.exp(v2 - v1)
    w1 = 1.0 / (1.0 + t)
    comb_ref[...] = (jnp.where(lane == i1, w1 * g_w, 0.0) + jnp.where(lane == i2, t * w1 * g_w, 0.0))


def _post_call(x2, gmix, wg, small, o_cmp, o_slc, o_win, o_moba, o_fox, o_ssd, wbr, wout, gffn,
               wr, br, tm, mode, nt):
    n = x2.shape[0]
    row = lambda a: pl.BlockSpec((tm, a.shape[1]), lambda i: (i, 0))
    full = lambda a: pl.BlockSpec(a.shape, lambda i: (0,) * a.ndim)
    big = (lambda a: pl.BlockSpec(a.shape, lambda i: (0,) * a.ndim, pipeline_mode=pl.Buffered(1))
           ) if mode != 'fast' else full
    rows = (x2, small, o_cmp, o_slc, o_win, o_moba, o_fox, o_ssd)
    return pl.pallas_call(
        functools.partial(_post_kernel, mode=mode, nt=nt),
        grid=(n // tm,),
        in_specs=[row(x2), full(gmix), big(wg)] + [row(a) for a in rows[1:]]
        + [big(wbr), big(wout), full(gffn), full(wr), full(br)],
        out_specs=[pl.BlockSpec((tm, D_MODEL), lambda i: (i, 0)),
                   pl.BlockSpec((tm, D_MODEL), lambda i: (i, 0)),
                   pl.BlockSpec((tm, LANE), lambda i: (i, 0))],
        out_shape=[jax.ShapeDtypeStruct((n, D_MODEL), F32),
                   jax.ShapeDtypeStruct((n, D_MODEL), BF16 if mode == 'fast' else F32),
                   jax.ShapeDtypeStruct((n, LANE), F32)],
        compiler_params=_cparams(("parallel",)),
        name="merge_out",
    )(x2, gmix, wg, small, o_cmp, o_slc, o_win, o_moba, o_fox, o_ssd, wbr, wout, gffn, wr, br)


def _moe_kernel(xn_ref, x1_ref, comb_ref, wg_ref, wu_ref, wd_ref, o_ref, acc_ref, *, precise):
    e = pl.program_id(1)

    @pl.when(e == 0)
    def _():
        acc_ref[...] = x1_ref[...]

    lane = lax.broadcasted_iota(I32, (1, LANE), 1)
    cw = jnp.sum(jnp.where(lane == e, comb_ref[...], 0.0), axis=-1, keepdims=True)
    xn = xn_ref[...]
    h = _silu(_mm(xn, wg_ref[0], precise)) * _mm(xn, wu_ref[0], precise)
    acc_ref[...] += cw * _mm(h, wd_ref[0], precise)

    @pl.when(e == MOE_EXPERTS - 1)
    def _():
        o_ref[...] = acc_ref[...]


def _moe_call(xn2, x1, comb, wg, wu, wd, tm, precise=False):
    n = xn2.shape[0]
    return pl.pallas_call(
        functools.partial(_moe_kernel, precise=precise),
        grid=(n // tm, MOE_EXPERTS),
        in_specs=[pl.BlockSpec((tm, D_MODEL), lambda i, e: (i, 0)),
                  pl.BlockSpec((tm, D_MODEL), lambda i, e: (i, 0)),
                  pl.BlockSpec((tm, LANE), lambda i, e: (i, 0)),
                  pl.BlockSpec((1, D_MODEL, MOE_HIDDEN), lambda i, e: (e, 0, 0)),
                  pl.BlockSpec((1, D_MODEL, MOE_HIDDEN), lambda i, e: (e, 0, 0)),
                  pl.BlockSpec((1, MOE_HIDDEN, D_MODEL), lambda i, e: (e, 0, 0))],
        out_specs=pl.BlockSpec((tm, D_MODEL), lambda i, e: (i, 0)),
        out_shape=jax.ShapeDtypeStruct((n, D_MODEL), F32),
        scratch_shapes=[pltpu.VMEM((tm, D_MODEL), F32)],
        compiler_params=_cparams(("parallel", "arbitrary")),
        name="moe_experts",
    )(xn2, x1, comb, wg, wu, wd)


def _layer_weights(prm):
    wp, wgate = _pack_w_in(prm['w_in'])
    wr = jnp.concatenate([prm['w_router_exp'], prm['w_router_grp'],
                          jnp.zeros((D_MODEL, LANE - MOE_EXPERTS - MOE_GROUPS), F32)], axis=1)
    br = jnp.concatenate([prm['b_router_exp'], prm['b_router_grp'],
                          jnp.zeros((LANE - MOE_EXPERTS - MOE_GROUPS,), F32)])[None, :]
    pad = lambda v: jnp.concatenate([v.astype(F32), jnp.zeros((LANE - v.shape[0],), F32)])
    hp = jnp.stack([pad(prm['ssd_a_log']), pad(prm['ssd_d'])] + [jnp.zeros((LANE,), F32)] * 6)
    w8 = jnp.concatenate([prm['ssd_conv_w'], jnp.zeros((SUB - SSD_CONV, SSD_CONV_DIM), F32)], axis=0)
    wbr = jnp.concatenate([prm['w_br_nsa'], prm['w_br_moba'], prm['w_br_fox'], prm['w_br_ssd']], axis=0)
    shared = dict(gmix=prm['norm_mix'][None, :], gffn=prm['norm_ffn'][None, :], gains=_proj_gains(prm),
                  biases=_proj_biases(prm), br=br, hp=hp, w8=w8, convw=prm['ssd_conv_w'],
                  convb=prm['ssd_conv_b'][None, :], norm=prm['ssd_norm'][None, :],
                  wr=_split_w(wr, True))
    big = lambda precise: dict(wp=_split_w(wp, precise), wgate=_split_w(wgate, precise),
                               wbr=_split_w(wbr, precise), wout=_split_w(prm['w_out'], precise))
    fast = dict(shared, **big(False), cmp=_cmp_weights(prm, BF16),
                weg=prm['w_exp_gate'].astype(BF16), weu=prm['w_exp_up'].astype(BF16),
                wed=prm['w_exp_down'].astype(BF16))
    exact = dict(shared, **big(True), cmp=_cmp_weights(prm, F32),
                 weg=prm['w_exp_gate'], weu=prm['w_exp_up'], wed=prm['w_exp_down'])
    return fast, exact


def _pad_rows(a, rows):
    return jnp.pad(a, ((0, 0), (0, rows - a.shape[1])) + ((0, 0),) * (a.ndim - 2))


def _heads_to_sublanes(logf, tk_pad):
    lt = jnp.swapaxes(logf, 1, 2)
    return jnp.pad(lt, ((0, 0), (0, SUB - lt.shape[1]), (0, tk_pad - lt.shape[2])))


def _layer_prompt(x, w, w_ffn, *, tm, tq, tk, mode):
    precise = mode != 'fast'
    bsz, t, _ = x.shape
    n = bsz * t
    x2 = x.reshape(n, D_MODEL)
    tabs = _rope_tables(jnp.arange(t, dtype=I32))
    (qn, cmp_rows, slc_rows, win_rows, qm, moba_rows, qf, fox_rows, z, xbc, small, dt) = _proj_call(
        x2, w['gmix'], w['wp'], w['gains'], tabs, w['biases'], tm, mode)
    b3 = lambda a: a.reshape(bsz, t, a.shape[-1])
    qn, qm, qf = b3(qn), b3(qm), b3(qf)
    kvc = _cmp_mlp_call(cmp_rows.reshape(bsz, t // NSA_CMP_STRIDE, NSA_CMP_STRIDE * LANE), *w['cmp'],
                        precise=precise)
    n_cmp = (t - NSA_CMP_LEN) // NSA_CMP_STRIDE + 1
    n_sel = -(-t // NSA_SEL_BLOCK)
    nselp = -(-n_sel // LANE) * LANE
    o_cmp, sel, _ = _nsa_cmp_call(qn, kvc, tq=tq, q_pos0=0, n_cmp=n_cmp, n_sel=n_sel, nselp=nselp,
                                  precise=precise)
    o_slc = _attn_call(qn, b3(slc_rows), tq=tq, tk=tk, bm=sel, bm_bs=NSA_SEL_BLOCK, mode=mode,
                       name="nsa_slc")
    tw = min(tq, tk, NSA_WINDOW // 2)
    o_win = _attn_call(qn, b3(win_rows), tq=tw, tk=tw, window=NSA_WINDOW, mode=mode, name="nsa_win")
    moba3 = b3(moba_rows)
    kmean = _pad_rows(_kmean_call(moba3), LANE)
    n_blk = -(-t // MOBA_BLOCK)
    bm = _moba_gate_call(qm, kmean, tq=tq, k_top=min(MOBA_TOPK, n_blk - 1), precise=precise)
    o_moba = _attn_call(qm, moba3, tq=tq, tk=tk, bm=bm, bm_bs=MOBA_BLOCK, bm_hs=LANE, mode=mode,
                        name="moba")
    logf = small[:, SM_FOXF:SM_FOXF + N_HEADS].reshape(bsz, t, N_HEADS)
    cum_t = _cum_call(_heads_to_sublanes(logf, t))
    cq = jnp.pad(jnp.swapaxes(cum_t, 1, 2), ((0, 0), (0, 0), (0, LANE - SUB)))
    o_fox = _attn_call(qf, b3(fox_rows), tq=tq, tk=tk, cq=cq, cum_t=cum_t, mode=mode, name="fox")
    y_ssd, ssd_state = _ssd_call(b3(xbc), b3(z), b3(dt), w['convw'], w['convb'], w['hp'], w['norm'],
                                 precise)
    f2 = lambda a: a.reshape(n, a.shape[-1])
    x1, xn2, comb = _post_call(x2, w['gmix'], w['wgate'], small, f2(o_cmp), f2(o_slc), f2(o_win),
                               f2(o_moba), f2(o_fox), f2(y_ssd), w['wbr'], w['wout'], w['gffn'],
                               w['wr'], w['br'], tm, mode, t // tm)
    y = _moe_call(xn2, x1, comb, w_ffn['weg'], w_ffn['weu'], w_ffn['wed'], min(n, 4 * tm))
    wb = min(NSA_WINDOW, t)
    state = (cmp_rows.reshape(bsz, t, 2, HEAD_DIM), slc_rows.reshape(bsz, t, 2, HEAD_DIM),
             win_rows.reshape(bsz, t, 2, HEAD_DIM)[:, t - wb:],
             moba_rows.reshape(bsz, t, 2, N_HEADS, HEAD_DIM), fox_rows.reshape(bsz, t, 2, N_HEADS, HEAD_DIM),
             logf, ssd_state, b3(xbc)[:, t - (SSD_CONV - 1):])
    return y.reshape(bsz, t, D_MODEL), state


def _cache_views(cache_nsa_cmp, cache_nsa_slc, state_nsa_win, cache_moba, cache_fox, cache_fox_logf):
    rows_last = lambda c: jnp.moveaxis(c, 2, -1)
    return (rows_last(cache_nsa_cmp), rows_last(cache_nsa_slc)[:, :, :, None],
            rows_last(state_nsa_win)[:, :, :, None], rows_last(cache_moba), rows_last(cache_fox),
            rows_last(cache_fox_logf))


def _layer_sample(x, w, views, layer, s_win, s_ssd, s_conv, page_table, *, g_pages):
    cmp_t, slc_t, win_t, moba_t, fox_t, logf_t = views
    bsz = x.shape[0]
    n_pages = page_table.shape[1]
    pos0 = n_pages * PAGE
    x2 = x.reshape(bsz, D_MODEL)
    tabs = _rope_tables(jnp.full((bsz,), pos0, I32))
    (qn, cmp_rows, slc_rows, win_rows, qm, moba_rows, qf, fox_rows, z, xbc, small, dt) = _proj_call(
        x2, w['gmix'], w['wp'], w['gains'], tabs, w['biases'], bsz, 'precise')
    q8 = lambda a: _pad_rows(a[:, None, :], SUB)
    qn, qm, qf = q8(qn), q8(qm), q8(qf)
    every_head = lambda n: jnp.full((bsz, n), -1, I32)
    assert (pos0 + 1 - NSA_CMP_LEN) // NSA_CMP_STRIDE + 1 == pos0 // NSA_CMP_STRIDE - 1
    kvc = _kt_cmp_call(cmp_t, layer, page_table, *w['cmp'], g=g_pages)
    n_cmp = pos0 // NSA_CMP_STRIDE - 1
    n_sel = -(-(pos0 + 1) // NSA_SEL_BLOCK)
    nselp = -(-n_sel // LANE) * LANE
    o_cmp, _, sel_idx = _nsa_cmp_call(qn, kvc, tq=SUB, q_pos0=pos0, n_cmp=n_cmp, n_sel=n_sel,
                                      nselp=nselp, precise=True)
    k_top = min(NSA_SEL_TOPK, n_sel)
    sel_ids = sel_idx[:, 0, :k_top]
    sel_heads = jnp.where(sel_ids < pos0 // NSA_SEL_BLOCK, -1, N_HEADS)
    o_slc = _kt_attn_call(qn, slc_t, layer, page_table, sel_ids, sel_heads, q8(slc_rows),
                          r=NSA_SEL_BLOCK, g=k_top, name="nsa_slc_s")
    wb = s_win.shape[1]
    o_win = _kt_attn_call(qn, win_t, layer, jnp.arange(bsz, dtype=I32)[:, None], jnp.zeros((bsz, 1), I32),
                          every_head(1), q8(win_rows), r=wb, g=1, lo=wb - NSA_WINDOW + 1, name="nsa_win_s")
    n_blk = -(-(pos0 + 1) // MOBA_BLOCK)
    m_top = min(MOBA_TOPK, n_blk - 1)
    blk_ids = _kt_gate_call(qm, moba_t, layer, page_table, g=g_pages, k_top=m_top)[:, :N_HEADS, :m_top]
    ppb = MOBA_BLOCK // PAGE
    moba_fetch = (blk_ids[..., None] * ppb + jnp.arange(ppb, dtype=I32)).reshape(bsz, -1)
    head_id = jnp.arange(N_HEADS, dtype=I32)[None, :, None, None]
    moba_heads = jnp.broadcast_to(jnp.where(blk_ids[..., None] >= 0, head_id, N_HEADS),
                                  blk_ids.shape + (ppb,)).reshape(bsz, -1)
    o_moba = _kt_attn_call(qm, moba_t, layer, page_table, moba_fetch, moba_heads, q8(moba_rows), r=PAGE,
                           g=m_top * ppb, one_head=True, name="moba_s")
    logf_new = small[:, SM_FOXF:SM_FOXF + N_HEADS]
    fox_fetch = jnp.tile(jnp.arange(n_pages - 1, -1, -1, dtype=I32)[None, :], (bsz, 1))
    o_fox = _kt_attn_call(qf, fox_t, layer, page_table, fox_fetch, every_head(n_pages), q8(fox_rows),
                          r=PAGE, g=g_pages, logf_t=logf_t, logf_new=q8(small), name="fox_s")
    full = jnp.concatenate([s_conv, xbc[:, None, :]], axis=1)
    y_ssd, ssd_state = _ssd_step_call(_pad_rows(full, SUB), z[:, None, :], dt[:, None, :], s_ssd,
                                      w['w8'], w['convb'], w['hp'], w['norm'])
    first = lambda a: a[:, 0, :]
    x1, xn2, comb = _post_call(x2, w['gmix'], w['wgate'], small, first(o_cmp), first(o_slc), first(o_win),
                               first(o_moba), first(o_fox), first(y_ssd), w['wbr'], w['wout'],
                               w['gffn'], w['wr'], w['br'], bsz, 'precise', 1)
    y = _moe_call(xn2, x1, comb, w['weg'], w['weu'], w['wed'], bsz, precise=True)
    state = (cmp_rows.reshape(bsz, 1, 2, HEAD_DIM), slc_rows.reshape(bsz, 1, 2, HEAD_DIM),
             jnp.concatenate([s_win[:, 1:], win_rows.reshape(bsz, 1, 2, HEAD_DIM)], axis=1),
             moba_rows.reshape(bsz, 1, 2, N_HEADS, HEAD_DIM), fox_rows.reshape(bsz, 1, 2, N_HEADS, HEAD_DIM),
             logf_new[:, None, :], ssd_state, full[:, 1:])
    return y.reshape(bsz, 1, D_MODEL), state


_PARAM_NAMES = ('norm_mix', 'norm_ffn', 'w_in', 'g_nsa_q', 'g_nsa_k', 'g_nsa_kc', 'pe_cmp_k', 'pe_cmp_v',
                'w_cmp_k1', 'w_cmp_k2', 'w_cmp_v1', 'w_cmp_v2', 'g_moba_q', 'g_moba_k', 'g_fox_q',
                'g_fox_k', 'b_fox_f', 'ssd_conv_w', 'ssd_conv_b', 'ssd_dt_bias', 'ssd_a_log', 'ssd_d',
                'ssd_norm', 'w_br_nsa', 'w_br_moba', 'w_br_fox', 'w_br_ssd', 'w_out', 'w_router_grp',
                'b_router_grp', 'w_router_exp', 'b_router_exp', 'w_exp_gate', 'w_exp_up', 'w_exp_down')


def kernel(x_prompt, x_sample, cache_nsa_cmp, cache_nsa_slc, state_nsa_win, cache_moba, cache_fox, cache_fox_logf, state_ssd, state_ssd_conv, page_table, norm_mix, norm_ffn, w_in, g_nsa_q, g_nsa_k, g_nsa_kc, pe_cmp_k, pe_cmp_v, w_cmp_k1, w_cmp_k2, w_cmp_v1, w_cmp_v2, g_moba_q, g_moba_k, g_fox_q, g_fox_k, b_fox_f, ssd_conv_w, ssd_conv_b, ssd_dt_bias, ssd_a_log, ssd_d, ssd_norm, w_br_nsa, w_br_moba, w_br_fox, w_br_ssd, w_out, w_router_grp, b_router_grp, w_router_exp, b_router_exp, w_exp_gate, w_exp_up, w_exp_down):
    params = dict(zip(_PARAM_NAMES, (
        norm_mix, norm_ffn, w_in, g_nsa_q, g_nsa_k, g_nsa_kc, pe_cmp_k, pe_cmp_v, w_cmp_k1, w_cmp_k2,
        w_cmp_v1, w_cmp_v2, g_moba_q, g_moba_k, g_fox_q, g_fox_k, b_fox_f, ssd_conv_w, ssd_conv_b,
        ssd_dt_bias, ssd_a_log, ssd_d, ssd_norm, w_br_nsa, w_br_moba, w_br_fox, w_br_ssd, w_out,
        w_router_grp, b_router_grp, w_router_exp, b_router_exp, w_exp_gate, w_exp_up, w_exp_down)))
    depth = norm_mix.shape[0]
    views = _cache_views(cache_nsa_cmp, cache_nsa_slc, state_nsa_win, cache_moba, cache_fox, cache_fox_logf)
    xp, xs = x_prompt, x_sample
    sp, ss = [], []
    for l in range(depth):
        w_fast, w_exact = _layer_weights({k: v[l] for k, v in params.items()})
        feeds_next = l < depth - 1
        xp, st_p = _layer_prompt(xp, w_exact if feeds_next else w_fast, w_fast, tm=256, tq=512, tk=512,
                                 mode='tail' if feeds_next else 'fast')
        xs, st_s = _layer_sample(xs, w_exact, views, l, state_nsa_win[l], state_ssd[l], state_ssd_conv[l],
                                 page_table, g_pages=16)
        sp.append(st_p)
        ss.append(st_s)
    outs = [xp, xs]
    for i in range(8):
        outs.append(jnp.stack([s[i] for s in sp], axis=0))
        outs.append(jnp.stack([s[i] for s in ss], axis=0))
    return tuple(outs)
```

```python
import functools
import math

import jax
import jax.numpy as jnp
import numpy as np
from jax import lax
from jax.experimental import pallas as pl
from jax.experimental.pallas import tpu as pltpu

F32 = jnp.float32
BF16 = jnp.bfloat16
I32 = jnp.int32

D_MODEL = 1024
HEAD_DIM = 64
N_HEADS = 4
HEADS_W = N_HEADS * HEAD_DIM
ROPE_DIM = HEAD_DIM // 4
ROPE_HALF = ROPE_DIM // 2
ROPE_THETA = 500000.0
SCALE = HEAD_DIM ** -0.5
EPS = 1e-6
NEG = -1e30
PAGE = 128

NSA_CMP_LEN = 32
NSA_CMP_STRIDE = 16
NSA_CMP_HIDDEN = 2 * HEAD_DIM
NSA_SEL_BLOCK = 64
NSA_SEL_TOPK = 16
NSA_WINDOW = 512
MOBA_BLOCK = 256
MOBA_TOPK = 3

SSD_HEADS = 8
SSD_HEAD_DIM = 64
SSD_INNER = SSD_HEADS * SSD_HEAD_DIM
SSD_STATE = 64
SSD_GROUPS = 2
SSD_CONV = 4
SSD_CHUNK = 128
SSD_BC = SSD_GROUPS * SSD_STATE
SSD_CONV_DIM = SSD_INNER + 2 * SSD_BC

MOE_GROUPS = 4
MOE_PER_GROUP = 4
MOE_EXPERTS = 16
MOE_HIDDEN = 512

LANE = 128
SUB = 8
VMEM_LIMIT = 56 * 1024 * 1024

_IN_SIZES = (HEADS_W, 6 * HEAD_DIM, 3 * N_HEADS, 3 * HEADS_W, 3 * HEADS_W, N_HEADS,
             2 * SSD_INNER + 2 * SSD_BC + SSD_HEADS, 4 * D_MODEL)
_IN_OFFS = tuple(int(sum(_IN_SIZES[:i])) for i in range(len(_IN_SIZES) + 1))

P_NSA_Q = 0
P_NSA_KV = 256
P_MOBA = 640
P_FOX = 1408
P_SSD = 2176
P_SMALL = 3456
P_DT = 3584
P_WIDTH = 3712
SM_FOXF = 0
SM_NSAG = 4


def _cparams(sem):
    return pltpu.CompilerParams(dimension_semantics=sem, vmem_limit_bytes=VMEM_LIMIT)


def _sigmoid(x):
    return 1.0 / (1.0 + jnp.exp(-x))


def _silu(x):
    return x * _sigmoid(x)


def _softplus(x):
    return jnp.maximum(x, 0.0) + jnp.log(1.0 + jnp.exp(-jnp.abs(x)))


def _dot(a, b):
    return jnp.dot(a, b, preferred_element_type=F32)


def _dot_nt(a, b):
    return lax.dot_general(a, b, (((1,), (1,)), ((), ())), preferred_element_type=F32)


def _split2(x):
    hi = x.astype(BF16)
    lo = (x - hi.astype(F32)).astype(BF16)
    return hi, lo


def _split3(x):
    h1 = x.astype(BF16)
    r = x - h1.astype(F32)
    h2 = r.astype(BF16)
    h3 = (r - h2.astype(F32)).astype(BF16)
    return h1, h2, h3


def _mm(a, b, precise):
    if precise:
        ah, al = _split2(a)
        bh, bl = _split2(b)
        return _dot(ah, bh) + _dot(ah, bl) + _dot(al, bh)
    return _dot(a.astype(BF16), b.astype(BF16))


def _wmm(a, w_ref, idx, precise):
    w_hi = w_ref[(0,) + idx]
    if precise:
        ah, al = _split2(a)
        return _dot(ah, w_hi) + _dot(ah, w_ref[(1,) + idx]) + _dot(al, w_hi)
    return _dot(a.astype(BF16), w_hi)


def _split_w(w, precise):
    if not precise:
        return w.astype(BF16)[None]
    bits = lax.bitcast_convert_type(w, jnp.uint32)
    bits = (bits + jnp.uint32(0x7FFF) + ((bits >> 16) & jnp.uint32(1))) & jnp.uint32(0xFFFF0000)
    hi = lax.bitcast_convert_type(bits, F32)
    return jnp.stack([hi.astype(BF16), (w - hi).astype(BF16)])


def _by_tile(mode, is_tail, body):
    if mode == 'tail':
        pl.when(is_tail)(lambda: body(True))
        pl.when(jnp.logical_not(is_tail))(lambda: body(False))
    else:
        body(mode == 'precise')


def _mm_nt(a, b, precise):
    if precise:
        ah, al = _split2(a)
        bh, bl = _split2(b)
        return _dot_nt(ah, bh) + _dot_nt(ah, bl) + _dot_nt(al, bh)
    return _dot_nt(a.astype(BF16), b.astype(BF16))


def _dot_exact(x, e):
    h1, h2, h3 = _split3(x)
    return _dot(h1, e) + _dot(h2, e) + _dot(h3, e)


def _head_sumsq(y, precise=False):
    r = lax.broadcasted_iota(I32, (LANE, LANE), 0) // HEAD_DIM
    c = lax.broadcasted_iota(I32, (LANE, LANE), 1) // HEAD_DIM
    e = jnp.where(r == c, 1.0, 0.0).astype(BF16)
    if precise:
        return _dot_exact(y * y, e)
    hi, lo = _split2(y * y)
    return _dot(hi, e) + _dot(lo, e)


def _rope_chunk(y, cos_t, sin_a, sin_b):
    return (y * cos_t + pltpu.roll(y, LANE - ROPE_HALF, 1) * sin_a
            + pltpu.roll(y, ROPE_HALF, 1) * sin_b)


def _proj_kernel(*refs, mode, nt):
    is_tail = (pl.program_id(0) % nt) == nt - 1
    _by_tile(mode, is_tail,
             lambda precise: _proj_body(*refs, precise=precise, keys_precise=mode != 'fast'))


def _proj_body(x_ref, gmix_ref, w_ref, gains_ref, tab_ref, bias_ref,
               qn_ref, cmp_ref, slc_ref, win_ref, qm_ref, moba_ref, qf_ref, fox_ref,
               z_ref, xbc_ref, small_ref, dt_ref, *, precise, keys_precise):
    x = x_ref[...]
    ms = jnp.mean(x * x, axis=-1, keepdims=True)
    xn = x * lax.rsqrt(ms + EPS) * gmix_ref[...]
    xn_fast = xn.astype(BF16)
    cos_t = tab_ref[0]
    sin_a = tab_ref[1]
    sin_b = tab_ref[2]
    lane = lax.broadcasted_iota(I32, (1, LANE), 1)
    first = lane < HEAD_DIM

    def proj(off, width, shared=False):
        prec = precise or (shared and keys_precise)
        return _wmm(xn if prec else xn_fast, w_ref, (slice(None), slice(off, off + width)), prec)

    def normed(y, gain_row, k_only=False, shared=False):
        g = gains_ref[gain_row:gain_row + 1, :]
        prec = precise or (shared and keys_precise)
        yn = y * lax.rsqrt(_head_sumsq(y, prec) * (1.0 / HEAD_DIM) + EPS) * g
        return jnp.where(first, yn, y) if k_only else yn

    p = proj(P_NSA_Q, HEADS_W)
    for c in range(2):
        y = _rope_chunk(normed(p[:, c * LANE:(c + 1) * LANE], 0), cos_t, sin_a, sin_b)
        qn_ref[:, c * LANE:(c + 1) * LANE] = (y * SCALE).astype(qn_ref.dtype)
    p = proj(P_NSA_KV, 3 * LANE, shared=True)
    cos_k = jnp.where(first, cos_t, 1.0)
    sin_ak = jnp.where(first, sin_a, 0.0)
    sin_bk = jnp.where(first, sin_b, 0.0)
    for c, ref in enumerate((cmp_ref, slc_ref, win_ref)):
        y = normed(p[:, c * LANE:(c + 1) * LANE], 1 + c, k_only=True, shared=True)
        ref[...] = _rope_chunk(y, cos_k, sin_ak, sin_bk)
    p = proj(P_MOBA, HEADS_W)
    pkv = proj(P_MOBA + HEADS_W, 2 * HEADS_W, shared=True)
    for c in range(2):
        y = _rope_chunk(normed(p[:, c * LANE:(c + 1) * LANE], 4), cos_t, sin_a, sin_b)
        qm_ref[:, c * LANE:(c + 1) * LANE] = (y * SCALE).astype(qn_ref.dtype)
        y = _rope_chunk(normed(pkv[:, c * LANE:(c + 1) * LANE], 5, shared=True), cos_t, sin_a, sin_b)
        moba_ref[:, c * LANE:(c + 1) * LANE] = y
    moba_ref[:, HEADS_W:] = pkv[:, HEADS_W:]
    p = proj(P_FOX, HEADS_W)
    pkv = proj(P_FOX + HEADS_W, 2 * HEADS_W, shared=True)
    for c in range(2):
        y = normed(p[:, c * LANE:(c + 1) * LANE], 6)
        qf_ref[:, c * LANE:(c + 1) * LANE] = (y * SCALE).astype(qn_ref.dtype)
        fox_ref[:, c * LANE:(c + 1) * LANE] = normed(pkv[:, c * LANE:(c + 1) * LANE], 7, shared=True)
    fox_ref[:, HEADS_W:] = pkv[:, HEADS_W:]
    z_ref[...] = proj(P_SSD, SSD_INNER)
    xbc_ref[...] = proj(P_SSD + SSD_INNER, SSD_CONV_DIM, shared=True)
    p = proj(P_SMALL, LANE, shared=True)
    logf = -_softplus(-(p + bias_ref[0:1, :]))
    small_ref[...] = jnp.where(lane < SM_NSAG, logf, p)
    p = proj(P_DT, LANE, shared=True)
    dt_ref[...] = _softplus(p + bias_ref[1:2, :])


def _proj_call(x2, gmix, wp, gains, tabs, biases, tm, mode):
    n = x2.shape[0]
    nt = tabs.shape[1] // tm
    row = lambda w: pl.BlockSpec((tm, w), lambda i: (i, 0))
    full = lambda a: pl.BlockSpec(a.shape, lambda i: (0,) * a.ndim)
    widths = (HEADS_W, LANE, LANE, LANE, HEADS_W, 2 * HEADS_W, HEADS_W, 2 * HEADS_W,
              SSD_INNER, SSD_CONV_DIM, LANE, LANE)
    qd = BF16 if mode == 'fast' else F32
    dtypes = (qd, F32, F32, F32, qd, F32, qd, F32, F32, F32, F32, F32)
    return pl.pallas_call(
        functools.partial(_proj_kernel, mode=mode, nt=nt),
        grid=(n // tm,),
        in_specs=[row(D_MODEL), full(gmix), full(wp), full(gains),
                  pl.BlockSpec((3, tm, LANE), lambda i: (0, i % nt, 0)), full(biases)],
        out_specs=[row(w) for w in widths],
        out_shape=[jax.ShapeDtypeStruct((n, w), d) for w, d in zip(widths, dtypes)],
        compiler_params=_cparams(("parallel",)),
        name="proj",
    )(x2, gmix, wp, gains, tabs, biases)


def _rope_tables(pos):
    inv = jnp.power(ROPE_THETA, -jnp.arange(ROPE_HALF, dtype=F32) / ROPE_HALF)
    ang = pos.astype(F32)[:, None] * inv[None, :]
    cos, sin = jnp.cos(ang), jnp.sin(ang)
    t = pos.shape[0]
    one = jnp.ones((t, HEAD_DIM - ROPE_DIM), F32)
    zero = jnp.zeros((t, HEAD_DIM - ROPE_DIM), F32)
    zh = jnp.zeros((t, ROPE_HALF), F32)
    c = jnp.concatenate([cos, cos, one], axis=1)
    a = jnp.concatenate([-sin, zh, zero], axis=1)
    b = jnp.concatenate([zh, sin, zero], axis=1)
    return jnp.stack([jnp.tile(c, (1, 2)), jnp.tile(a, (1, 2)), jnp.tile(b, (1, 2))], axis=0)


def _pack_w_in(w_in):
    o = _IN_OFFS
    ssd = w_in[:, o[6]:o[7]]
    small = jnp.concatenate([w_in[:, o[5]:o[6]], w_in[:, o[2]:o[3]],
                             jnp.zeros((D_MODEL, LANE - 4 * N_HEADS), F32)], axis=1)
    dt = jnp.concatenate([ssd[:, 2 * SSD_INNER + 2 * SSD_BC:],
                          jnp.zeros((D_MODEL, LANE - SSD_HEADS), F32)], axis=1)
    wp = jnp.concatenate([w_in[:, o[0]:o[2]], w_in[:, o[3]:o[5]],
                          ssd[:, :2 * SSD_INNER + 2 * SSD_BC], small, dt], axis=1)
    return wp, w_in[:, o[7]:o[8]]


def _proj_gains(prm):
    two = lambda g: jnp.tile(g, 2)
    ones = jnp.ones((HEAD_DIM,), F32)
    rows = [two(prm['g_nsa_q'])]
    rows += [jnp.concatenate([prm['g_nsa_k'][i], ones]) for i in range(3)]
    rows += [two(prm['g_moba_q']), two(prm['g_moba_k']), two(prm['g_fox_q']), two(prm['g_fox_k'])]
    return jnp.stack(rows, axis=0)


def _proj_biases(prm):
    pad = lambda v: jnp.concatenate([v.astype(F32), jnp.zeros((LANE - v.shape[0],), F32)])
    rows = [pad(prm['b_fox_f']), pad(prm['ssd_dt_bias'])] + [jnp.zeros((LANE,), F32)] * 6
    return jnp.stack(rows, axis=0)


def _cmp_mlp_kernel(r_ref, wab_ref, pe_ref, w2_ref, g_ref, out_ref, *, precise):
    n16 = r_ref.shape[1]
    hw = 2 * NSA_CMP_HIDDEN
    h = _mm(r_ref[0], wab_ref[...], precise)
    c = _mm(pe_ref[...], wab_ref[...], precise)
    c = c[0:1, :hw] + c[1:2, hw:]
    pre = h[:, :hw] + pltpu.roll(h[:, hw:], n16 - 1, 0) + c
    act = 0.5 * pre * (1.0 + jnp.tanh(math.sqrt(2.0 / math.pi) * (pre + 0.044715 * pre * pre * pre)))
    y = _mm(act, w2_ref[...], precise)
    lane = lax.broadcasted_iota(I32, (1, LANE), 1)
    yn = y * lax.rsqrt(_head_sumsq(y, precise) * (1.0 / HEAD_DIM) + EPS) * g_ref[...]
    out_ref[0] = jnp.where(lane < HEAD_DIM, yn, y)


def _cmp_mlp_call(rows16, wab, pe2, w2, gkc, precise=False):
    bsz, n16, w = rows16.shape
    full = lambda a: pl.BlockSpec(a.shape, lambda b: (0,) * a.ndim)
    return pl.pallas_call(
        functools.partial(_cmp_mlp_kernel, precise=precise),
        grid=(bsz,),
        in_specs=[pl.BlockSpec((1, n16, w), lambda b: (b, 0, 0)), full(wab), full(pe2), full(w2),
                  full(gkc)],
        out_specs=pl.BlockSpec((1, n16, LANE), lambda b: (b, 0, 0)),
        out_shape=jax.ShapeDtypeStruct((bsz, n16, LANE), F32),
        compiler_params=_cparams(("parallel",)),
        name="cmp_mlp",
    )(rows16, wab, pe2, w2, gkc)


def _cmp_weights(prm, dtype):
    def expand(w1, slot):
        w = w1.reshape(2, NSA_CMP_STRIDE, HEAD_DIM, NSA_CMP_HIDDEN)
        z = jnp.zeros_like(w)
        pair = (w, z) if slot == 0 else (z, w)
        return jnp.concatenate(pair, axis=2).reshape(2, NSA_CMP_STRIDE * LANE, NSA_CMP_HIDDEN)
    wk, wv = expand(prm['w_cmp_k1'], 0), expand(prm['w_cmp_v1'], 1)
    wab = jnp.concatenate([wk[0], wv[0], wk[1], wv[1]], axis=1).astype(dtype)
    pe = jnp.concatenate([prm['pe_cmp_k'], prm['pe_cmp_v']], axis=1)
    pe = pe.reshape(2, NSA_CMP_STRIDE * LANE)
    pe2 = jnp.concatenate([pe, jnp.zeros((SUB - 2, NSA_CMP_STRIDE * LANE), F32)], axis=0)
    z = jnp.zeros((NSA_CMP_HIDDEN, HEAD_DIM), F32)
    w2 = jnp.concatenate([jnp.concatenate([prm['w_cmp_k2'], z], axis=1),
                          jnp.concatenate([z, prm['w_cmp_v2']], axis=1)], axis=0).astype(dtype)
    gkc = jnp.concatenate([prm['g_nsa_kc'], jnp.ones((HEAD_DIM,), F32)])[None, :]
    return wab, pe2, w2, gkc


def _first_argmax(score, lane):
    m = jnp.max(score, axis=-1, keepdims=True)
    cand = jnp.where(score == m, lane.astype(F32), float(score.shape[-1]))
    return m, jnp.min(cand, axis=-1, keepdims=True).astype(I32)


def _nsa_cmp_kernel(q_ref, kvc_ref, ocmp_ref, sel_ref, idx_ref, *, tq, q_pos0, n_cmp, n_sel, k_top,
                    precise):
    n16 = kvc_ref.shape[1]
    nselp = sel_ref.shape[2]
    qi = pl.program_id(1)
    qpos = q_pos0 + qi * tq + lax.broadcasted_iota(I32, (tq, 1), 0)
    q = q_ref[0]
    kvc = kvc_ref[0]
    kc = kvc[:, :HEAD_DIM] if precise else kvc[:, :HEAD_DIM].astype(BF16)
    vc = kvc[:, HEAD_DIM:] if precise else kvc[:, HEAD_DIM:].astype(BF16)
    n_idx = lax.broadcasted_iota(I32, (1, n16), 1)
    mask = jnp.logical_and(n_idx * NSA_CMP_STRIDE + (NSA_CMP_LEN - 1) <= qpos, n_idx < n_cmp)
    psum = jnp.zeros((tq, n16), F32)
    outs = []
    for h in range(N_HEADS):
        lg = jnp.where(mask, _mm_nt(q[:, h * HEAD_DIM:(h + 1) * HEAD_DIM], kc, precise), NEG)
        m = jnp.max(lg, axis=-1, keepdims=True)
        p = jnp.where(mask, jnp.exp(lg - m), 0.0)
        p = p / jnp.maximum(jnp.sum(p, axis=-1, keepdims=True), 1e-30)
        outs.append(_mm(p, vc, precise))
        psum = psum + p
    ocmp_ref[0] = jnp.concatenate(outs, axis=-1)
    n_col = lax.broadcasted_iota(I32, (n16, nselp), 0) * NSA_CMP_STRIDE
    s_row = lax.broadcasted_iota(I32, (n16, nselp), 1) * NSA_SEL_BLOCK
    cover = jnp.logical_and(n_col <= s_row + (NSA_SEL_BLOCK - 1), n_col + (NSA_CMP_LEN - 1) >= s_row)
    cover = jnp.where(cover, 1.0, 0.0).astype(BF16)
    imp = _dot_exact(psum, cover)
    blk = lax.broadcasted_iota(I32, (1, nselp), 1)
    cur = qpos // NSA_SEL_BLOCK
    forced = jnp.logical_or(blk == 0, jnp.logical_or(blk == cur, blk == cur - 1))
    score = jnp.where(blk > cur, -1e9, jnp.where(forced, 1e9, imp))
    score = jnp.where(blk < n_sel, score, -jnp.inf)
    sel = jnp.zeros((tq, nselp), F32)
    lane = lax.broadcasted_iota(I32, (1, LANE), 1)
    picks = jnp.full((tq, LANE), -1, I32)
    for i in range(k_top):
        _, idx = _first_argmax(score, blk)
        hit = blk == idx
        sel = jnp.where(hit, 1.0, sel)
        picks = jnp.where(lane == i, idx, picks)
        score = jnp.where(hit, -jnp.inf, score)
    sel_ref[0] = sel
    idx_ref[0] = picks


def _nsa_cmp_call(q, kvc, *, tq, q_pos0, n_cmp, n_sel, nselp, precise=False):
    bsz, t, _ = q.shape
    n16 = kvc.shape[1]
    kern = functools.partial(_nsa_cmp_kernel, tq=tq, q_pos0=q_pos0, n_cmp=n_cmp, n_sel=n_sel,
                             k_top=min(NSA_SEL_TOPK, n_sel), precise=precise)
    return pl.pallas_call(
        kern,
        grid=(bsz, t // tq),
        in_specs=[pl.BlockSpec((1, tq, HEADS_W), lambda b, i: (b, i, 0)),
                  pl.BlockSpec((1, n16, LANE), lambda b, i: (b, 0, 0))],
        out_specs=[pl.BlockSpec((1, tq, HEADS_W), lambda b, i: (b, i, 0)),
                   pl.BlockSpec((1, tq, nselp), lambda b, i: (b, i, 0)),
                   pl.BlockSpec((1, tq, LANE), lambda b, i: (b, i, 0))],
        out_shape=[jax.ShapeDtypeStruct((bsz, t, HEADS_W), F32),
                   jax.ShapeDtypeStruct((bsz, t, nselp), F32),
                   jax.ShapeDtypeStruct((bsz, t, LANE), I32)],
        compiler_params=_cparams(("parallel", "parallel")),
        name="nsa_cmp",
    )(q, kvc)


ATTN_CHUNK = 128


def _attn_pairs(nq, tq, tk, window):
    qi_l, ki_l, first, last = [], [], [], []
    for qi in range(nq):
        lo = max(qi * tq - (window - 1), 0) // tk if window else 0
        hi = (qi * tq + tq - 1) // tk
        for ki in range(lo, hi + 1):
            qi_l.append(qi)
            ki_l.append(ki)
            first.append(int(ki == lo))
            last.append(int(ki == hi))
    return tuple(np.asarray(a, np.int32) for a in (qi_l, ki_l, first, last))


def _mm_tn(a, b, precise):
    dims = (((0,), (0,)), ((), ()))
    dot = lambda x, y: lax.dot_general(x, y, dims, preferred_element_type=F32)
    if precise:
        ah, al = _split2(a)
        bh, bl = _split2(b)
        return dot(ah, bh) + dot(ah, bl) + dot(al, bh)
    return dot(a.astype(BF16), b.astype(BF16))


def _attn_kernel(qi_ref, ki_ref, first_ref, last_ref, *refs, cfg):
    tq, tk, kw = cfg['tq'], cfg['tk'], cfg['kw']
    refs = list(refs)
    q_ref, kv_ref = refs.pop(0), refs.pop(0)
    bm_ref = refs.pop(0) if cfg['bm_bs'] else None
    if cfg['bias']:
        cqt_ref, cumk_ref = refs.pop(0), refs.pop(0)
    o_ref, m_ref, l_ref, acc_ref, qt_ref = refs[:5]
    bmt_ref = refs[5] if bm_ref is not None else None
    step = pl.program_id(1)
    qi, ki = qi_ref[step], ki_ref[step]

    @pl.when(first_ref[step] == 1)
    def _():
        m_ref[...] = jnp.full(m_ref.shape, NEG, F32)
        l_ref[...] = jnp.zeros(l_ref.shape, F32)
        acc_ref[...] = jnp.zeros(acc_ref.shape, F32)
        qt = q_ref[0].astype(F32).T
        zero = jnp.zeros((HEAD_DIM, tq), F32)
        for h in range(N_HEADS):
            qh = qt[h * HEAD_DIM:(h + 1) * HEAD_DIM]
            upper = kw == HEAD_DIM or h % 2 == 0
            qpad = jnp.concatenate([qh, zero] if upper else [zero, qh], axis=0)
            hi = qpad.astype(BF16)
            qt_ref[0, h] = hi
            if cfg['mode'] != 'fast':
                qt_ref[1, h] = (qpad - hi.astype(F32)).astype(BF16)
        if bm_ref is not None:
            bmt_ref[...] = bm_ref[0].T.astype(BF16)

    def compute(precise):
        qpos = qi * tq + lax.broadcasted_iota(I32, (1, tq), 1)
        hs = cfg['bm_hs']
        n_chunks = tk // ATTN_CHUNK
        heads = range(N_HEADS)
        rows, logits = [], []
        for c in range(n_chunks):
            rc = kv_ref[0, c * ATTN_CHUNK:(c + 1) * ATTN_CHUNK, :]
            rows.append(rc)
            kpos = ki * tk + c * ATTN_CHUNK + lax.broadcasted_iota(I32, (ATTN_CHUNK, 1), 0)
            mask = kpos <= qpos
            if cfg['window']:
                mask = jnp.logical_and(mask, qpos - kpos < cfg['window'])
            neg = jnp.where(mask, 0.0, NEG)
            if bm_ref is not None:
                j = lax.broadcasted_iota(I32, (1, hs if hs else bmt_ref.shape[0]), 1)
                expand = jnp.where(j == kpos // cfg['bm_bs'], 1.0, 0.0).astype(BF16)
                if not hs:
                    neg = neg + (_dot(expand, bmt_ref[...]) - 1.0) * (-NEG)
            per_head = []
            for h in heads:
                pair = 0 if kw == HEAD_DIM else h // 2
                kp = rc[:, pair * LANE:(pair + 1) * LANE]
                if precise:
                    kh, kl = _split2(kp)
                    s = _dot(kh, qt_ref[0, h]) + _dot(kh, qt_ref[1, h]) + _dot(kl, qt_ref[0, h])
                else:
                    s = _dot(kp.astype(BF16), qt_ref[0, h])
                s = s + neg
                if cfg['bias']:
                    s = s + (cqt_ref[0][h:h + 1, :]
                             - cumk_ref[0, c * ATTN_CHUNK:(c + 1) * ATTN_CHUNK, h:h + 1])
                if bm_ref is not None and hs:
                    s = s + (_dot(expand, bmt_ref[h * hs:(h + 1) * hs, :]) - 1.0) * (-NEG)
                per_head.append(s)
            logits.append(per_head)
        m = [m_ref[h] for h in heads]
        l = [l_ref[h] for h in heads]
        acc = [acc_ref[h] for h in heads]
        for c in range(n_chunks):
            probs, alphas = [], []
            for h in heads:
                s = logits[c][h]
                m_new = jnp.maximum(m[h], jnp.max(s, axis=0, keepdims=True))
                p = jnp.exp(s - m_new[0:1, :])
                alpha = jnp.exp(m[h] - m_new)
                l[h] = alpha * l[h] + jnp.sum(p, axis=0, keepdims=True)
                m[h] = m_new
                probs.append(p)
                alphas.append(alpha)
            for h in heads:
                ko = h * HEAD_DIM if kw > HEAD_DIM else 0
                v = rows[c][:, kw + ko:kw + ko + HEAD_DIM]
                acc[h] = alphas[h][0:1, :] * acc[h] + _mm_tn(v, probs[h], precise)
        for h in heads:
            m_ref[h] = m[h]
            l_ref[h] = l[h]
            acc_ref[h] = acc[h]

    _by_tile(cfg['mode'], qi == cfg['nq'] - 1, compute)

    @pl.when(last_ref[step] == 1)
    def _():
        out_t = jnp.concatenate(
            [acc_ref[h] / jnp.maximum(l_ref[h][0:1, :], 1e-30) for h in range(N_HEADS)], axis=0)
        o_ref[0] = out_t.T


def _attn_call(q, kv, *, tq, tk, window=0, bm=None, bm_bs=0, bm_hs=0, cq=None, cum_t=None,
               mode='fast', name="attn"):
    bsz, t, _ = q.shape
    kw = kv.shape[-1] // 2
    assert t % tq == 0 and t % tk == 0 and tk % ATTN_CHUNK == 0
    tabs = _attn_pairs(t // tq, tq, tk, window)
    cfg = dict(tq=tq, tk=tk, kw=kw, bm_bs=bm_bs, bm_hs=bm_hs, bias=cq is not None, window=window,
               mode=mode, nq=t // tq)
    q_map = lambda b, s, qi, ki, fi, la: (b, qi[s], 0)
    k_map = lambda b, s, qi, ki, fi, la: (b, ki[s], 0)
    args = [q, kv]
    specs = [pl.BlockSpec((1, tq, HEADS_W), q_map), pl.BlockSpec((1, tk, 2 * kw), k_map)]
    scratch = [pltpu.VMEM((N_HEADS, SUB, tq), F32), pltpu.VMEM((N_HEADS, SUB, tq), F32),
               pltpu.VMEM((N_HEADS, HEAD_DIM, tq), F32),
               pltpu.VMEM((1 if mode == 'fast' else 2, N_HEADS, LANE, tq), BF16)]
    if bm is not None:
        args.append(bm)
        specs.append(pl.BlockSpec((1, tq, bm.shape[2]), q_map))
        scratch.append(pltpu.VMEM((bm.shape[2], tq), BF16))
    if cq is not None:
        args += [cum_t, cq]
        specs += [pl.BlockSpec((1, SUB, tq), lambda b, s, qi, ki, fi, la: (b, 0, qi[s])),
                  pl.BlockSpec((1, tk, LANE), k_map)]
    return pl.pallas_call(
        functools.partial(_attn_kernel, cfg=cfg),
        grid_spec=pltpu.PrefetchScalarGridSpec(
            num_scalar_prefetch=4, grid=(bsz, tabs[0].shape[0]), in_specs=specs,
            out_specs=pl.BlockSpec((1, tq, HEADS_W), q_map), scratch_shapes=scratch),
        out_shape=jax.ShapeDtypeStruct((bsz, t, HEADS_W), F32),
        compiler_params=_cparams(("parallel", "arbitrary")), name=name,
    )(*tabs, *args)


def _col_from_row(row, n):
    eye = lax.broadcasted_iota(I32, (n, n), 0) == lax.broadcasted_iota(I32, (n, n), 1)
    return jnp.sum(jnp.where(eye, jnp.broadcast_to(row, (n, n)), 0.0), axis=1, keepdims=True)


def _row_from_col(col, n):
    eye = lax.broadcasted_iota(I32, (n, n), 0) == lax.broadcasted_iota(I32, (n, n), 1)
    return jnp.sum(jnp.where(eye, jnp.broadcast_to(col, (n, n)), 0.0), axis=0, keepdims=True)


def _sublane_total(x):
    for s in (4, 2, 1):
        x = x + pltpu.roll(x, s, 0)
    return x


def _query_page_logits(qb, kt):
    prod = qb * kt
    return _sublane_total(prod.reshape(HEAD_DIM // SUB, SUB, prod.shape[1]).sum(axis=0))


def _kt_attn_kernel(pt_ref, ft_ref, hd_ref, q_ref, *refs, cfg):
    g, nk, hk, pw, r, lo = cfg['g'], cfg['nk'], cfg['hk'], cfg['pw'], cfg['r'], cfg['lo']
    bias, one_head = cfg['bias'], cfg['one_head']
    refs = list(refs)
    kv_refs = [refs.pop(0) for _ in range(g)]
    self_ref = refs.pop(0)
    if bias:
        lf_refs = [refs.pop(0) for _ in range(g)]
        lfnew_ref = refs.pop(0)
    o_ref, m_ref, l_ref, acc_ref, qb_ref = refs[:5]
    carry_ref = refs[5] if bias else None
    b, ki = pl.program_id(0), pl.program_id(1)
    sub = lax.broadcasted_iota(I32, (SUB, pw), 0)
    lane = lax.broadcasted_iota(I32, (SUB, pw), 1)
    kv_head = lambda h: h if hk == N_HEADS else 0

    @pl.when(ki == 0)
    def _():
        m_ref[...] = jnp.full(m_ref.shape, NEG, F32)
        l_ref[...] = jnp.zeros(l_ref.shape, F32)
        acc_ref[...] = jnp.zeros(acc_ref.shape, F32)
        for h in range(N_HEADS):
            qcol = _col_from_row(q_ref[0][0:1, h * HEAD_DIM:(h + 1) * HEAD_DIM], HEAD_DIM)
            qb_ref[h] = jnp.broadcast_to(qcol, (HEAD_DIM, pw))
        if bias:
            row8 = jnp.broadcast_to(lfnew_ref[0][0:1, :], (SUB, LANE))
            e8 = lax.broadcasted_iota(I32, (SUB, LANE), 0) == lax.broadcasted_iota(I32, (SUB, LANE), 1)
            col8 = jnp.sum(jnp.where(e8, row8, 0.0), axis=1, keepdims=True)
            carry_ref[...] = jnp.broadcast_to(col8, carry_ref.shape)

    if bias:
        carry = carry_ref[...]
        pad = jnp.zeros((SUB - N_HEADS, pw), F32)
        stack = jnp.concatenate([x for i in range(g) for x in (lf_refs[i][0, 0], pad)], axis=0)
        upper = (lax.broadcasted_iota(I32, (pw, pw), 0) <= lax.broadcasted_iota(I32, (pw, pw), 1))
        prefix = _dot_exact(stack, jnp.where(upper, 1.0, 0.0).astype(BF16))
    logits, masks = [], []
    for i in range(g):
        j = ki * g + i
        blk, hd = ft_ref[b, j], hd_ref[b, j]
        lane0 = (blk * r) % pw
        ok = jnp.logical_and(jnp.logical_and(lane >= lane0, lane < lane0 + r),
                             blk * r + (lane - lane0) >= lo)
        s = jnp.full((SUB, pw), NEG, F32)
        if one_head:
            hc = jnp.minimum(hd, N_HEADS - 1)
            s = jnp.where(sub == hc, _query_page_logits(qb_ref[hc], kv_refs[i][0, 0, 0, hc]), s)
            ok = jnp.logical_and(ok, jnp.logical_and(sub == hd, hd < N_HEADS))
        else:
            for h in range(N_HEADS):
                s = jnp.where(sub == h, _query_page_logits(qb_ref[h], kv_refs[i][0, 0, 0, kv_head(h)]), s)
            ok = jnp.logical_and(ok, jnp.logical_and(sub < N_HEADS, hd < N_HEADS))
        if bias:
            pre = prefix[i * SUB:(i + 1) * SUB]
            tot = jnp.broadcast_to(pre[:, pw - 1:pw], (SUB, pw))
            s = s + (carry + (tot - pre))
            carry = carry + tot
        logits.append(jnp.where(ok, s, NEG))
        masks.append(ok)
    if bias:
        carry_ref[...] = carry

    m_old = m_ref[...]
    m_new = m_old
    for s in logits:
        m_new = jnp.maximum(m_new, jnp.broadcast_to(jnp.max(s, axis=1, keepdims=True), (SUB, pw)))
    alpha = jnp.exp(m_old - m_new)
    l_new = alpha * l_ref[...]
    probs = []
    for s, ok in zip(logits, masks):
        p = jnp.where(ok, jnp.exp(s - m_new), 0.0)
        l_new = l_new + jnp.broadcast_to(jnp.sum(p, axis=1, keepdims=True), (SUB, pw))
        probs.append(p)
    m_ref[...] = m_new
    l_ref[...] = l_new
    for h in range(N_HEADS):
        acc = alpha[h:h + 1, :] * acc_ref[h]
        if not one_head:
            for i in range(g):
                acc = acc + probs[i][h:h + 1, :] * kv_refs[i][0, 0, 1, kv_head(h)]
        acc_ref[h] = acc
    if one_head:
        for i in range(g):
            hc = jnp.minimum(hd_ref[b, ki * g + i], N_HEADS - 1)
            prow = _sublane_total(jnp.where(sub == hc, probs[i], 0.0))[0:1, :]
            acc_ref[hc] = acc_ref[hc] + prow * kv_refs[i][0, 0, 1, hc]

    @pl.when(ki == nk - 1)
    def _():
        w = hk * HEAD_DIM
        own = self_ref[0]
        sub1 = lax.broadcasted_iota(I32, (SUB, 1), 0)
        s_own = jnp.full((SUB, 1), NEG, F32)
        for h in range(N_HEADS):
            kh = own[0:1, kv_head(h) * HEAD_DIM:(kv_head(h) + 1) * HEAD_DIM]
            qh = q_ref[0][0:1, h * HEAD_DIM:(h + 1) * HEAD_DIM]
            s_own = jnp.where(sub1 == h, jnp.sum(qh * kh, axis=1, keepdims=True), s_own)
        m_fin = m_ref[:, 0:1]
        m_tot = jnp.maximum(m_fin, s_own)
        a_fin = jnp.exp(m_fin - m_tot)
        p_own = jnp.exp(s_own - m_tot)
        l_tot = a_fin * l_ref[:, 0:1] + p_own
        outs = []
        for h in range(N_HEADS):
            vcol = _col_from_row(own[0:1, w + kv_head(h) * HEAD_DIM:w + (kv_head(h) + 1) * HEAD_DIM],
                                 HEAD_DIM)
            tot = a_fin[h:h + 1, :] * jnp.sum(acc_ref[h], axis=1, keepdims=True) + p_own[h:h + 1, :] * vcol
            outs.append(_row_from_col(tot / jnp.maximum(l_tot[h:h + 1, :], 1e-30), HEAD_DIM))
        o_ref[0] = jnp.broadcast_to(jnp.concatenate(outs, axis=1), (SUB, HEADS_W))


def _kt_attn_call(q8, cache_t, layer, page_table, fetch, heads, self8, *, r, g, lo=0, one_head=False,
                  logf_t=None, logf_new=None, name="kt_attn"):
    bsz = q8.shape[0]
    _, _, _, hk, _, pw = cache_t.shape
    nf = fetch.shape[1]
    assert nf % g == 0
    nk = nf // g
    n_pages = page_table.shape[1]
    bias = logf_t is not None
    cfg = dict(g=g, nk=nk, hk=hk, pw=pw, r=r, lo=lo, bias=bias, one_head=one_head)

    pages = jnp.take_along_axis(page_table, jnp.clip(fetch * r // pw, 0, n_pages - 1), axis=1)
    kv_map = lambda i: (lambda b, ki, pg, ft, hd: (layer, pg[b, ki * g + i], 0, 0, 0, 0))
    lf_map = lambda i: (lambda b, ki, pg, ft, hd: (layer, pg[b, ki * g + i], 0, 0))
    fixed = lambda shape: pl.BlockSpec(shape, lambda b, ki, pt, ft, hd: (b, 0, 0))
    args = [q8] + [cache_t] * g + [self8]
    specs = ([fixed((1, SUB, HEADS_W))]
             + [pl.BlockSpec((1, 1, 2, hk, HEAD_DIM, pw), kv_map(i)) for i in range(g)]
             + [fixed((1, SUB, self8.shape[2]))])
    scratch = [pltpu.VMEM((SUB, pw), F32), pltpu.VMEM((SUB, pw), F32),
               pltpu.VMEM((N_HEADS, HEAD_DIM, pw), F32), pltpu.VMEM((N_HEADS, HEAD_DIM, pw), F32)]
    if bias:
        args += [logf_t] * g + [logf_new]
        specs += [pl.BlockSpec((1, 1, N_HEADS, pw), lf_map(i)) for i in range(g)] + [fixed((1, SUB, LANE))]
        scratch.append(pltpu.VMEM((SUB, pw), F32))
    return pl.pallas_call(
        functools.partial(_kt_attn_kernel, cfg=cfg),
        grid_spec=pltpu.PrefetchScalarGridSpec(
            num_scalar_prefetch=3, grid=(bsz, nk), in_specs=specs,
            out_specs=fixed((1, SUB, HEADS_W)), scratch_shapes=scratch),
        out_shape=jax.ShapeDtypeStruct((bsz, SUB, HEADS_W), F32),
        compiler_params=_cparams(("parallel", "arbitrary")),
        name=name,
    )(pages, fetch, heads, *args)


def _kt_gate_kernel(pt_ref, q_ref, *refs, g, nk, k_top, n_blk):
    k_refs = refs[:g]
    idx_ref, qb_ref, gate_ref = refs[g:]
    ki = pl.program_id(1)
    sub = lax.broadcasted_iota(I32, (SUB, LANE), 0)
    lane = lax.broadcasted_iota(I32, (SUB, LANE), 1)
    ppb = MOBA_BLOCK // PAGE

    @pl.when(ki == 0)
    def _():
        gate_ref[...] = jnp.zeros(gate_ref.shape, F32)
        for h in range(N_HEADS):
            qcol = _col_from_row(q_ref[0][0:1, h * HEAD_DIM:(h + 1) * HEAD_DIM], HEAD_DIM)
            qb_ref[h] = jnp.broadcast_to(qcol, (HEAD_DIM, LANE))

    gate = gate_ref[...]
    for i in range(g):
        s = jnp.zeros((SUB, LANE), F32)
        for h in range(N_HEADS):
            s = jnp.where(sub == h, _query_page_logits(qb_ref[h], k_refs[i][0, 0, 0, h]), s)
        tot = jnp.sum(s, axis=1, keepdims=True) * (1.0 / MOBA_BLOCK)
        gate = gate + jnp.where(lane == (ki * g + i) // ppb, tot, 0.0)
    gate_ref[...] = gate

    @pl.when(ki == nk - 1)
    def _():
        score = jnp.where(lane < n_blk, gate, NEG)
        picks = jnp.full((SUB, LANE), -1, I32)
        for i in range(k_top):
            m, idx = _first_argmax(score, lane)
            hit = lane == idx
            picks = jnp.where(jnp.logical_and(lane == i, m > 0.5 * NEG), idx, picks)
            score = jnp.where(hit, -jnp.inf, score)
        idx_ref[0] = picks


def _kt_gate_call(q8, cache_t, layer, page_table, *, g, k_top):
    bsz = q8.shape[0]
    n_pages = page_table.shape[1]
    assert n_pages % g == 0 and n_pages * PAGE % MOBA_BLOCK == 0
    nk = n_pages // g
    n_blk = n_pages * PAGE // MOBA_BLOCK
    assert n_blk <= LANE
    k_map = lambda i: (lambda b, ki, pt: (layer, pt[b, ki * g + i], 0, 0, 0, 0))
    return pl.pallas_call(
        functools.partial(_kt_gate_kernel, g=g, nk=nk, k_top=k_top, n_blk=n_blk),
        grid_spec=pltpu.PrefetchScalarGridSpec(
            num_scalar_prefetch=1, grid=(bsz, nk),
            in_specs=[pl.BlockSpec((1, SUB, HEADS_W), lambda b, ki, pt: (b, 0, 0))]
            + [pl.BlockSpec((1, 1, 1, N_HEADS, HEAD_DIM, PAGE), k_map(i)) for i in range(g)],
            out_specs=pl.BlockSpec((1, SUB, LANE), lambda b, ki, pt: (b, 0, 0)),
            scratch_shapes=[pltpu.VMEM((N_HEADS, HEAD_DIM, LANE), F32), pltpu.VMEM((SUB, LANE), F32)]),
        out_shape=jax.ShapeDtypeStruct((bsz, SUB, LANE), I32),
        compiler_params=_cparams(("parallel", "arbitrary")),
        name="moba_gate_s",
    )(page_table, q8, *([cache_t] * g))


def _kt_cmp_kernel(pt_ref, *refs, g, nk):
    kv_refs = refs[:g]
    wc_ref, pe_ref, w2_ref, g_ref, out_ref, rows_ref = refs[g:]
    ki = pl.program_id(1)
    for i in range(g):
        off = pl.multiple_of((ki * g + i) * PAGE, PAGE)
        rows_ref[pl.ds(off, PAGE), :] = kv_refs[i][0, 0].reshape(2 * HEAD_DIM, PAGE).T

    @pl.when(ki == nk - 1)
    def _():
        n16 = rows_ref.shape[0] // NSA_CMP_STRIDE
        hw = 2 * NSA_CMP_HIDDEN
        h = jnp.zeros((n16, 2 * hw), F32)
        c = jnp.zeros((SUB, 2 * hw), F32)
        for j in range(NSA_CMP_STRIDE // 2):
            pair = [rows_ref[pl.ds(2 * j + i, n16, stride=NSA_CMP_STRIDE), :] for i in range(2)]
            h = h + _mm(jnp.concatenate(pair, axis=1), wc_ref[j], True)
            c = c + _mm(pe_ref[j], wc_ref[j], True)
        c = c[0:1, :hw] + c[1:2, hw:]
        pre = h[:, :hw] + pltpu.roll(h[:, hw:], n16 - 1, 0) + c
        act = 0.5 * pre * (1.0 + jnp.tanh(math.sqrt(2.0 / math.pi) * (pre + 0.044715 * pre * pre * pre)))
        y = _mm(act, w2_ref[...], True)
        lane = lax.broadcasted_iota(I32, (1, LANE), 1)
        yn = y * lax.rsqrt(_head_sumsq(y, True) * (1.0 / HEAD_DIM) + EPS) * g_ref[...]
        out_ref[0] = jnp.where(lane < HEAD_DIM, yn, y)


def _kt_cmp_call(cache_t, layer, page_table, wab, pe2, w2, gkc, *, g):
    bsz, n_pages = page_table.shape
    assert n_pages % g == 0
    nk = n_pages // g
    n16 = n_pages * PAGE // NSA_CMP_STRIDE
    wc = wab.reshape(NSA_CMP_STRIDE // 2, 2 * LANE, wab.shape[1])
    pe = jnp.swapaxes(pe2[:2].reshape(2, NSA_CMP_STRIDE // 2, 2 * LANE), 0, 1)
    pe = jnp.pad(pe, ((0, 0), (0, SUB - 2), (0, 0)))
    full = lambda a: pl.BlockSpec(a.shape, lambda b, ki, pt: (0,) * a.ndim)
    kv_map = lambda i: (lambda b, ki, pt: (layer, pt[b, ki * g + i], 0, 0, 0))
    return pl.pallas_call(
        functools.partial(_kt_cmp_kernel, g=g, nk=nk),
        grid_spec=pltpu.PrefetchScalarGridSpec(
            num_scalar_prefetch=1, grid=(bsz, nk),
            in_specs=[pl.BlockSpec((1, 1, 2, HEAD_DIM, PAGE), kv_map(i)) for i in range(g)]
            + [full(wc), full(pe), full(w2), full(gkc)],
            out_specs=pl.BlockSpec((1, n16, LANE), lambda b, ki, pt: (b, 0, 0)),
            scratch_shapes=[pltpu.VMEM((n_pages * PAGE, LANE), F32)]),
        out_shape=jax.ShapeDtypeStruct((bsz, n16, LANE), F32),
        compiler_params=_cparams(("parallel", "arbitrary")),
        name="cmp_mlp_s",
    )(page_table, *([cache_t] * g), wc, pe, w2, gkc)


MOBA_BLOCKS_PER_STEP = 8


def _kmean_kernel(kv_ref, out_ref):
    row = lax.broadcasted_iota(I32, (MOBA_BLOCKS_PER_STEP, 1), 0)
    acc = jnp.zeros((MOBA_BLOCKS_PER_STEP, HEADS_W), F32)
    for j in range(MOBA_BLOCKS_PER_STEP):
        s = jnp.sum(kv_ref[0, j * MOBA_BLOCK:(j + 1) * MOBA_BLOCK, :], axis=0, keepdims=True)
        acc = acc + jnp.where(row == j, s, 0.0)
    out_ref[0] = acc * (1.0 / MOBA_BLOCK)


def _kmean_call(kv):
    bsz, tk, _ = kv.shape
    span = MOBA_BLOCKS_PER_STEP * MOBA_BLOCK
    assert tk % span == 0
    return pl.pallas_call(
        _kmean_kernel, grid=(bsz, tk // span),
        in_specs=[pl.BlockSpec((1, span, HEADS_W), lambda b, j: (b, j, 0))],
        out_specs=pl.BlockSpec((1, MOBA_BLOCKS_PER_STEP, HEADS_W), lambda b, j: (b, j, 0)),
        out_shape=jax.ShapeDtypeStruct((bsz, tk // MOBA_BLOCK, HEADS_W), F32),
        compiler_params=_cparams(("parallel", "parallel")), name="moba_kmean")(kv)


def _moba_gate_kernel(q_ref, km_ref, bm_ref, *, tq, k_top, precise):
    qi = pl.program_id(1)
    qpos = qi * tq + lax.broadcasted_iota(I32, (tq, 1), 0)
    cur = qpos // MOBA_BLOCK
    blk = lax.broadcasted_iota(I32, (1, LANE), 1)
    valid = blk < cur
    q = q_ref[0]
    km = km_ref[0]
    for h in range(N_HEADS):
        kmh = km[:, h * HEAD_DIM:(h + 1) * HEAD_DIM]
        qh = q[:, h * HEAD_DIM:(h + 1) * HEAD_DIM]
        if precise:
            gate = _mm_nt(qh, kmh, True)
        else:
            hi, lo = _split2(kmh)
            gate = _dot_nt(qh, hi) + _dot_nt(qh, lo)
        score = jnp.where(valid, gate, NEG)
        sel = jnp.where(blk == cur, 1.0, 0.0)
        for _ in range(k_top):
            m, idx = _first_argmax(score, blk)
            hit = blk == idx
            sel = jnp.where(jnp.logical_and(hit, m > 0.5 * NEG), 1.0, sel)
            score = jnp.where(hit, -jnp.inf, score)
        bm_ref[0, :, h * LANE:(h + 1) * LANE] = sel


def _moba_gate_call(q, kmean, *, tq, k_top, precise=False):
    bsz, t, _ = q.shape
    return pl.pallas_call(
        functools.partial(_moba_gate_kernel, tq=tq, k_top=k_top, precise=precise),
        grid=(bsz, t // tq),
        in_specs=[pl.BlockSpec((1, tq, HEADS_W), lambda b, i: (b, i, 0)),
                  pl.BlockSpec((1, LANE, HEADS_W), lambda b, i: (b, 0, 0))],
        out_specs=pl.BlockSpec((1, tq, N_HEADS * LANE), lambda b, i: (b, i, 0)),
        out_shape=jax.ShapeDtypeStruct((bsz, t, N_HEADS * LANE), F32),
        compiler_params=_cparams(("parallel", "parallel")),
        name="moba_gate",
    )(q, kmean)


CUM_CHUNK = 512


def _cum_kernel(l_ref, o_ref, carry_ref):
    @pl.when(pl.program_id(1) == 0)
    def _():
        carry_ref[...] = jnp.zeros(carry_ref.shape, F32)

    r = lax.broadcasted_iota(I32, (CUM_CHUNK, CUM_CHUNK), 0)
    c = lax.broadcasted_iota(I32, (CUM_CHUNK, CUM_CHUNK), 1)
    u = jnp.where(r <= c, 1.0, 0.0).astype(BF16)
    h1, h2, h3 = _split3(l_ref[0])
    cum = _dot(h1, u) + _dot(h2, u) + _dot(h3, u) + carry_ref[:, 0:1]
    o_ref[0] = cum
    carry_ref[...] = jnp.broadcast_to(cum[:, CUM_CHUNK - 1:CUM_CHUNK], carry_ref.shape)


def _cum_call(logf_t):
    bsz, rows, tk = logf_t.shape
    assert rows == SUB and tk % CUM_CHUNK == 0
    return pl.pallas_call(
        _cum_kernel,
        grid=(bsz, tk // CUM_CHUNK),
        in_specs=[pl.BlockSpec((1, SUB, CUM_CHUNK), lambda b, j: (b, 0, j))],
        out_specs=pl.BlockSpec((1, SUB, CUM_CHUNK), lambda b, j: (b, 0, j)),
        out_shape=jax.ShapeDtypeStruct(logf_t.shape, F32),
        scratch_shapes=[pltpu.VMEM((SUB, LANE), F32)],
        compiler_params=_cparams(("parallel", "arbitrary")),
        name="fox_cum",
    )(logf_t)


def _group_norm_gate(y, z, norm):
    y = y * _silu(z)
    gw = SSD_INNER // SSD_GROUPS
    outs = []
    for g in range(SSD_GROUPS):
        yg = y[:, g * gw:(g + 1) * gw]
        ms = jnp.mean(yg * yg, axis=-1, keepdims=True)
        outs.append(yg * lax.rsqrt(ms + EPS) * norm[:, g * gw:(g + 1) * gw])
    return jnp.concatenate(outs, axis=-1)


def _ssd_kernel(xbc_ref, z_ref, dt_ref, convw_ref, convb_ref, hp_ref, norm_ref, y_ref, state_ref,
                ext_ref, h_ref, *, nc, precise):
    q = SSD_CHUNK
    ci = pl.program_id(1)

    @pl.when(ci == 0)
    def _():
        ext_ref[0:SUB, :] = jnp.zeros((SUB, SSD_CONV_DIM), F32)
        h_ref[...] = jnp.zeros(h_ref.shape, F32)

    u = xbc_ref[0]
    ext_ref[SUB:SUB + q, :] = u
    acc = jnp.broadcast_to(convb_ref[...], (q, SSD_CONV_DIM))
    for i in range(SSD_CONV):
        k = SSD_CONV - 1 - i
        acc = acc + ext_ref[SUB - k:SUB - k + q, :] * convw_ref[i:i + 1, :]
    ext_ref[0:SUB, :] = u[q - SUB:, :]
    xbc = _silu(acc)
    xs = xbc[:, :SSD_INNER]
    dt = dt_ref[0]
    a = -jnp.exp(hp_ref[0:1, :])
    d_skip = hp_ref[1:2, :]
    row = lax.broadcasted_iota(I32, (q, 1), 0)
    cum = dt * a
    s = 1
    while s < q:
        cum = cum + jnp.where(row >= s, pltpu.roll(cum, s, 0), 0.0)
        s *= 2
    cum_t = cum.T
    tri = lax.broadcasted_iota(I32, (q, q), 0) >= lax.broadcasted_iota(I32, (q, q), 1)
    ys, xdd = [], []
    cbs = []
    for g in range(SSD_GROUPS):
        bm = xbc[:, SSD_INNER + g * SSD_STATE:SSD_INNER + (g + 1) * SSD_STATE]
        cm = xbc[:, SSD_INNER + SSD_BC + g * SSD_STATE:SSD_INNER + SSD_BC + (g + 1) * SSD_STATE]
        cbs.append((bm, cm, _mm_nt(cm, bm, precise)))
    rep = SSD_HEADS // SSD_GROUPS
    for h in range(SSD_HEADS):
        bm, cm, cb = cbs[h // rep]
        ch = cum[:, h:h + 1]
        lmat = jnp.where(tri, jnp.exp(ch - cum_t[h:h + 1, :]), 0.0)
        xh = xs[:, h * SSD_HEAD_DIM:(h + 1) * SSD_HEAD_DIM]
        xdt = xh * dt[:, h:h + 1]
        y = _mm(cb * lmat, xdt, precise)
        y = y + jnp.exp(ch) * _mm_nt(cm, h_ref[h], precise)
        ys.append(y + d_skip[:, h:h + 1] * xh)
        xdd.append(xdt * jnp.exp(cum[q - 1:q, h:h + 1] - ch))
    xdd_t = jnp.concatenate(xdd, axis=-1).T
    for h in range(SSD_HEADS):
        bm = cbs[h // rep][0]
        upd = _mm(xdd_t[h * SSD_HEAD_DIM:(h + 1) * SSD_HEAD_DIM, :], bm, precise)
        h_ref[h] = jnp.exp(cum[q - 1:q, h:h + 1]) * h_ref[h] + upd
    y_ref[0] = _group_norm_gate(jnp.concatenate(ys, axis=-1), z_ref[0], norm_ref[...])

    @pl.when(ci == nc - 1)
    def _():
        state_ref[0] = h_ref[...]


def _ssd_call(xbc, z, dt, convw, convb, hp, norm, precise=False):
    bsz, t, _ = xbc.shape
    nc = t // SSD_CHUNK
    blk = lambda w: pl.BlockSpec((1, SSD_CHUNK, w), lambda b, c: (b, c, 0))
    full = lambda a: pl.BlockSpec(a.shape, lambda b, c: (0,) * a.ndim)
    return pl.pallas_call(
        functools.partial(_ssd_kernel, nc=nc, precise=precise),
        grid=(bsz, nc),
        in_specs=[blk(SSD_CONV_DIM), blk(SSD_INNER), blk(LANE), full(convw), full(convb), full(hp),
                  full(norm)],
        out_specs=[blk(SSD_INNER),
                   pl.BlockSpec((1, SSD_HEADS, SSD_HEAD_DIM, SSD_STATE), lambda b, c: (b, 0, 0, 0))],
        out_shape=[jax.ShapeDtypeStruct((bsz, t, SSD_INNER), F32),
                   jax.ShapeDtypeStruct((bsz, SSD_HEADS, SSD_HEAD_DIM, SSD_STATE), F32)],
        scratch_shapes=[pltpu.VMEM((SUB + SSD_CHUNK, SSD_CONV_DIM), F32),
                        pltpu.VMEM((SSD_HEADS, SSD_HEAD_DIM, SSD_STATE), F32)],
        compiler_params=_cparams(("parallel", "arbitrary")),
        name="ssd_scan",
    )(xbc, z, dt, convw, convb, hp, norm)


def _ssd_step_kernel(full_ref, z_ref, dt_ref, st_ref, w8_ref, convb_ref, hp_ref, norm_ref,
                     y_ref, state_ref):
    acc = convb_ref[...] + jnp.sum(full_ref[0] * w8_ref[...], axis=0, keepdims=True)
    xbc = _silu(acc)
    xs = xbc[:, :SSD_INNER]
    dt = dt_ref[0]
    a = -jnp.exp(hp_ref[0:1, :])
    d_skip = hp_ref[1:2, :]
    n = SSD_HEAD_DIM
    eye = lax.broadcasted_iota(I32, (n, n), 0) == lax.broadcasted_iota(I32, (n, n), 1)
    rep = SSD_HEADS // SSD_GROUPS
    ys = []
    for h in range(SSD_HEADS):
        g = h // rep
        bm = xbc[:, SSD_INNER + g * SSD_STATE:SSD_INNER + (g + 1) * SSD_STATE]
        cm = xbc[:, SSD_INNER + SSD_BC + g * SSD_STATE:SSD_INNER + SSD_BC + (g + 1) * SSD_STATE]
        dth = dt[:, h:h + 1]
        xh = xs[:, h * n:(h + 1) * n]
        xcol = jnp.sum(jnp.where(eye, jnp.broadcast_to(xh * dth, (n, n)), 0.0), axis=1, keepdims=True)
        hn = jnp.exp(dth * a[:, h:h + 1]) * st_ref[0, h] + xcol * bm
        state_ref[0, h] = hn
        ycol = jnp.sum(hn * cm, axis=1, keepdims=True)
        yrow = jnp.sum(jnp.where(eye, jnp.broadcast_to(ycol, (n, n)), 0.0), axis=0, keepdims=True)
        ys.append(yrow + d_skip[:, h:h + 1] * xh)
    y_ref[0] = _group_norm_gate(jnp.concatenate(ys, axis=-1), z_ref[0], norm_ref[...])


def _ssd_step_call(full8, z, dt, state, w8, convb, hp, norm):
    bsz = full8.shape[0]
    one = lambda w: pl.BlockSpec((1, 1, w), lambda b: (b, 0, 0))
    full = lambda a: pl.BlockSpec(a.shape, lambda b: (0,) * a.ndim)
    st = pl.BlockSpec((1, SSD_HEADS, SSD_HEAD_DIM, SSD_STATE), lambda b: (b, 0, 0, 0))
    return pl.pallas_call(
        _ssd_step_kernel,
        grid=(bsz,),
        in_specs=[pl.BlockSpec((1, SUB, SSD_CONV_DIM), lambda b: (b, 0, 0)), one(SSD_INNER), one(LANE),
                  st, full(w8), full(convb), full(hp), full(norm)],
        out_specs=[one(SSD_INNER), st],
        out_shape=[jax.ShapeDtypeStruct((bsz, 1, SSD_INNER), F32),
                   jax.ShapeDtypeStruct(state.shape, F32)],
        compiler_params=_cparams(("parallel",)),
        name="ssd_step",
    )(full8, z, dt, state, w8, convb, hp, norm)


R_GROUP0 = MOE_EXPERTS


def _post_kernel(*refs, mode, nt):
    is_tail = (pl.program_id(0) % nt) == nt - 1
    _by_tile(mode, is_tail, lambda precise: _post_body(*refs, precise=precise))


def _post_body(x_ref, gmix_ref, wg_ref, small_ref, ocmp_ref, oslc_ref, owin_ref, omoba_ref,
               ofox_ref, ossd_ref, wbr_ref, wout_ref, gffn_ref, wr_ref, br_ref,
               x1_ref, xn2_ref, comb_ref, *, precise):
    x = x_ref[...]
    xn = x * lax.rsqrt(jnp.mean(x * x, axis=-1, keepdims=True) + EPS) * gmix_ref[...]
    xn = xn if precise else xn.astype(BF16)
    every = slice(None)
    r = lax.broadcasted_iota(I32, (LANE, 3 * HEADS_W), 0)
    c = lax.broadcasted_iota(I32, (LANE, 3 * HEADS_W), 1)
    pick = r == SM_NSAG + 3 * ((c % HEADS_W) // HEAD_DIM) + c // HEADS_W
    pick = jnp.where(pick, 1.0, 0.0).astype(BF16)
    gexp = _dot_exact(_sigmoid(small_ref[...]), pick)
    o_nsa = (gexp[:, :HEADS_W] * ocmp_ref[...] + gexp[:, HEADS_W:2 * HEADS_W] * oslc_ref[...]
             + gexp[:, 2 * HEADS_W:] * owin_ref[...])
    merged = jnp.zeros(x.shape, F32)
    off = 0
    for b, o in enumerate((o_nsa, omoba_ref[...], ofox_ref[...], ossd_ref[...])):
        w = o.shape[1]
        gate = _sigmoid(_wmm(xn, wg_ref, (every, slice(b * D_MODEL, (b + 1) * D_MODEL)), precise))
        merged = merged + gate * _wmm(o, wbr_ref, (slice(off, off + w), every), precise)
        off += w
    x1 = x + _wmm(merged, wout_ref, (every, every), precise)
    x1_ref[...] = x1
    xn2 = x1 * lax.rsqrt(jnp.mean(x1 * x1, axis=-1, keepdims=True) + EPS) * gffn_ref[...]
    xn2_ref[...] = xn2.astype(xn2_ref.dtype)
    logits = _wmm(xn2, wr_ref, (every, every), True) + br_ref[...]
    lane = lax.broadcasted_iota(I32, (1, LANE), 1)
    is_grp = jnp.logical_and(lane >= R_GROUP0, lane < R_GROUP0 + MOE_GROUPS)
    gmax, gidx = _first_argmax(jnp.where(is_grp, logits, -jnp.inf), lane)
    g_w = 1.0 / jnp.sum(jnp.where(is_grp, jnp.exp(logits - gmax), 0.0), axis=-1, keepdims=True)
    in_grp = lane // MOE_PER_GROUP == gidx - R_GROUP0
    e1 = jnp.where(in_grp, logits, -jnp.inf)
    v1, i1 = _first_argmax(e1, lane)
    e2 = jnp.where(lane == i1, -jnp.inf, e1)
    v2, i2 = _first_argmax(e2, lane)
    t = jnp.exp(v2 - v1)
    w1 = 1.0 / (1.0 + t)
    comb_ref[...] = (jnp.where(lane == i1, w1 * g_w, 0.0) + jnp.where(lane == i2, t * w1 * g_w, 0.0))


def _post_call(x2, gmix, wg, small, o_cmp, o_slc, o_win, o_moba, o_fox, o_ssd, wbr, wout, gffn,
               wr, br, tm, mode, nt):
    n = x2.shape[0]
    row = lambda a: pl.BlockSpec((tm, a.shape[1]), lambda i: (i, 0))
    full = lambda a: pl.BlockSpec(a.shape, lambda i: (0,) * a.ndim)
    big = (lambda a: pl.BlockSpec(a.shape, lambda i: (0,) * a.ndim, pipeline_mode=pl.Buffered(1))
           ) if mode != 'fast' else full
    rows = (x2, small, o_cmp, o_slc, o_win, o_moba, o_fox, o_ssd)
    return pl.pallas_call(
        functools.partial(_post_kernel, mode=mode, nt=nt),
        grid=(n // tm,),
        in_specs=[row(x2), full(gmix), big(wg)] + [row(a) for a in rows[1:]]
        + [big(wbr), big(wout), full(gffn), full(wr), full(br)],
        out_specs=[pl.BlockSpec((tm, D_MODEL), lambda i: (i, 0)),
                   pl.BlockSpec((tm, D_MODEL), lambda i: (i, 0)),
                   pl.BlockSpec((tm, LANE), lambda i: (i, 0))],
        out_shape=[jax.ShapeDtypeStruct((n, D_MODEL), F32),
                   jax.ShapeDtypeStruct((n, D_MODEL), BF16 if mode == 'fast' else F32),
                   jax.ShapeDtypeStruct((n, LANE), F32)],
        compiler_params=_cparams(("parallel",)),
        name="merge_out",
    )(x2, gmix, wg, small, o_cmp, o_slc, o_win, o_moba, o_fox, o_ssd, wbr, wout, gffn, wr, br)


def _moe_kernel(xn_ref, x1_ref, comb_ref, wg_ref, wu_ref, wd_ref, o_ref, acc_ref, *, precise):
    e = pl.program_id(1)

    @pl.when(e == 0)
    def _():
        acc_ref[...] = x1_ref[...]

    lane = lax.broadcasted_iota(I32, (1, LANE), 1)
    cw = jnp.sum(jnp.where(lane == e, comb_ref[...], 0.0), axis=-1, keepdims=True)
    xn = xn_ref[...]
    h = _silu(_mm(xn, wg_ref[0], precise)) * _mm(xn, wu_ref[0], precise)
    acc_ref[...] += cw * _mm(h, wd_ref[0], precise)

    @pl.when(e == MOE_EXPERTS - 1)
    def _():
        o_ref[...] = acc_ref[...]


def _moe_call(xn2, x1, comb, wg, wu, wd, tm, precise=False):
    n = xn2.shape[0]
    return pl.pallas_call(
        functools.partial(_moe_kernel, precise=precise),
        grid=(n // tm, MOE_EXPERTS),
        in_specs=[pl.BlockSpec((tm, D_MODEL), lambda i, e: (i, 0)),
                  pl.BlockSpec((tm, D_MODEL), lambda i, e: (i, 0)),
                  pl.BlockSpec((tm, LANE), lambda i, e: (i, 0)),
                  pl.BlockSpec((1, D_MODEL, MOE_HIDDEN), lambda i, e: (e, 0, 0)),
                  pl.BlockSpec((1, D_MODEL, MOE_HIDDEN), lambda i, e: (e, 0, 0)),
                  pl.BlockSpec((1, MOE_HIDDEN, D_MODEL), lambda i, e: (e, 0, 0))],
        out_specs=pl.BlockSpec((tm, D_MODEL), lambda i, e: (i, 0)),
        out_shape=jax.ShapeDtypeStruct((n, D_MODEL), F32),
        scratch_shapes=[pltpu.VMEM((tm, D_MODEL), F32)],
        compiler_params=_cparams(("parallel", "arbitrary")),
        name="moe_experts",
    )(xn2, x1, comb, wg, wu, wd)


def _layer_weights(prm):
    wp, wgate = _pack_w_in(prm['w_in'])
    wr = jnp.concatenate([prm['w_router_exp'], prm['w_router_grp'],
                          jnp.zeros((D_MODEL, LANE - MOE_EXPERTS - MOE_GROUPS), F32)], axis=1)
    br = jnp.concatenate([prm['b_router_exp'], prm['b_router_grp'],
                          jnp.zeros((LANE - MOE_EXPERTS - MOE_GROUPS,), F32)])[None, :]
    pad = lambda v: jnp.concatenate([v.astype(F32), jnp.zeros((LANE - v.shape[0],), F32)])
    hp = jnp.stack([pad(prm['ssd_a_log']), pad(prm['ssd_d'])] + [jnp.zeros((LANE,), F32)] * 6)
    w8 = jnp.concatenate([prm['ssd_conv_w'], jnp.zeros((SUB - SSD_CONV, SSD_CONV_DIM), F32)], axis=0)
    wbr = jnp.concatenate([prm['w_br_nsa'], prm['w_br_moba'], prm['w_br_fox'], prm['w_br_ssd']], axis=0)
    shared = dict(gmix=prm['norm_mix'][None, :], gffn=prm['norm_ffn'][None, :], gains=_proj_gains(prm),
                  biases=_proj_biases(prm), br=br, hp=hp, w8=w8, convw=prm['ssd_conv_w'],
                  convb=prm['ssd_conv_b'][None, :], norm=prm['ssd_norm'][None, :],
                  wr=_split_w(wr, True))
    big = lambda precise: dict(wp=_split_w(wp, precise), wgate=_split_w(wgate, precise),
                               wbr=_split_w(wbr, precise), wout=_split_w(prm['w_out'], precise))
    fast = dict(shared, **big(False), cmp=_cmp_weights(prm, BF16),
                weg=prm['w_exp_gate'].astype(BF16), weu=prm['w_exp_up'].astype(BF16),
                wed=prm['w_exp_down'].astype(BF16))
    exact = dict(shared, **big(True), cmp=_cmp_weights(prm, F32),
                 weg=prm['w_exp_gate'], weu=prm['w_exp_up'], wed=prm['w_exp_down'])
    return fast, exact


def _pad_rows(a, rows):
    return jnp.pad(a, ((0, 0), (0, rows - a.shape[1])) + ((0, 0),) * (a.ndim - 2))


def _heads_to_sublanes(logf, tk_pad):
    lt = jnp.swapaxes(logf, 1, 2)
    return jnp.pad(lt, ((0, 0), (0, SUB - lt.shape[1]), (0, tk_pad - lt.shape[2])))


def _layer_prompt(x, w, w_ffn, *, tm, tq, tk, mode):
    precise = mode != 'fast'
    bsz, t, _ = x.shape
    n = bsz * t
    x2 = x.reshape(n, D_MODEL)
    tabs = _rope_tables(jnp.arange(t, dtype=I32))
    (qn, cmp_rows, slc_rows, win_rows, qm, moba_rows, qf, fox_rows, z, xbc, small, dt) = _proj_call(
        x2, w['gmix'], w['wp'], w['gains'], tabs, w['biases'], tm, mode)
    b3 = lambda a: a.reshape(bsz, t, a.shape[-1])
    qn, qm, qf = b3(qn), b3(qm), b3(qf)
    kvc = _cmp_mlp_call(cmp_rows.reshape(bsz, t // NSA_CMP_STRIDE, NSA_CMP_STRIDE * LANE), *w['cmp'],
                        precise=precise)
    n_cmp = (t - NSA_CMP_LEN) // NSA_CMP_STRIDE + 1
    n_sel = -(-t // NSA_SEL_BLOCK)
    nselp = -(-n_sel // LANE) * LANE
    o_cmp, sel, _ = _nsa_cmp_call(qn, kvc, tq=tq, q_pos0=0, n_cmp=n_cmp, n_sel=n_sel, nselp=nselp,
                                  precise=precise)
    o_slc = _attn_call(qn, b3(slc_rows), tq=tq, tk=tk, bm=sel, bm_bs=NSA_SEL_BLOCK, mode=mode,
                       name="nsa_slc")
    tw = min(tq, tk, NSA_WINDOW // 2)
    o_win = _attn_call(qn, b3(win_rows), tq=tw, tk=tw, window=NSA_WINDOW, mode=mode, name="nsa_win")
    moba3 = b3(moba_rows)
    kmean = _pad_rows(_kmean_call(moba3), LANE)
    n_blk = -(-t // MOBA_BLOCK)
    bm = _moba_gate_call(qm, kmean, tq=tq, k_top=min(MOBA_TOPK, n_blk - 1), precise=precise)
    o_moba = _attn_call(qm, moba3, tq=tq, tk=tk, bm=bm, bm_bs=MOBA_BLOCK, bm_hs=LANE, mode=mode,
                        name="moba")
    logf = small[:, SM_FOXF:SM_FOXF + N_HEADS].reshape(bsz, t, N_HEADS)
    cum_t = _cum_call(_heads_to_sublanes(logf, t))
    cq = jnp.pad(jnp.swapaxes(cum_t, 1, 2), ((0, 0), (0, 0), (0, LANE - SUB)))
    o_fox = _attn_call(qf, b3(fox_rows), tq=tq, tk=tk, cq=cq, cum_t=cum_t, mode=mode, name="fox")
    y_ssd, ssd_state = _ssd_call(b3(xbc), b3(z), b3(dt), w['convw'], w['convb'], w['hp'], w['norm'],
                                 precise)
    f2 = lambda a: a.reshape(n, a.shape[-1])
    x1, xn2, comb = _post_call(x2, w['gmix'], w['wgate'], small, f2(o_cmp), f2(o_slc), f2(o_win),
                               f2(o_moba), f2(o_fox), f2(y_ssd), w['wbr'], w['wout'], w['gffn'],
                               w['wr'], w['br'], tm, mode, t // tm)
    y = _moe_call(xn2, x1, comb, w_ffn['weg'], w_ffn['weu'], w_ffn['wed'], min(n, 4 * tm))
    wb = min(NSA_WINDOW, t)
    state = (cmp_rows.reshape(bsz, t, 2, HEAD_DIM), slc_rows.reshape(bsz, t, 2, HEAD_DIM),
             win_rows.reshape(bsz, t, 2, HEAD_DIM)[:, t - wb:],
             moba_rows.reshape(bsz, t, 2, N_HEADS, HEAD_DIM), fox_rows.reshape(bsz, t, 2, N_HEADS, HEAD_DIM),
             logf, ssd_state, b3(xbc)[:, t - (SSD_CONV - 1):])
    return y.reshape(bsz, t, D_MODEL), state


def _cache_views(cache_nsa_cmp, cache_nsa_slc, state_nsa_win, cache_moba, cache_fox, cache_fox_logf):
    rows_last = lambda c: jnp.moveaxis(c, 2, -1)
    return (rows_last(cache_nsa_cmp), rows_last(cache_nsa_slc)[:, :, :, None],
            rows_last(state_nsa_win)[:, :, :, None], rows_last(cache_moba), rows_last(cache_fox),
            rows_last(cache_fox_logf))


def _layer_sample(x, w, views, layer, s_win, s_ssd, s_conv, page_table, *, g_pages):
    cmp_t, slc_t, win_t, moba_t, fox_t, logf_t = views
    bsz = x.shape[0]
    n_pages = page_table.shape[1]
    pos0 = n_pages * PAGE
    x2 = x.reshape(bsz, D_MODEL)
    tabs = _rope_tables(jnp.full((bsz,), pos0, I32))
    (qn, cmp_rows, slc_rows, win_rows, qm, moba_rows, qf, fox_rows, z, xbc, small, dt) = _proj_call(
        x2, w['gmix'], w['wp'], w['gains'], tabs, w['biases'], bsz, 'precise')
    q8 = lambda a: _pad_rows(a[:, None, :], SUB)
    qn, qm, qf = q8(qn), q8(qm), q8(qf)
    every_head = lambda n: jnp.full((bsz, n), -1, I32)
    assert (pos0 + 1 - NSA_CMP_LEN) // NSA_CMP_STRIDE + 1 == pos0 // NSA_CMP_STRIDE - 1
    kvc = _kt_cmp_call(cmp_t, layer, page_table, *w['cmp'], g=g_pages)
    n_cmp = pos0 // NSA_CMP_STRIDE - 1
    n_sel = -(-(pos0 + 1) // NSA_SEL_BLOCK)
    nselp = -(-n_sel // LANE) * LANE
    o_cmp, _, sel_idx = _nsa_cmp_call(qn, kvc, tq=SUB, q_pos0=pos0, n_cmp=n_cmp, n_sel=n_sel,
                                      nselp=nselp, precise=True)
    k_top = min(NSA_SEL_TOPK, n_sel)
    sel_ids = sel_idx[:, 0, :k_top]
    sel_heads = jnp.where(sel_ids < pos0 // NSA_SEL_BLOCK, -1, N_HEADS)
    o_slc = _kt_attn_call(qn, slc_t, layer, page_table, sel_ids, sel_heads, q8(slc_rows),
                          r=NSA_SEL_BLOCK, g=k_top, name="nsa_slc_s")
    wb = s_win.shape[1]
    o_win = _kt_attn_call(qn, win_t, layer, jnp.arange(bsz, dtype=I32)[:, None], jnp.zeros((bsz, 1), I32),
                          every_head(1), q8(win_rows), r=wb, g=1, lo=wb - NSA_WINDOW + 1, name="nsa_win_s")
    n_blk = -(-(pos0 + 1) // MOBA_BLOCK)
    m_top = min(MOBA_TOPK, n_blk - 1)
    blk_ids = _kt_gate_call(qm, moba_t, layer, page_table, g=g_pages, k_top=m_top)[:, :N_HEADS, :m_top]
    ppb = MOBA_BLOCK // PAGE
    moba_fetch = (blk_ids[..., None] * ppb + jnp.arange(ppb, dtype=I32)).reshape(bsz, -1)
    head_id = jnp.arange(N_HEADS, dtype=I32)[None, :, None, None]
    moba_heads = jnp.broadcast_to(jnp.where(blk_ids[..., None] >= 0, head_id, N_HEADS),
                                  blk_ids.shape + (ppb,)).reshape(bsz, -1)
    o_moba = _kt_attn_call(qm, moba_t, layer, page_table, moba_fetch, moba_heads, q8(moba_rows), r=PAGE,
                           g=m_top * ppb, one_head=True, name="moba_s")
    logf_new = small[:, SM_FOXF:SM_FOXF + N_HEADS]
    fox_fetch = jnp.tile(jnp.arange(n_pages - 1, -1, -1, dtype=I32)[None, :], (bsz, 1))
    o_fox = _kt_attn_call(qf, fox_t, layer, page_table, fox_fetch, every_head(n_pages), q8(fox_rows),
                          r=PAGE, g=g_pages, logf_t=logf_t, logf_new=q8(small), name="fox_s")
    full = jnp.concatenate([s_conv, xbc[:, None, :]], axis=1)
    y_ssd, ssd_state = _ssd_step_call(_pad_rows(full, SUB), z[:, None, :], dt[:, None, :], s_ssd,
                                      w['w8'], w['convb'], w['hp'], w['norm'])
    first = lambda a: a[:, 0, :]
    x1, xn2, comb = _post_call(x2, w['gmix'], w['wgate'], small, first(o_cmp), first(o_slc), first(o_win),
                               first(o_moba), first(o_fox), first(y_ssd), w['wbr'], w['wout'],
                               w['gffn'], w['wr'], w['br'], bsz, 'precise', 1)
    y = _moe_call(xn2, x1, comb, w['weg'], w['weu'], w['wed'], bsz, precise=True)
    state = (cmp_rows.reshape(bsz, 1, 2, HEAD_DIM), slc_rows.reshape(bsz, 1, 2, HEAD_DIM),
             jnp.concatenate([s_win[:, 1:], win_rows.reshape(bsz, 1, 2, HEAD_DIM)], axis=1),
             moba_rows.reshape(bsz, 1, 2, N_HEADS, HEAD_DIM), fox_rows.reshape(bsz, 1, 2, N_HEADS, HEAD_DIM),
             logf_new[:, None, :], ssd_state, full[:, 1:])
    return y.reshape(bsz, 1, D_MODEL), state


_PARAM_NAMES = ('norm_mix', 'norm_ffn', 'w_in', 'g_nsa_q', 'g_nsa_k', 'g_nsa_kc', 'pe_cmp_k', 'pe_cmp_v',
                'w_cmp_k1', 'w_cmp_k2', 'w_cmp_v1', 'w_cmp_v2', 'g_moba_q', 'g_moba_k', 'g_fox_q',
                'g_fox_k', 'b_fox_f', 'ssd_conv_w', 'ssd_conv_b', 'ssd_dt_bias', 'ssd_a_log', 'ssd_d',
                'ssd_norm', 'w_br_nsa', 'w_br_moba', 'w_br_fox', 'w_br_ssd', 'w_out', 'w_router_grp',
                'b_router_grp', 'w_router_exp', 'b_router_exp', 'w_exp_gate', 'w_exp_up', 'w_exp_down')


def kernel(x_prompt, x_sample, cache_nsa_cmp, cache_nsa_slc, state_nsa_win, cache_moba, cache_fox, cache_fox_logf, state_ssd, state_ssd_conv, page_table, norm_mix, norm_ffn, w_in, g_nsa_q, g_nsa_k, g_nsa_kc, pe_cmp_k, pe_cmp_v, w_cmp_k1, w_cmp_k2, w_cmp_v1, w_cmp_v2, g_moba_q, g_moba_k, g_fox_q, g_fox_k, b_fox_f, ssd_conv_w, ssd_conv_b, ssd_dt_bias, ssd_a_log, ssd_d, ssd_norm, w_br_nsa, w_br_moba, w_br_fox, w_br_ssd, w_out, w_router_grp, b_router_grp, w_router_exp, b_router_exp, w_exp_gate, w_exp_up, w_exp_down):
    params = dict(zip(_PARAM_NAMES, (
        norm_mix, norm_ffn, w_in, g_nsa_q, g_nsa_k, g_nsa_kc, pe_cmp_k, pe_cmp_v, w_cmp_k1, w_cmp_k2,
        w_cmp_v1, w_cmp_v2, g_moba_q, g_moba_k, g_fox_q, g_fox_k, b_fox_f, ssd_conv_w, ssd_conv_b,
        ssd_dt_bias, ssd_a_log, ssd_d, ssd_norm, w_br_nsa, w_br_moba, w_br_fox, w_br_ssd, w_out,
        w_router_grp, b_router_grp, w_router_exp, b_router_exp, w_exp_gate, w_exp_up, w_exp_down)))
    depth = norm_mix.shape[0]
    views = _cache_views(cache_nsa_cmp, cache_nsa_slc, state_nsa_win, cache_moba, cache_fox, cache_fox_logf)
    xp, xs = x_prompt, x_sample
    sp, ss = [], []
    for l in range(depth):
        w_fast, w_exact = _layer_weights({k: v[l] for k, v in params.items()})
        feeds_next = l < depth - 1
        xp, st_p = _layer_prompt(xp, w_exact if feeds_next else w_fast, w_fast, tm=256, tq=512, tk=512,
                                 mode='tail' if feeds_next else 'fast')
        xs, st_s = _layer_sample(xs, w_exact, views, l, state_nsa_win[l], state_ssd[l], state_ssd_conv[l],
                                 page_table, g_pages=16)
        sp.append(st_p)
        ss.append(st_s)
    outs = [xp, xs]
    for i in range(8):
        outs.append(jnp.stack([s[i] for s in sp], axis=0))
        outs.append(jnp.stack([s[i] for s in ss], axis=0))
    return tuple(outs)
```

```python
import functools
import math

import jax
import jax.numpy as jnp
import numpy as np
from jax import lax
from jax.experimental import pallas as pl
from jax.experimental.pallas import tpu as pltpu

F32 = jnp.float32
BF16 = jnp.bfloat16
I32 = jnp.int32

D_MODEL = 1024
HEAD_DIM = 64
N_HEADS = 4
HEADS_W = N_HEADS * HEAD_DIM
ROPE_DIM = HEAD_DIM // 4
ROPE_HALF = ROPE_DIM // 2
ROPE_THETA = 500000.0
SCALE = HEAD_DIM ** -0.5
EPS = 1e-6
NEG = -1e30
PAGE = 128

NSA_CMP_LEN = 32
NSA_CMP_STRIDE = 16
NSA_CMP_HIDDEN = 2 * HEAD_DIM
NSA_SEL_BLOCK = 64
NSA_SEL_TOPK = 16
NSA_WINDOW = 512
MOBA_BLOCK = 256
MOBA_TOPK = 3

SSD_HEADS = 8
SSD_HEAD_DIM = 64
SSD_INNER = SSD_HEADS * SSD_HEAD_DIM
SSD_STATE = 64
SSD_GROUPS = 2
SSD_CONV = 4
SSD_CHUNK = 128
SSD_BC = SSD_GROUPS * SSD_STATE
SSD_CONV_DIM = SSD_INNER + 2 * SSD_BC

MOE_GROUPS = 4
MOE_PER_GROUP = 4
MOE_EXPERTS = 16
MOE_HIDDEN = 512

LANE = 128
SUB = 8
VMEM_LIMIT = 56 * 1024 * 1024

_IN_SIZES = (HEADS_W, 6 * HEAD_DIM, 3 * N_HEADS, 3 * HEADS_W, 3 * HEADS_W, N_HEADS,
             2 * SSD_INNER + 2 * SSD_BC + SSD_HEADS, 4 * D_MODEL)
_IN_OFFS = tuple(int(sum(_IN_SIZES[:i])) for i in range(len(_IN_SIZES) + 1))

P_NSA_Q = 0
P_NSA_KV = 256
P_MOBA = 640
P_FOX = 1408
P_SSD = 2176
P_SMALL = 3456
P_DT = 3584
P_WIDTH = 3712
SM_FOXF = 0
SM_NSAG = 4


def _cparams(sem):
    return pltpu.CompilerParams(dimension_semantics=sem, vmem_limit_bytes=VMEM_LIMIT)


def _sigmoid(x):
    return 1.0 / (1.0 + jnp.exp(-x))


def _silu(x):
    return x * _sigmoid(x)


def _softplus(x):
    return jnp.maximum(x, 0.0) + jnp.log(1.0 + jnp.exp(-jnp.abs(x)))


def _dot(a, b):
    return jnp.dot(a, b, preferred_element_type=F32)


def _dot_nt(a, b):
    return lax.dot_general(a, b, (((1,), (1,)), ((), ())), preferred_element_type=F32)


def _split2(x):
    hi = x.astype(BF16)
    lo = (x - hi.astype(F32)).astype(BF16)
    return hi, lo


def _split3(x):
    h1 = x.astype(BF16)
    r = x - h1.astype(F32)
    h2 = r.astype(BF16)
    h3 = (r - h2.astype(F32)).astype(BF16)
    return h1, h2, h3


def _mm(a, b, precise):
    if precise:
        ah, al = _split2(a)
        bh, bl = _split2(b)
        return _dot(ah, bh) + _dot(ah, bl) + _dot(al, bh)
    return _dot(a.astype(BF16), b.astype(BF16))


def _wmm(a, w_ref, idx, precise):
    w_hi = w_ref[(0,) + idx]
    if precise:
        ah, al = _split2(a)
        return _dot(ah, w_hi) + _dot(ah, w_ref[(1,) + idx]) + _dot(al, w_hi)
    return _dot(a.astype(BF16), w_hi)


def _split_w(w, precise):
    if not precise:
        return w.astype(BF16)[None]
    bits = lax.bitcast_convert_type(w, jnp.uint32)
    bits = (bits + jnp.uint32(0x7FFF) + ((bits >> 16) & jnp.uint32(1))) & jnp.uint32(0xFFFF0000)
    hi = lax.bitcast_convert_type(bits, F32)
    return jnp.stack([hi.astype(BF16), (w - hi).astype(BF16)])


def _by_tile(mode, is_tail, body):
    if mode == 'tail':
        pl.when(is_tail)(lambda: body(True))
        pl.when(jnp.logical_not(is_tail))(lambda: body(False))
    else:
        body(mode == 'precise')


def _mm_nt(a, b, precise):
    if precise:
        ah, al = _split2(a)
        bh, bl = _split2(b)
        return _dot_nt(ah, bh) + _dot_nt(ah, bl) + _dot_nt(al, bh)
    return _dot_nt(a.astype(BF16), b.astype(BF16))


def _dot_exact(x, e):
    h1, h2, h3 = _split3(x)
    return _dot(h1, e) + _dot(h2, e) + _dot(h3, e)


def _head_sumsq(y, precise=False):
    r = lax.broadcasted_iota(I32, (LANE, LANE), 0) // HEAD_DIM
    c = lax.broadcasted_iota(I32, (LANE, LANE), 1) // HEAD_DIM
    e = jnp.where(r == c, 1.0, 0.0).astype(BF16)
    if precise:
        return _dot_exact(y * y, e)
    hi, lo = _split2(y * y)
    return _dot(hi, e) + _dot(lo, e)


def _rope_chunk(y, cos_t, sin_a, sin_b):
    return (y * cos_t + pltpu.roll(y, LANE - ROPE_HALF, 1) * sin_a
            + pltpu.roll(y, ROPE_HALF, 1) * sin_b)


def _proj_kernel(*refs, mode, nt):
    is_tail = (pl.program_id(0) % nt) == nt - 1
    _by_tile(mode, is_tail,
             lambda precise: _proj_body(*refs, precise=precise, keys_precise=mode != 'fast'))


def _proj_body(x_ref, gmix_ref, w_ref, gains_ref, tab_ref, bias_ref,
               qn_ref, cmp_ref, slc_ref, win_ref, qm_ref, moba_ref, qf_ref, fox_ref,
               z_ref, xbc_ref, small_ref, dt_ref, *, precise, keys_precise):
    x = x_ref[...]
    ms = jnp.mean(x * x, axis=-1, keepdims=True)
    xn = x * lax.rsqrt(ms + EPS) * gmix_ref[...]
    xn_fast = xn.astype(BF16)
    cos_t = tab_ref[0]
    sin_a = tab_ref[1]
    sin_b = tab_ref[2]
    lane = lax.broadcasted_iota(I32, (1, LANE), 1)
    first = lane < HEAD_DIM

    def proj(off, width, shared=False):
        prec = precise or (shared and keys_precise)
        return _wmm(xn if prec else xn_fast, w_ref, (slice(None), slice(off, off + width)), prec)

    def normed(y, gain_row, k_only=False, shared=False):
        g = gains_ref[gain_row:gain_row + 1, :]
        prec = precise or (shared and keys_precise)
        yn = y * lax.rsqrt(_head_sumsq(y, prec) * (1.0 / HEAD_DIM) + EPS) * g
        return jnp.where(first, yn, y) if k_only else yn

    p = proj(P_NSA_Q, HEADS_W)
    for c in range(2):
        y = _rope_chunk(normed(p[:, c * LANE:(c + 1) * LANE], 0), cos_t, sin_a, sin_b)
        qn_ref[:, c * LANE:(c + 1) * LANE] = (y * SCALE).astype(qn_ref.dtype)
    p = proj(P_NSA_KV, 3 * LANE, shared=True)
    cos_k = jnp.where(first, cos_t, 1.0)
    sin_ak = jnp.where(first, sin_a, 0.0)
    sin_bk = jnp.where(first, sin_b, 0.0)
    for c, ref in enumerate((cmp_ref, slc_ref, win_ref)):
        y = normed(p[:, c * LANE:(c + 1) * LANE], 1 + c, k_only=True, shared=True)
        ref[...] = _rope_chunk(y, cos_k, sin_ak, sin_bk)
    p = proj(P_MOBA, HEADS_W)
    pkv = proj(P_MOBA + HEADS_W, 2 * HEADS_W, shared=True)
    for c in range(2):
        y = _rope_chunk(normed(p[:, c * LANE:(c + 1) * LANE], 4), cos_t, sin_a, sin_b)
        qm_ref[:, c * LANE:(c + 1) * LANE] = (y * SCALE).astype(qn_ref.dtype)
        y = _rope_chunk(normed(pkv[:, c * LANE:(c + 1) * LANE], 5, shared=True), cos_t, sin_a, sin_b)
        moba_ref[:, c * LANE:(c + 1) * LANE] = y
    moba_ref[:, HEADS_W:] = pkv[:, HEADS_W:]
    p = proj(P_FOX, HEADS_W)
    pkv = proj(P_FOX + HEADS_W, 2 * HEADS_W, shared=True)
    for c in range(2):
        y = normed(p[:, c * LANE:(c + 1) * LANE], 6)
        qf_ref[:, c * LANE:(c + 1) * LANE] = (y * SCALE).astype(qn_ref.dtype)
        fox_ref[:, c * LANE:(c + 1) * LANE] = normed(pkv[:, c * LANE:(c + 1) * LANE], 7, shared=True)
    fox_ref[:, HEADS_W:] = pkv[:, HEADS_W:]
    z_ref[...] = proj(P_SSD, SSD_INNER)
    xbc_ref[...] = proj(P_SSD + SSD_INNER, SSD_CONV_DIM, shared=True)
    p = proj(P_SMALL, LANE, shared=True)
    logf = -_softplus(-(p + bias_ref[0:1, :]))
    small_ref[...] = jnp.where(lane < SM_NSAG, logf, p)
    p = proj(P_DT, LANE, shared=True)
    dt_ref[...] = _softplus(p + bias_ref[1:2, :])


def _proj_call(x2, gmix, wp, gains, tabs, biases, tm, mode):
    n = x2.shape[0]
    nt = tabs.shape[1] // tm
    row = lambda w: pl.BlockSpec((tm, w), lambda i: (i, 0))
    full = lambda a: pl.BlockSpec(a.shape, lambda i: (0,) * a.ndim)
    widths = (HEADS_W, LANE, LANE, LANE, HEADS_W, 2 * HEADS_W, HEADS_W, 2 * HEADS_W,
              SSD_INNER, SSD_CONV_DIM, LANE, LANE)
    qd = BF16 if mode == 'fast' else F32
    dtypes = (qd, F32, F32, F32, qd, F32, qd, F32, F32, F32, F32, F32)
    return pl.pallas_call(
        functools.partial(_proj_kernel, mode=mode, nt=nt),
        grid=(n // tm,),
        in_specs=[row(D_MODEL), full(gmix), full(wp), full(gains),
                  pl.BlockSpec((3, tm, LANE), lambda i: (0, i % nt, 0)), full(biases)],
        out_specs=[row(w) for w in widths],
        out_shape=[jax.ShapeDtypeStruct((n, w), d) for w, d in zip(widths, dtypes)],
        compiler_params=_cparams(("parallel",)),
        name="proj",
    )(x2, gmix, wp, gains, tabs, biases)


def _rope_tables(pos):
    inv = jnp.power(ROPE_THETA, -jnp.arange(ROPE_HALF, dtype=F32) / ROPE_HALF)
    ang = pos.astype(F32)[:, None] * inv[None, :]
    cos, sin = jnp.cos(ang), jnp.sin(ang)
    t = pos.shape[0]
    one = jnp.ones((t, HEAD_DIM - ROPE_DIM), F32)
    zero = jnp.zeros((t, HEAD_DIM - ROPE_DIM), F32)
    zh = jnp.zeros((t, ROPE_HALF), F32)
    c = jnp.concatenate([cos, cos, one], axis=1)
    a = jnp.concatenate([-sin, zh, zero], axis=1)
    b = jnp.concatenate([zh, sin, zero], axis=1)
    return jnp.stack([jnp.tile(c, (1, 2)), jnp.tile(a, (1, 2)), jnp.tile(b, (1, 2))], axis=0)


def _pack_w_in(w_in):
    o = _IN_OFFS
    ssd = w_in[:, o[6]:o[7]]
    small = jnp.concatenate([w_in[:, o[5]:o[6]], w_in[:, o[2]:o[3]],
                             jnp.zeros((D_MODEL, LANE - 4 * N_HEADS), F32)], axis=1)
    dt = jnp.concatenate([ssd[:, 2 * SSD_INNER + 2 * SSD_BC:],
                          jnp.zeros((D_MODEL, LANE - SSD_HEADS), F32)], axis=1)
    wp = jnp.concatenate([w_in[:, o[0]:o[2]], w_in[:, o[3]:o[5]],
                          ssd[:, :2 * SSD_INNER + 2 * SSD_BC], small, dt], axis=1)
    return wp, w_in[:, o[7]:o[8]]


def _proj_gains(prm):
    two = lambda g: jnp.tile(g, 2)
    ones = jnp.ones((HEAD_DIM,), F32)
    rows = [two(prm['g_nsa_q'])]
    rows += [jnp.concatenate([prm['g_nsa_k'][i], ones]) for i in range(3)]
    rows += [two(prm['g_moba_q']), two(prm['g_moba_k']), two(prm['g_fox_q']), two(prm['g_fox_k'])]
    return jnp.stack(rows, axis=0)


def _proj_biases(prm):
    pad = lambda v: jnp.concatenate([v.astype(F32), jnp.zeros((LANE - v.shape[0],), F32)])
    rows = [pad(prm['b_fox_f']), pad(prm['ssd_dt_bias'])] + [jnp.zeros((LANE,), F32)] * 6
    return jnp.stack(rows, axis=0)


def _cmp_mlp_kernel(r_ref, wab_ref, pe_ref, w2_ref, g_ref, out_ref, *, precise):
    n16 = r_ref.shape[1]
    hw = 2 * NSA_CMP_HIDDEN
    h = _mm(r_ref[0], wab_ref[...], precise)
    c = _mm(pe_ref[...], wab_ref[...], precise)
    c = c[0:1, :hw] + c[1:2, hw:]
    pre = h[:, :hw] + pltpu.roll(h[:, hw:], n16 - 1, 0) + c
    act = 0.5 * pre * (1.0 + jnp.tanh(math.sqrt(2.0 / math.pi) * (pre + 0.044715 * pre * pre * pre)))
    y = _mm(act, w2_ref[...], precise)
    lane = lax.broadcasted_iota(I32, (1, LANE), 1)
    yn = y * lax.rsqrt(_head_sumsq(y, precise) * (1.0 / HEAD_DIM) + EPS) * g_ref[...]
    out_ref[0] = jnp.where(lane < HEAD_DIM, yn, y)


def _cmp_mlp_call(rows16, wab, pe2, w2, gkc, precise=False):
    bsz, n16, w = rows16.shape
    full = lambda a: pl.BlockSpec(a.shape, lambda b: (0,) * a.ndim)
    return pl.pallas_call(
        functools.partial(_cmp_mlp_kernel, precise=precise),
        grid=(bsz,),
        in_specs=[pl.BlockSpec((1, n16, w), lambda b: (b, 0, 0)), full(wab), full(pe2), full(w2),
                  full(gkc)],
        out_specs=pl.BlockSpec((1, n16, LANE), lambda b: (b, 0, 0)),
        out_shape=jax.ShapeDtypeStruct((bsz, n16, LANE), F32),
        compiler_params=_cparams(("parallel",)),
        name="cmp_mlp",
    )(rows16, wab, pe2, w2, gkc)


def _cmp_weights(prm, dtype):
    def expand(w1, slot):
        w = w1.reshape(2, NSA_CMP_STRIDE, HEAD_DIM, NSA_CMP_HIDDEN)
        z = jnp.zeros_like(w)
        pair = (w, z) if slot == 0 else (z, w)
        return jnp.concatenate(pair, axis=2).reshape(2, NSA_CMP_STRIDE * LANE, NSA_CMP_HIDDEN)
    wk, wv = expand(prm['w_cmp_k1'], 0), expand(prm['w_cmp_v1'], 1)
    wab = jnp.concatenate([wk[0], wv[0], wk[1], wv[1]], axis=1).astype(dtype)
    pe = jnp.concatenate([prm['pe_cmp_k'], prm['pe_cmp_v']], axis=1)
    pe = pe.reshape(2, NSA_CMP_STRIDE * LANE)
    pe2 = jnp.concatenate([pe, jnp.zeros((SUB - 2, NSA_CMP_STRIDE * LANE), F32)], axis=0)
    z = jnp.zeros((NSA_CMP_HIDDEN, HEAD_DIM), F32)
    w2 = jnp.concatenate([jnp.concatenate([prm['w_cmp_k2'], z], axis=1),
                          jnp.concatenate([z, prm['w_cmp_v2']], axis=1)], axis=0).astype(dtype)
    gkc = jnp.concatenate([prm['g_nsa_kc'], jnp.ones((HEAD_DIM,), F32)])[None, :]
    return wab, pe2, w2, gkc


def _first_argmax(score, lane):
    m = jnp.max(score, axis=-1, keepdims=True)
    cand = jnp.where(score == m, lane.astype(F32), float(score.shape[-1]))
    return m, jnp.min(cand, axis=-1, keepdims=True).astype(I32)


def _nsa_cmp_kernel(q_ref, kvc_ref, ocmp_ref, sel_ref, idx_ref, *, tq, q_pos0, n_cmp, n_sel, k_top,
                    precise):
    n16 = kvc_ref.shape[1]
    nselp = sel_ref.shape[2]
    qi = pl.program_id(1)
    qpos = q_pos0 + qi * tq + lax.broadcasted_iota(I32, (tq, 1), 0)
    q = q_ref[0]
    kvc = kvc_ref[0]
    kc = kvc[:, :HEAD_DIM] if precise else kvc[:, :HEAD_DIM].astype(BF16)
    vc = kvc[:, HEAD_DIM:] if precise else kvc[:, HEAD_DIM:].astype(BF16)
    n_idx = lax.broadcasted_iota(I32, (1, n16), 1)
    mask = jnp.logical_and(n_idx * NSA_CMP_STRIDE + (NSA_CMP_LEN - 1) <= qpos, n_idx < n_cmp)
    psum = jnp.zeros((tq, n16), F32)
    outs = []
    for h in range(N_HEADS):
        lg = jnp.where(mask, _mm_nt(q[:, h * HEAD_DIM:(h + 1) * HEAD_DIM], kc, precise), NEG)
        m = jnp.max(lg, axis=-1, keepdims=True)
        p = jnp.where(mask, jnp.exp(lg - m), 0.0)
        p = p / jnp.maximum(jnp.sum(p, axis=-1, keepdims=True), 1e-30)
        outs.append(_mm(p, vc, precise))
        psum = psum + p
    ocmp_ref[0] = jnp.concatenate(outs, axis=-1)
    n_col = lax.broadcasted_iota(I32, (n16, nselp), 0) * NSA_CMP_STRIDE
    s_row = lax.broadcasted_iota(I32, (n16, nselp), 1) * NSA_SEL_BLOCK
    cover = jnp.logical_and(n_col <= s_row + (NSA_SEL_BLOCK - 1), n_col + (NSA_CMP_LEN - 1) >= s_row)
    cover = jnp.where(cover, 1.0, 0.0).astype(BF16)
    imp = _dot_exact(psum, cover)
    blk = lax.broadcasted_iota(I32, (1, nselp), 1)
    cur = qpos // NSA_SEL_BLOCK
    forced = jnp.logical_or(blk == 0, jnp.logical_or(blk == cur, blk == cur - 1))
    score = jnp.where(blk > cur, -1e9, jnp.where(forced, 1e9, imp))
    score = jnp.where(blk < n_sel, score, -jnp.inf)
    sel = jnp.zeros((tq, nselp), F32)
    lane = lax.broadcasted_iota(I32, (1, LANE), 1)
    picks = jnp.full((tq, LANE), -1, I32)
    for i in range(k_top):
        _, idx = _first_argmax(score, blk)
        hit = blk == idx
        sel = jnp.where(hit, 1.0, sel)
        picks = jnp.where(lane == i, idx, picks)
        score = jnp.where(hit, -jnp.inf, score)
    sel_ref[0] = sel
    idx_ref[0] = picks


def _nsa_cmp_call(q, kvc, *, tq, q_pos0, n_cmp, n_sel, nselp, precise=False):
    bsz, t, _ = q.shape
    n16 = kvc.shape[1]
    kern = functools.partial(_nsa_cmp_kernel, tq=tq, q_pos0=q_pos0, n_cmp=n_cmp, n_sel=n_sel,
                             k_top=min(NSA_SEL_TOPK, n_sel), precise=precise)
    return pl.pallas_call(
        kern,
        grid=(bsz, t // tq),
        in_specs=[pl.BlockSpec((1, tq, HEADS_W), lambda b, i: (b, i, 0)),
                  pl.BlockSpec((1, n16, LANE), lambda b, i: (b, 0, 0))],
        out_specs=[pl.BlockSpec((1, tq, HEADS_W), lambda b, i: (b, i, 0)),
                   pl.BlockSpec((1, tq, nselp), lambda b, i: (b, i, 0)),
                   pl.BlockSpec((1, tq, LANE), lambda b, i: (b, i, 0))],
        out_shape=[jax.ShapeDtypeStruct((bsz, t, HEADS_W), F32),
                   jax.ShapeDtypeStruct((bsz, t, nselp), F32),
                   jax.ShapeDtypeStruct((bsz, t, LANE), I32)],
        compiler_params=_cparams(("parallel", "parallel")),
        name="nsa_cmp",
    )(q, kvc)


ATTN_CHUNK = 128


def _attn_pairs(nq, tq, tk, window):
    qi_l, ki_l, first, last = [], [], [], []
    for qi in range(nq):
        lo = max(qi * tq - (window - 1), 0) // tk if window else 0
        hi = (qi * tq + tq - 1) // tk
        for ki in range(lo, hi + 1):
            qi_l.append(qi)
            ki_l.append(ki)
            first.append(int(ki == lo))
            last.append(int(ki == hi))
    return tuple(np.asarray(a, np.int32) for a in (qi_l, ki_l, first, last))


def _mm_tn(a, b, precise):
    dims = (((0,), (0,)), ((), ()))
    dot = lambda x, y: lax.dot_general(x, y, dims, preferred_element_type=F32)
    if precise:
        ah, al = _split2(a)
        bh, bl = _split2(b)
        return dot(ah, bh) + dot(ah, bl) + dot(al, bh)
    return dot(a.astype(BF16), b.astype(BF16))


def _attn_kernel(qi_ref, ki_ref, first_ref, last_ref, *refs, cfg):
    tq, tk, kw = cfg['tq'], cfg['tk'], cfg['kw']
    refs = list(refs)
    q_ref, kv_ref = refs.pop(0), refs.pop(0)
    bm_ref = refs.pop(0) if cfg['bm_bs'] else None
    if cfg['bias']:
        cqt_ref, cumk_ref = refs.pop(0), refs.pop(0)
    o_ref, m_ref, l_ref, acc_ref, qt_ref = refs[:5]
    bmt_ref = refs[5] if bm_ref is not None else None
    step = pl.program_id(1)
    qi, ki = qi_ref[step], ki_ref[step]

    @pl.when(first_ref[step] == 1)
    def _():
        m_ref[...] = jnp.full(m_ref.shape, NEG, F32)
        l_ref[...] = jnp.zeros(l_ref.shape, F32)
        acc_ref[...] = jnp.zeros(acc_ref.shape, F32)
        qt = q_ref[0].astype(F32).T
        zero = jnp.zeros((HEAD_DIM, tq), F32)
        for h in range(N_HEADS):
            qh = qt[h * HEAD_DIM:(h + 1) * HEAD_DIM]
            upper = kw == HEAD_DIM or h % 2 == 0
            qpad = jnp.concatenate([qh, zero] if upper else [zero, qh], axis=0)
            hi = qpad.astype(BF16)
            qt_ref[0, h] = hi
            if cfg['mode'] != 'fast':
                qt_ref[1, h] = (qpad - hi.astype(F32)).astype(BF16)
        if bm_ref is not None:
            bmt_ref[...] = bm_ref[0].T.astype(BF16)

    def compute(precise):
        qpos = qi * tq + lax.broadcasted_iota(I32, (1, tq), 1)
        hs = cfg['bm_hs']
        n_chunks = tk // ATTN_CHUNK
        heads = range(N_HEADS)
        rows, logits = [], []
        for c in range(n_chunks):
            rc = kv_ref[0, c * ATTN_CHUNK:(c + 1) * ATTN_CHUNK, :]
            rows.append(rc)
            kpos = ki * tk + c * ATTN_CHUNK + lax.broadcasted_iota(I32, (ATTN_CHUNK, 1), 0)
            mask = kpos <= qpos
            if cfg['window']:
                mask = jnp.logical_and(mask, qpos - kpos < cfg['window'])
            neg = jnp.where(mask, 0.0, NEG)
            if bm_ref is not None:
                j = lax.broadcasted_iota(I32, (1, hs if hs else bmt_ref.shape[0]), 1)
                expand = jnp.where(j == kpos // cfg['bm_bs'], 1.0, 0.0).astype(BF16)
                if not hs:
                    neg = neg + (_dot(expand, bmt_ref[...]) - 1.0) * (-NEG)
            per_head = []
            for h in heads:
                pair = 0 if kw == HEAD_DIM else h // 2
                kp = rc[:, pair * LANE:(pair + 1) * LANE]
                if precise:
                    kh, kl = _split2(kp)
                    s = _dot(kh, qt_ref[0, h]) + _dot(kh, qt_ref[1, h]) + _dot(kl, qt_ref[0, h])
                else:
                    s = _dot(kp.astype(BF16), qt_ref[0, h])
                s = s + neg
                if cfg['bias']:
                    s = s + (cqt_ref[0][h:h + 1, :]
                             - cumk_ref[0, c * ATTN_CHUNK:(c + 1) * ATTN_CHUNK, h:h + 1])
                if bm_ref is not None and hs:
                    s = s + (_dot(expand, bmt_ref[h * hs:(h + 1) * hs, :]) - 1.0) * (-NEG)
                per_head.append(s)
            logits.append(per_head)
        m = [m_ref[h] for h in heads]
        l = [l_ref[h] for h in heads]
        acc = [acc_ref[h] for h in heads]
        for c in range(n_chunks):
            probs, alphas = [], []
            for h in heads:
                s = logits[c][h]
                m_new = jnp.maximum(m[h], jnp.max(s, axis=0, keepdims=True))
                p = jnp.exp(s - m_new[0:1, :])
                alpha = jnp.exp(m[h] - m_new)
                l[h] = alpha * l[h] + jnp.sum(p, axis=0, keepdims=True)
                m[h] = m_new
                probs.append(p)
                alphas.append(alpha)
            for h in heads:
                ko = h * HEAD_DIM if kw > HEAD_DIM else 0
                v = rows[c][:, kw + ko:kw + ko + HEAD_DIM]
                acc[h] = alphas[h][0:1, :] * acc[h] + _mm_tn(v, probs[h], precise)
        for h in heads:
            m_ref[h] = m[h]
            l_ref[h] = l[h]
            acc_ref[h] = acc[h]

    _by_tile(cfg['mode'], qi == cfg['nq'] - 1, compute)

    @pl.when(last_ref[step] == 1)
    def _():
        out_t = jnp.concatenate(
            [acc_ref[h] / jnp.maximum(l_ref[h][0:1, :], 1e-30) for h in range(N_HEADS)], axis=0)
        o_ref[0] = out_t.T


def _attn_call(q, kv, *, tq, tk, window=0, bm=None, bm_bs=0, bm_hs=0, cq=None, cum_t=None,
               mode='fast', name="attn"):
    bsz, t, _ = q.shape
    kw = kv.shape[-1] // 2
    assert t % tq == 0 and t % tk == 0 and tk % ATTN_CHUNK == 0
    tabs = _attn_pairs(t // tq, tq, tk, window)
    cfg = dict(tq=tq, tk=tk, kw=kw, bm_bs=bm_bs, bm_hs=bm_hs, bias=cq is not None, window=window,
               mode=mode, nq=t // tq)
    q_map = lambda b, s, qi, ki, fi, la: (b, qi[s], 0)
    k_map = lambda b, s, qi, ki, fi, la: (b, ki[s], 0)
    args = [q, kv]
    specs = [pl.BlockSpec((1, tq, HEADS_W), q_map), pl.BlockSpec((1, tk, 2 * kw), k_map)]
    scratch = [pltpu.VMEM((N_HEADS, SUB, tq), F32), pltpu.VMEM((N_HEADS, SUB, tq), F32),
               pltpu.VMEM((N_HEADS, HEAD_DIM, tq), F32),
               pltpu.VMEM((1 if mode == 'fast' else 2, N_HEADS, LANE, tq), BF16)]
    if bm is not None:
        args.append(bm)
        specs.append(pl.BlockSpec((1, tq, bm.shape[2]), q_map))
        scratch.append(pltpu.VMEM((bm.shape[2], tq), BF16))
    if cq is not None:
        args += [cum_t, cq]
        specs += [pl.BlockSpec((1, SUB, tq), lambda b, s, qi, ki, fi, la: (b, 0, qi[s])),
                  pl.BlockSpec((1, tk, LANE), k_map)]
    return pl.pallas_call(
        functools.partial(_attn_kernel, cfg=cfg),
        grid_spec=pltpu.PrefetchScalarGridSpec(
            num_scalar_prefetch=4, grid=(bsz, tabs[0].shape[0]), in_specs=specs,
            out_specs=pl.BlockSpec((1, tq, HEADS_W), q_map), scratch_shapes=scratch),
        out_shape=jax.ShapeDtypeStruct((bsz, t, HEADS_W), F32),
        compiler_params=_cparams(("parallel", "arbitrary")), name=name,
    )(*tabs, *args)


def _col_from_row(row, n):
    eye = lax.broadcasted_iota(I32, (n, n), 0) == lax.broadcasted_iota(I32, (n, n), 1)
    return jnp.sum(jnp.where(eye, jnp.broadcast_to(row, (n, n)), 0.0), axis=1, keepdims=True)


def _row_from_col(col, n):
    eye = lax.broadcasted_iota(I32, (n, n), 0) == lax.broadcasted_iota(I32, (n, n), 1)
    return jnp.sum(jnp.where(eye, jnp.broadcast_to(col, (n, n)), 0.0), axis=0, keepdims=True)


def _sublane_total(x):
    for s in (4, 2, 1):
        x = x + pltpu.roll(x, s, 0)
    return x


def _query_page_logits(qb, kt):
    prod = qb * kt
    return _sublane_total(prod.reshape(HEAD_DIM // SUB, SUB, prod.shape[1]).sum(axis=0))


def _kt_attn_kernel(pt_ref, ft_ref, hd_ref, q_ref, *refs, cfg):
    g, nk, hk, pw, r, lo = cfg['g'], cfg['nk'], cfg['hk'], cfg['pw'], cfg['r'], cfg['lo']
    bias, one_head = cfg['bias'], cfg['one_head']
    refs = list(refs)
    kv_refs = [refs.pop(0) for _ in range(g)]
    self_ref = refs.pop(0)
    if bias:
        lf_refs = [refs.pop(0) for _ in range(g)]
        lfnew_ref = refs.pop(0)
    o_ref, m_ref, l_ref, acc_ref, qb_ref = refs[:5]
    carry_ref = refs[5] if bias else None
    b, ki = pl.program_id(0), pl.program_id(1)
    sub = lax.broadcasted_iota(I32, (SUB, pw), 0)
    lane = lax.broadcasted_iota(I32, (SUB, pw), 1)
    kv_head = lambda h: h if hk == N_HEADS else 0

    @pl.when(ki == 0)
    def _():
        m_ref[...] = jnp.full(m_ref.shape, NEG, F32)
        l_ref[...] = jnp.zeros(l_ref.shape, F32)
        acc_ref[...] = jnp.zeros(acc_ref.shape, F32)
        for h in range(N_HEADS):
            qcol = _col_from_row(q_ref[0][0:1, h * HEAD_DIM:(h + 1) * HEAD_DIM], HEAD_DIM)
            qb_ref[h] = jnp.broadcast_to(qcol, (HEAD_DIM, pw))
        if bias:
            row8 = jnp.broadcast_to(lfnew_ref[0][0:1, :], (SUB, LANE))
            e8 = lax.broadcasted_iota(I32, (SUB, LANE), 0) == lax.broadcasted_iota(I32, (SUB, LANE), 1)
            col8 = jnp.sum(jnp.where(e8, row8, 0.0), axis=1, keepdims=True)
            carry_ref[...] = jnp.broadcast_to(col8, carry_ref.shape)

    if bias:
        carry = carry_ref[...]
        pad = jnp.zeros((SUB - N_HEADS, pw), F32)
        stack = jnp.concatenate([x for i in range(g) for x in (lf_refs[i][0, 0], pad)], axis=0)
        upper = (lax.broadcasted_iota(I32, (pw, pw), 0) <= lax.broadcasted_iota(I32, (pw, pw), 1))
        prefix = _dot_exact(stack, jnp.where(upper, 1.0, 0.0).astype(BF16))
    logits, masks = [], []
    for i in range(g):
        j = ki * g + i
        blk, hd = ft_ref[b, j], hd_ref[b, j]
        lane0 = (blk * r) % pw
        ok = jnp.logical_and(jnp.logical_and(lane >= lane0, lane < lane0 + r),
                             blk * r + (lane - lane0) >= lo)
        s = jnp.full((SUB, pw), NEG, F32)
        if one_head:
            hc = jnp.minimum(hd, N_HEADS - 1)
            s = jnp.where(sub == hc, _query_page_logits(qb_ref[hc], kv_refs[i][0, 0, 0, hc]), s)
            ok = jnp.logical_and(ok, jnp.logical_and(sub == hd, hd < N_HEADS))
        else:
            for h in range(N_HEADS):
                s = jnp.where(sub == h, _query_page_logits(qb_ref[h], kv_refs[i][0, 0, 0, kv_head(h)]), s)
            ok = jnp.logical_and(ok, jnp.logical_and(sub < N_HEADS, hd < N_HEADS))
        if bias:
            pre = prefix[i * SUB:(i + 1) * SUB]
            tot = jnp.broadcast_to(pre[:, pw - 1:pw], (SUB, pw))
            s = s + (carry + (tot - pre))
            carry = carry + tot
        logits.append(jnp.where(ok, s, NEG))
        masks.append(ok)
    if bias:
        carry_ref[...] = carry

    m_old = m_ref[...]
    m_new = m_old
    for s in logits:
        m_new = jnp.maximum(m_new, jnp.broadcast_to(jnp.max(s, axis=1, keepdims=True), (SUB, pw)))
    alpha = jnp.exp(m_old - m_new)
    l_new = alpha * l_ref[...]
    probs = []
    for s, ok in zip(logits, masks):
        p = jnp.where(ok, jnp.exp(s - m_new), 0.0)
        l_new = l_new + jnp.broadcast_to(jnp.sum(p, axis=1, keepdims=True), (SUB, pw))
        probs.append(p)
    m_ref[...] = m_new
    l_ref[...] = l_new
    for h in range(N_HEADS):
        acc = alpha[h:h + 1, :] * acc_ref[h]
        if not one_head:
            for i in range(g):
                acc = acc + probs[i][h:h + 1, :] * kv_refs[i][0, 0, 1, kv_head(h)]
        acc_ref[h] = acc
    if one_head:
        for i in range(g):
            hc = jnp.minimum(hd_ref[b, ki * g + i], N_HEADS - 1)
            prow = _sublane_total(jnp.where(sub == hc, probs[i], 0.0))[0:1, :]
            acc_ref[hc] = acc_ref[hc] + prow * kv_refs[i][0, 0, 1, hc]

    @pl.when(ki == nk - 1)
    def _():
        w = hk * HEAD_DIM
        own = self_ref[0]
        sub1 = lax.broadcasted_iota(I32, (SUB, 1), 0)
        s_own = jnp.full((SUB, 1), NEG, F32)
        for h in range(N_HEADS):
            kh = own[0:1, kv_head(h) * HEAD_DIM:(kv_head(h) + 1) * HEAD_DIM]
            qh = q_ref[0][0:1, h * HEAD_DIM:(h + 1) * HEAD_DIM]
            s_own = jnp.where(sub1 == h, jnp.sum(qh * kh, axis=1, keepdims=True), s_own)
        m_fin = m_ref[:, 0:1]
        m_tot = jnp.maximum(m_fin, s_own)
        a_fin = jnp.exp(m_fin - m_tot)
        p_own = jnp.exp(s_own - m_tot)
        l_tot = a_fin * l_ref[:, 0:1] + p_own
        outs = []
        for h in range(N_HEADS):
            vcol = _col_from_row(own[0:1, w + kv_head(h) * HEAD_DIM:w + (kv_head(h) + 1) * HEAD_DIM],
                                 HEAD_DIM)
            tot = a_fin[h:h + 1, :] * jnp.sum(acc_ref[h], axis=1, keepdims=True) + p_own[h:h + 1, :] * vcol
            outs.append(_row_from_col(tot / jnp.maximum(l_tot[h:h + 1, :], 1e-30), HEAD_DIM))
        o_ref[0] = jnp.broadcast_to(jnp.concatenate(outs, axis=1), (SUB, HEADS_W))


def _kt_attn_call(q8, cache_t, layer, page_table, fetch, heads, self8, *, r, g, lo=0, one_head=False,
                  logf_t=None, logf_new=None, name="kt_attn"):
    bsz = q8.shape[0]
    _, _, _, hk, _, pw = cache_t.shape
    nf = fetch.shape[1]
    assert nf % g == 0
    nk = nf // g
    n_pages = page_table.shape[1]
    bias = logf_t is not None
    cfg = dict(g=g, nk=nk, hk=hk, pw=pw, r=r, lo=lo, bias=bias, one_head=one_head)

    pages = jnp.take_along_axis(page_table, jnp.clip(fetch * r // pw, 0, n_pages - 1), axis=1)
    kv_map = lambda i: (lambda b, ki, pg, ft, hd: (layer, pg[b, ki * g + i], 0, 0, 0, 0))
    lf_map = lambda i: (lambda b, ki, pg, ft, hd: (layer, pg[b, ki * g + i], 0, 0))
    fixed = lambda shape: pl.BlockSpec(shape, lambda b, ki, pt, ft, hd: (b, 0, 0))
    args = [q8] + [cache_t] * g + [self8]
    specs = ([fixed((1, SUB, HEADS_W))]
             + [pl.BlockSpec((1, 1, 2, hk, HEAD_DIM, pw), kv_map(i)) for i in range(g)]
             + [fixed((1, SUB, self8.shape[2]))])
    scratch = [pltpu.VMEM((SUB, pw), F32), pltpu.VMEM((SUB, pw), F32),
               pltpu.VMEM((N_HEADS, HEAD_DIM, pw), F32), pltpu.VMEM((N_HEADS, HEAD_DIM, pw), F32)]
    if bias:
        args += [logf_t] * g + [logf_new]
        specs += [pl.BlockSpec((1, 1, N_HEADS, pw), lf_map(i)) for i in range(g)] + [fixed((1, SUB, LANE))]
        scratch.append(pltpu.VMEM((SUB, pw), F32))
    return pl.pallas_call(
        functools.partial(_kt_attn_kernel, cfg=cfg),
        grid_spec=pltpu.PrefetchScalarGridSpec(
            num_scalar_prefetch=3, grid=(bsz, nk), in_specs=specs,
            out_specs=fixed((1, SUB, HEADS_W)), scratch_shapes=scratch),
        out_shape=jax.ShapeDtypeStruct((bsz, SUB, HEADS_W), F32),
        compiler_params=_cparams(("parallel", "arbitrary")),
        name=name,
    )(pages, fetch, heads, *args)


def _kt_gate_kernel(pt_ref, q_ref, *refs, g, nk, k_top, n_blk):
    k_refs = refs[:g]
    idx_ref, qb_ref, gate_ref = refs[g:]
    ki = pl.program_id(1)
    sub = lax.broadcasted_iota(I32, (SUB, LANE), 0)
    lane = lax.broadcasted_iota(I32, (SUB, LANE), 1)
    ppb = MOBA_BLOCK // PAGE

    @pl.when(ki == 0)
    def _():
        gate_ref[...] = jnp.zeros(gate_ref.shape, F32)
        for h in range(N_HEADS):
            qcol = _col_from_row(q_ref[0][0:1, h * HEAD_DIM:(h + 1) * HEAD_DIM], HEAD_DIM)
            qb_ref[h] = jnp.broadcast_to(qcol, (HEAD_DIM, LANE))

    gate = gate_ref[...]
    for i in range(g):
        s = jnp.zeros((SUB, LANE), F32)
        for h in range(N_HEADS):
            s = jnp.where(sub == h, _query_page_logits(qb_ref[h], k_refs[i][0, 0, 0, h]), s)
        tot = jnp.sum(s, axis=1, keepdims=True) * (1.0 / MOBA_BLOCK)
        gate = gate + jnp.where(lane == (ki * g + i) // ppb, tot, 0.0)
    gate_ref[...] = gate

    @pl.when(ki == nk - 1)
    def _():
        score = jnp.where(lane < n_blk, gate, NEG)
        picks = jnp.full((SUB, LANE), -1, I32)
        for i in range(k_top):
            m, idx = _first_argmax(score, lane)
            hit = lane == idx
            picks = jnp.where(jnp.logical_and(lane == i, m > 0.5 * NEG), idx, picks)
            score = jnp.where(hit, -jnp.inf, score)
        idx_ref[0] = picks


def _kt_gate_call(q8, cache_t, layer, page_table, *, g, k_top):
    bsz = q8.shape[0]
    n_pages = page_table.shape[1]
    assert n_pages % g == 0 and n_pages * PAGE % MOBA_BLOCK == 0
    nk = n_pages // g
    n_blk = n_pages * PAGE // MOBA_BLOCK
    assert n_blk <= LANE
    k_map = lambda i: (lambda b, ki, pt: (layer, pt[b, ki * g + i], 0, 0, 0, 0))
    return pl.pallas_call(
        functools.partial(_kt_gate_kernel, g=g, nk=nk, k_top=k_top, n_blk=n_blk),
        grid_spec=pltpu.PrefetchScalarGridSpec(
            num_scalar_prefetch=1, grid=(bsz, nk),
            in_specs=[pl.BlockSpec((1, SUB, HEADS_W), lambda b, ki, pt: (b, 0, 0))]
            + [pl.BlockSpec((1, 1, 1, N_HEADS, HEAD_DIM, PAGE), k_map(i)) for i in range(g)],
            out_specs=pl.BlockSpec((1, SUB, LANE), lambda b, ki, pt: (b, 0, 0)),
            scratch_shapes=[pltpu.VMEM((N_HEADS, HEAD_DIM, LANE), F32), pltpu.VMEM((SUB, LANE), F32)]),
        out_shape=jax.ShapeDtypeStruct((bsz, SUB, LANE), I32),
        compiler_params=_cparams(("parallel", "arbitrary")),
        name="moba_gate_s",
    )(page_table, q8, *([cache_t] * g))


def _kt_cmp_kernel(pt_ref, *refs, g, nk):
    kv_refs = refs[:g]
    wc_ref, pe_ref, w2_ref, g_ref, out_ref, rows_ref = refs[g:]
    ki = pl.program_id(1)
    for i in range(g):
        off = pl.multiple_of((ki * g + i) * PAGE, PAGE)
        rows_ref[pl.ds(off, PAGE), :] = kv_refs[i][0, 0].reshape(2 * HEAD_DIM, PAGE).T

    @pl.when(ki == nk - 1)
    def _():
        n16 = rows_ref.shape[0] // NSA_CMP_STRIDE
        hw = 2 * NSA_CMP_HIDDEN
        h = jnp.zeros((n16, 2 * hw), F32)
        c = jnp.zeros((SUB, 2 * hw), F32)
        for j in range(NSA_CMP_STRIDE // 2):
            pair = [rows_ref[pl.ds(2 * j + i, n16, stride=NSA_CMP_STRIDE), :] for i in range(2)]
            h = h + _mm(jnp.concatenate(pair, axis=1), wc_ref[j], True)
            c = c + _mm(pe_ref[j], wc_ref[j], True)
        c = c[0:1, :hw] + c[1:2, hw:]
        pre = h[:, :hw] + pltpu.roll(h[:, hw:], n16 - 1, 0) + c
        act = 0.5 * pre * (1.0 + jnp.tanh(math.sqrt(2.0 / math.pi) * (pre + 0.044715 * pre * pre * pre)))
        y = _mm(act, w2_ref[...], True)
        lane = lax.broadcasted_iota(I32, (1, LANE), 1)
        yn = y * lax.rsqrt(_head_sumsq(y, True) * (1.0 / HEAD_DIM) + EPS) * g_ref[...]
        out_ref[0] = jnp.where(lane < HEAD_DIM, yn, y)


def _kt_cmp_call(cache_t, layer, page_table, wab, pe2, w2, gkc, *, g):
    bsz, n_pages = page_table.shape
    assert n_pages % g == 0
    nk = n_pages // g
    n16 = n_pages * PAGE // NSA_CMP_STRIDE
    wc = wab.reshape(NSA_CMP_STRIDE // 2, 2 * LANE, wab.shape[1])
    pe = jnp.swapaxes(pe2[:2].reshape(2, NSA_CMP_STRIDE // 2, 2 * LANE), 0, 1)
    pe = jnp.pad(pe, ((0, 0), (0, SUB - 2), (0, 0)))
    full = lambda a: pl.BlockSpec(a.shape, lambda b, ki, pt: (0,) * a.ndim)
    kv_map = lambda i: (lambda b, ki, pt: (layer, pt[b, ki * g + i], 0, 0, 0))
    return pl.pallas_call(
        functools.partial(_kt_cmp_kernel, g=g, nk=nk),
        grid_spec=pltpu.PrefetchScalarGridSpec(
            num_scalar_prefetch=1, grid=(bsz, nk),
            in_specs=[pl.BlockSpec((1, 1, 2, HEAD_DIM, PAGE), kv_map(i)) for i in range(g)]
            + [full(wc), full(pe), full(w2), full(gkc)],
            out_specs=pl.BlockSpec((1, n16, LANE), lambda b, ki, pt: (b, 0, 0)),
            scratch_shapes=[pltpu.VMEM((n_pages * PAGE, LANE), F32)]),
        out_shape=jax.ShapeDtypeStruct((bsz, n16, LANE), F32),
        compiler_params=_cparams(("parallel", "arbitrary")),
        name="cmp_mlp_s",
    )(page_table, *([cache_t] * g), wc, pe, w2, gkc)


MOBA_BLOCKS_PER_STEP = 8


def _kmean_kernel(kv_ref, out_ref):
    row = lax.broadcasted_iota(I32, (MOBA_BLOCKS_PER_STEP, 1), 0)
    acc = jnp.zeros((MOBA_BLOCKS_PER_STEP, HEADS_W), F32)
    for j in range(MOBA_BLOCKS_PER_STEP):
        s = jnp.sum(kv_ref[0, j * MOBA_BLOCK:(j + 1) * MOBA_BLOCK, :], axis=0, keepdims=True)
        acc = acc + jnp.where(row == j, s, 0.0)
    out_ref[0] = acc * (1.0 / MOBA_BLOCK)


def _kmean_call(kv):
    bsz, tk, _ = kv.shape
    span = MOBA_BLOCKS_PER_STEP * MOBA_BLOCK
    assert tk % span == 0
    return pl.pallas_call(
        _kmean_kernel, grid=(bsz, tk // span),
        in_specs=[pl.BlockSpec((1, span, HEADS_W), lambda b, j: (b, j, 0))],
        out_specs=pl.BlockSpec((1, MOBA_BLOCKS_PER_STEP, HEADS_W), lambda b, j: (b, j, 0)),
        out_shape=jax.ShapeDtypeStruct((bsz, tk // MOBA_BLOCK, HEADS_W), F32),
        compiler_params=_cparams(("parallel", "parallel")), name="moba_kmean")(kv)


def _moba_gate_kernel(q_ref, km_ref, bm_ref, *, tq, k_top, precise):
    qi = pl.program_id(1)
    qpos = qi * tq + lax.broadcasted_iota(I32, (tq, 1), 0)
    cur = qpos // MOBA_BLOCK
    blk = lax.broadcasted_iota(I32, (1, LANE), 1)
    valid = blk < cur
    q = q_ref[0]
    km = km_ref[0]
    for h in range(N_HEADS):
        kmh = km[:, h * HEAD_DIM:(h + 1) * HEAD_DIM]
        qh = q[:, h * HEAD_DIM:(h + 1) * HEAD_DIM]
        if precise:
            gate = _mm_nt(qh, kmh, True)
        else:
            hi, lo = _split2(kmh)
            gate = _dot_nt(qh, hi) + _dot_nt(qh, lo)
        score = jnp.where(valid, gate, NEG)
        sel = jnp.where(blk == cur, 1.0, 0.0)
        for _ in range(k_top):
            m, idx = _first_argmax(score, blk)
            hit = blk == idx
            sel = jnp.where(jnp.logical_and(hit, m > 0.5 * NEG), 1.0, sel)
            score = jnp.where(hit, -jnp.inf, score)
        bm_ref[0, :, h * LANE:(h + 1) * LANE] = sel


def _moba_gate_call(q, kmean, *, tq, k_top, precise=False):
    bsz, t, _ = q.shape
    return pl.pallas_call(
        functools.partial(_moba_gate_kernel, tq=tq, k_top=k_top, precise=precise),
        grid=(bsz, t // tq),
        in_specs=[pl.BlockSpec((1, tq, HEADS_W), lambda b, i: (b, i, 0)),
                  pl.BlockSpec((1, LANE, HEADS_W), lambda b, i: (b, 0, 0))],
        out_specs=pl.BlockSpec((1, tq, N_HEADS * LANE), lambda b, i: (b, i, 0)),
        out_shape=jax.ShapeDtypeStruct((bsz, t, N_HEADS * LANE), F32),
        compiler_params=_cparams(("parallel", "parallel")),
        name="moba_gate",
    )(q, kmean)


CUM_CHUNK = 512


def _cum_kernel(l_ref, o_ref, carry_ref):
    @pl.when(pl.program_id(1) == 0)
    def _():
        carry_ref[...] = jnp.zeros(carry_ref.shape, F32)

    r = lax.broadcasted_iota(I32, (CUM_CHUNK, CUM_CHUNK), 0)
    c = lax.broadcasted_iota(I32, (CUM_CHUNK, CUM_CHUNK), 1)
    u = jnp.where(r <= c, 1.0, 0.0).astype(BF16)
    h1, h2, h3 = _split3(l_ref[0])
    cum = _dot(h1, u) + _dot(h2, u) + _dot(h3, u) + carry_ref[:, 0:1]
    o_ref[0] = cum
    carry_ref[...] = jnp.broadcast_to(cum[:, CUM_CHUNK - 1:CUM_CHUNK], carry_ref.shape)


def _cum_call(logf_t):
    bsz, rows, tk = logf_t.shape
    assert rows == SUB and tk % CUM_CHUNK == 0
    return pl.pallas_call(
        _cum_kernel,
        grid=(bsz, tk // CUM_CHUNK),
        in_specs=[pl.BlockSpec((1, SUB, CUM_CHUNK), lambda b, j: (b, 0, j))],
        out_specs=pl.BlockSpec((1, SUB, CUM_CHUNK), lambda b, j: (b, 0, j)),
        out_shape=jax.ShapeDtypeStruct(logf_t.shape, F32),
        scratch_shapes=[pltpu.VMEM((SUB, LANE), F32)],
        compiler_params=_cparams(("parallel", "arbitrary")),
        name="fox_cum",
    )(logf_t)


def _group_norm_gate(y, z, norm):
    y = y * _silu(z)
    gw = SSD_INNER // SSD_GROUPS
    outs = []
    for g in range(SSD_GROUPS):
        yg = y[:, g * gw:(g + 1) * gw]
        ms = jnp.mean(yg * yg, axis=-1, keepdims=True)
        outs.append(yg * lax.rsqrt(ms + EPS) * norm[:, g * gw:(g + 1) * gw])
    return jnp.concatenate(outs, axis=-1)


SSD_SEQS_PER_STEP = 2


def _ssd_kernel(xbc_ref, z_ref, dt_ref, convw_ref, convb_ref, hp_ref, norm_ref, y_ref, state_ref,
                ext_ref, h_ref, *, nc, precise, g):
    for s in range(g):
        one = pl.ds(s, 1)
        _ssd_body(xbc_ref.at[one], z_ref.at[one], dt_ref.at[one], convw_ref, convb_ref, hp_ref, norm_ref,
                  y_ref.at[one], state_ref.at[one], ext_ref.at[s], h_ref.at[s], nc=nc, precise=precise)


def _ssd_body(xbc_ref, z_ref, dt_ref, convw_ref, convb_ref, hp_ref, norm_ref, y_ref, state_ref,
              ext_ref, h_ref, *, nc, precise):
    q = SSD_CHUNK
    ci = pl.program_id(1)

    @pl.when(ci == 0)
    def _():
        ext_ref[0:SUB, :] = jnp.zeros((SUB, SSD_CONV_DIM), F32)
        h_ref[...] = jnp.zeros(h_ref.shape, F32)

    u = xbc_ref[0]
    ext_ref[SUB:SUB + q, :] = u
    acc = jnp.broadcast_to(convb_ref[...], (q, SSD_CONV_DIM))
    for i in range(SSD_CONV):
        k = SSD_CONV - 1 - i
        acc = acc + ext_ref[SUB - k:SUB - k + q, :] * convw_ref[i:i + 1, :]
    ext_ref[0:SUB, :] = u[q - SUB:, :]
    xbc = _silu(acc)
    xs = xbc[:, :SSD_INNER]
    dt = dt_ref[0]
    a = -jnp.exp(hp_ref[0:1, :])
    d_skip = hp_ref[1:2, :]
    row = lax.broadcasted_iota(I32, (q, 1), 0)
    cum = dt * a
    s = 1
    while s < q:
        cum = cum + jnp.where(row >= s, pltpu.roll(cum, s, 0), 0.0)
        s *= 2
    cum_t = cum.T
    tri = lax.broadcasted_iota(I32, (q, q), 0) >= lax.broadcasted_iota(I32, (q, q), 1)
    ys, xdd = [], []
    cbs = []
    for g in range(SSD_GROUPS):
        bm = xbc[:, SSD_INNER + g * SSD_STATE:SSD_INNER + (g + 1) * SSD_STATE]
        cm = xbc[:, SSD_INNER + SSD_BC + g * SSD_STATE:SSD_INNER + SSD_BC + (g + 1) * SSD_STATE]
        cbs.append((bm, cm, _mm_nt(cm, bm, precise)))
    rep = SSD_HEADS // SSD_GROUPS
    for h in range(SSD_HEADS):
        bm, cm, cb = cbs[h // rep]
        ch = cum[:, h:h + 1]
        lmat = jnp.where(tri, jnp.exp(ch - cum_t[h:h + 1, :]), 0.0)
        xh = xs[:, h * SSD_HEAD_DIM:(h + 1) * SSD_HEAD_DIM]
        xdt = xh * dt[:, h:h + 1]
        y = _mm(cb * lmat, xdt, precise)
        y = y + jnp.exp(ch) * _mm_nt(cm, h_ref[h], precise)
        ys.append(y + d_skip[:, h:h + 1] * xh)
        xdd.append(xdt * jnp.exp(cum[q - 1:q, h:h + 1] - ch))
    xdd_t = jnp.concatenate(xdd, axis=-1).T
    for h in range(SSD_HEADS):
        bm = cbs[h // rep][0]
        upd = _mm(xdd_t[h * SSD_HEAD_DIM:(h + 1) * SSD_HEAD_DIM, :], bm, precise)
        h_ref[h] = jnp.exp(cum[q - 1:q, h:h + 1]) * h_ref[h] + upd
    y_ref[0] = _group_norm_gate(jnp.concatenate(ys, axis=-1), z_ref[0], norm_ref[...])

    @pl.when(ci == nc - 1)
    def _():
        state_ref[0] = h_ref[...]


def _ssd_call(xbc, z, dt, convw, convb, hp, norm, precise=False):
    bsz, t, _ = xbc.shape
    nc = t // SSD_CHUNK
    g = SSD_SEQS_PER_STEP if bsz % SSD_SEQS_PER_STEP == 0 else 1
    blk = lambda w: pl.BlockSpec((g, SSD_CHUNK, w), lambda b, c: (b, c, 0))
    full = lambda a: pl.BlockSpec(a.shape, lambda b, c: (0,) * a.ndim)
    return pl.pallas_call(
        functools.partial(_ssd_kernel, nc=nc, precise=precise, g=g),
        grid=(bsz // g, nc),
        in_specs=[blk(SSD_CONV_DIM), blk(SSD_INNER), blk(LANE), full(convw), full(convb), full(hp),
                  full(norm)],
        out_specs=[blk(SSD_INNER),
                   pl.BlockSpec((g, SSD_HEADS, SSD_HEAD_DIM, SSD_STATE), lambda b, c: (b, 0, 0, 0))],
        out_shape=[jax.ShapeDtypeStruct((bsz, t, SSD_INNER), F32),
                   jax.ShapeDtypeStruct((bsz, SSD_HEADS, SSD_HEAD_DIM, SSD_STATE), F32)],
        scratch_shapes=[pltpu.VMEM((g, SUB + SSD_CHUNK, SSD_CONV_DIM), F32),
                        pltpu.VMEM((g, SSD_HEADS, SSD_HEAD_DIM, SSD_STATE), F32)],
        compiler_params=_cparams(("parallel", "arbitrary")),
        name="ssd_scan",
    )(xbc, z, dt, convw, convb, hp, norm)


def _ssd_step_kernel(full_ref, z_ref, dt_ref, st_ref, w8_ref, convb_ref, hp_ref, norm_ref,
                     y_ref, state_ref):
    acc = convb_ref[...] + jnp.sum(full_ref[0] * w8_ref[...], axis=0, keepdims=True)
    xbc = _silu(acc)
    xs = xbc[:, :SSD_INNER]
    dt = dt_ref[0]
    a = -jnp.exp(hp_ref[0:1, :])
    d_skip = hp_ref[1:2, :]
    n = SSD_HEAD_DIM
    eye = lax.broadcasted_iota(I32, (n, n), 0) == lax.broadcasted_iota(I32, (n, n), 1)
    rep = SSD_HEADS // SSD_GROUPS
    ys = []
    for h in range(SSD_HEADS):
        g = h // rep
        bm = xbc[:, SSD_INNER + g * SSD_STATE:SSD_INNER + (g + 1) * SSD_STATE]
        cm = xbc[:, SSD_INNER + SSD_BC + g * SSD_STATE:SSD_INNER + SSD_BC + (g + 1) * SSD_STATE]
        dth = dt[:, h:h + 1]
        xh = xs[:, h * n:(h + 1) * n]
        xcol = jnp.sum(jnp.where(eye, jnp.broadcast_to(xh * dth, (n, n)), 0.0), axis=1, keepdims=True)
        hn = jnp.exp(dth * a[:, h:h + 1]) * st_ref[0, h] + xcol * bm
        state_ref[0, h] = hn
        ycol = jnp.sum(hn * cm, axis=1, keepdims=True)
        yrow = jnp.sum(jnp.where(eye, jnp.broadcast_to(ycol, (n, n)), 0.0), axis=0, keepdims=True)
        ys.append(yrow + d_skip[:, h:h + 1] * xh)
    y_ref[0] = _group_norm_gate(jnp.concatenate(ys, axis=-1), z_ref[0], norm_ref[...])


def _ssd_step_call(full8, z, dt, state, w8, convb, hp, norm):
    bsz = full8.shape[0]
    one = lambda w: pl.BlockSpec((1, 1, w), lambda b: (b, 0, 0))
    full = lambda a: pl.BlockSpec(a.shape, lambda b: (0,) * a.ndim)
    st = pl.BlockSpec((1, SSD_HEADS, SSD_HEAD_DIM, SSD_STATE), lambda b: (b, 0, 0, 0))
    return pl.pallas_call(
        _ssd_step_kernel,
        grid=(bsz,),
        in_specs=[pl.BlockSpec((1, SUB, SSD_CONV_DIM), lambda b: (b, 0, 0)), one(SSD_INNER), one(LANE),
                  st, full(w8), full(convb), full(hp), full(norm)],
        out_specs=[one(SSD_INNER), st],
        out_shape=[jax.ShapeDtypeStruct((bsz, 1, SSD_INNER), F32),
                   jax.ShapeDtypeStruct(state.shape, F32)],
        compiler_params=_cparams(("parallel",)),
        name="ssd_step",
    )(full8, z, dt, state, w8, convb, hp, norm)


R_GROUP0 = MOE_EXPERTS


def _post_kernel(*refs, mode, nt):
    is_tail = (pl.program_id(0) % nt) == nt - 1
    _by_tile(mode, is_tail, lambda precise: _post_body(*refs, precise=precise))


def _post_body(x_ref, gmix_ref, wg_ref, small_ref, ocmp_ref, oslc_ref, owin_ref, omoba_ref,
               ofox_ref, ossd_ref, wbr_ref, wout_ref, gffn_ref, wr_ref, br_ref,
               x1_ref, xn2_ref, comb_ref, *, precise):
    x = x_ref[...]
    xn = x * lax.rsqrt(jnp.mean(x * x, axis=-1, keepdims=True) + EPS) * gmix_ref[...]
    xn = xn if precise else xn.astype(BF16)
    every = slice(None)
    r = lax.broadcasted_iota(I32, (LANE, 3 * HEADS_W), 0)
    c = lax.broadcasted_iota(I32, (LANE, 3 * HEADS_W), 1)
    pick = r == SM_NSAG + 3 * ((c % HEADS_W) // HEAD_DIM) + c // HEADS_W
    pick = jnp.where(pick, 1.0, 0.0).astype(BF16)
    gexp = _dot_exact(_sigmoid(small_ref[...]), pick)
    o_nsa = (gexp[:, :HEADS_W] * ocmp_ref[...] + gexp[:, HEADS_W:2 * HEADS_W] * oslc_ref[...]
             + gexp[:, 2 * HEADS_W:] * owin_ref[...])
    merged = jnp.zeros(x.shape, F32)
    off = 0
    for b, o in enumerate((o_nsa, omoba_ref[...], ofox_ref[...], ossd_ref[...])):
        w = o.shape[1]
        gate = _sigmoid(_wmm(xn, wg_ref, (every, slice(b * D_MODEL, (b + 1) * D_MODEL)), precise))
        merged = merged + gate * _wmm(o, wbr_ref, (slice(off, off + w), every), precise)
        off += w
    x1 = x + _wmm(merged, wout_ref, (every, every), precise)
    x1_ref[...] = x1
    xn2 = x1 * lax.rsqrt(jnp.mean(x1 * x1, axis=-1, keepdims=True) + EPS) * gffn_ref[...]
    xn2_ref[...] = xn2.astype(xn2_ref.dtype)
    logits = _wmm(xn2, wr_ref, (every, every), True) + br_ref[...]
    lane = lax.broadcasted_iota(I32, (1, LANE), 1)
    is_grp = jnp.logical_and(lane >= R_GROUP0, lane < R_GROUP0 + MOE_GROUPS)
    gmax, gidx = _first_argmax(jnp.where(is_grp, logits, -jnp.inf), lane)
    g_w = 1.0 / jnp.sum(jnp.where(is_grp, jnp.exp(logits - gmax), 0.0), axis=-1, keepdims=True)
    in_grp = lane // MOE_PER_GROUP == gidx - R_GROUP0
    e1 = jnp.where(in_grp, logits, -jnp.inf)
    v1, i1 = _first_argmax(e1, lane)
    e2 = jnp.where(lane == i1, -jnp.inf, e1)
    v2, i2 = _first_argmax(e2, lane)
    t = jnp.exp(v2 - v1)
    w1 = 1.0 / (1.0 + t)
    comb_ref[...] = (jnp.where(lane == i1, w1 * g_w, 0.0) + jnp.where(lane == i2, t * w1 * g_w, 0.0))


def _post_call(x2, gmix, wg, small, o_cmp, o_slc, o_win, o_moba, o_fox, o_ssd, wbr, wout, gffn,
               wr, br, tm, mode, nt):
    n = x2.shape[0]
    row = lambda a: pl.BlockSpec((tm, a.shape[1]), lambda i: (i, 0))
    full = lambda a: pl.BlockSpec(a.shape, lambda i: (0,) * a.ndim)
    big = (lambda a: pl.BlockSpec(a.shape, lambda i: (0,) * a.ndim, pipeline_mode=pl.Buffered(1))
           ) if mode != 'fast' else full
    rows = (x2, small, o_cmp, o_slc, o_win, o_moba, o_fox, o_ssd)
    return pl.pallas_call(
        functools.partial(_post_kernel, mode=mode, nt=nt),
        grid=(n // tm,),
        in_specs=[row(x2), full(gmix), big(wg)] + [row(a) for a in rows[1:]]
        + [big(wbr), big(wout), full(gffn), full(wr), full(br)],
        out_specs=[pl.BlockSpec((tm, D_MODEL), lambda i: (i, 0)),
                   pl.BlockSpec((tm, D_MODEL), lambda i: (i, 0)),
                   pl.BlockSpec((tm, LANE), lambda i: (i, 0))],
        out_shape=[jax.ShapeDtypeStruct((n, D_MODEL), F32),
                   jax.ShapeDtypeStruct((n, D_MODEL), BF16 if mode == 'fast' else F32),
                   jax.ShapeDtypeStruct((n, LANE), F32)],
        compiler_params=_cparams(("parallel",)),
        name="merge_out",
    )(x2, gmix, wg, small, o_cmp, o_slc, o_win, o_moba, o_fox, o_ssd, wbr, wout, gffn, wr, br)


def _moe_kernel(xn_ref, x1_ref, comb_ref, wg_ref, wu_ref, wd_ref, o_ref, acc_ref, *, precise):
    e = pl.program_id(1)

    @pl.when(e == 0)
    def _():
        acc_ref[...] = x1_ref[...]

    lane = lax.broadcasted_iota(I32, (1, LANE), 1)
    cw = jnp.sum(jnp.where(lane == e, comb_ref[...], 0.0), axis=-1, keepdims=True)
    xn = xn_ref[...]
    h = _silu(_mm(xn, wg_ref[0], precise)) * _mm(xn, wu_ref[0], precise)
    acc_ref[...] += cw * _mm(h, wd_ref[0], precise)

    @pl.when(e == MOE_EXPERTS - 1)
    def _():
        o_ref[...] = acc_ref[...]


def _moe_call(xn2, x1, comb, wg, wu, wd, tm, precise=False):
    n = xn2.shape[0]
    return pl.pallas_call(
        functools.partial(_moe_kernel, precise=precise),
        grid=(n // tm, MOE_EXPERTS),
        in_specs=[pl.BlockSpec((tm, D_MODEL), lambda i, e: (i, 0)),
                  pl.BlockSpec((tm, D_MODEL), lambda i, e: (i, 0)),
                  pl.BlockSpec((tm, LANE), lambda i, e: (i, 0)),
                  pl.BlockSpec((1, D_MODEL, MOE_HIDDEN), lambda i, e: (e, 0, 0)),
                  pl.BlockSpec((1, D_MODEL, MOE_HIDDEN), lambda i, e: (e, 0, 0)),
                  pl.BlockSpec((1, MOE_HIDDEN, D_MODEL), lambda i, e: (e, 0, 0))],
        out_specs=pl.BlockSpec((tm, D_MODEL), lambda i, e: (i, 0)),
        out_shape=jax.ShapeDtypeStruct((n, D_MODEL), F32),
        scratch_shapes=[pltpu.VMEM((tm, D_MODEL), F32)],
        compiler_params=_cparams(("parallel", "arbitrary")),
        name="moe_experts",
    )(xn2, x1, comb, wg, wu, wd)


def _layer_weights(prm):
    wp, wgate = _pack_w_in(prm['w_in'])
    wr = jnp.concatenate([prm['w_router_exp'], prm['w_router_grp'],
                          jnp.zeros((D_MODEL, LANE - MOE_EXPERTS - MOE_GROUPS), F32)], axis=1)
    br = jnp.concatenate([prm['b_router_exp'], prm['b_router_grp'],
                          jnp.zeros((LANE - MOE_EXPERTS - MOE_GROUPS,), F32)])[None, :]
    pad = lambda v: jnp.concatenate([v.astype(F32), jnp.zeros((LANE - v.shape[0],), F32)])
    hp = jnp.stack([pad(prm['ssd_a_log']), pad(prm['ssd_d'])] + [jnp.zeros((LANE,), F32)] * 6)
    w8 = jnp.concatenate([prm['ssd_conv_w'], jnp.zeros((SUB - SSD_CONV, SSD_CONV_DIM), F32)], axis=0)
    wbr = jnp.concatenate([prm['w_br_nsa'], prm['w_br_moba'], prm['w_br_fox'], prm['w_br_ssd']], axis=0)
    shared = dict(gmix=prm['norm_mix'][None, :], gffn=prm['norm_ffn'][None, :], gains=_proj_gains(prm),
                  biases=_proj_biases(prm), br=br, hp=hp, w8=w8, convw=prm['ssd_conv_w'],
                  convb=prm['ssd_conv_b'][None, :], norm=prm['ssd_norm'][None, :],
                  wr=_split_w(wr, True))
    big = lambda precise: dict(wp=_split_w(wp, precise), wgate=_split_w(wgate, precise),
                               wbr=_split_w(wbr, precise), wout=_split_w(prm['w_out'], precise))
    fast = dict(shared, **big(False), cmp=_cmp_weights(prm, BF16),
                weg=prm['w_exp_gate'].astype(BF16), weu=prm['w_exp_up'].astype(BF16),
                wed=prm['w_exp_down'].astype(BF16))
    exact = dict(shared, **big(True), cmp=_cmp_weights(prm, F32),
                 weg=prm['w_exp_gate'], weu=prm['w_exp_up'], wed=prm['w_exp_down'])
    return fast, exact


def _pad_rows(a, rows):
    return jnp.pad(a, ((0, 0), (0, rows - a.shape[1])) + ((0, 0),) * (a.ndim - 2))


def _heads_to_sublanes(logf, tk_pad):
    lt = jnp.swapaxes(logf, 1, 2)
    return jnp.pad(lt, ((0, 0), (0, SUB - lt.shape[1]), (0, tk_pad - lt.shape[2])))


def _layer_prompt(x, w, w_ffn, *, tm, tq, tk, mode):
    precise = mode != 'fast'
    bsz, t, _ = x.shape
    n = bsz * t
    x2 = x.reshape(n, D_MODEL)
    tabs = _rope_tables(jnp.arange(t, dtype=I32))
    (qn, cmp_rows, slc_rows, win_rows, qm, moba_rows, qf, fox_rows, z, xbc, small, dt) = _proj_call(
        x2, w['gmix'], w['wp'], w['gains'], tabs, w['biases'], tm, mode)
    b3 = lambda a: a.reshape(bsz, t, a.shape[-1])
    qn, qm, qf = b3(qn), b3(qm), b3(qf)
    kvc = _cmp_mlp_call(cmp_rows.reshape(bsz, t // NSA_CMP_STRIDE, NSA_CMP_STRIDE * LANE), *w['cmp'],
                        precise=precise)
    n_cmp = (t - NSA_CMP_LEN) // NSA_CMP_STRIDE + 1
    n_sel = -(-t // NSA_SEL_BLOCK)
    nselp = -(-n_sel // LANE) * LANE
    o_cmp, sel, _ = _nsa_cmp_call(qn, kvc, tq=tq, q_pos0=0, n_cmp=n_cmp, n_sel=n_sel, nselp=nselp,
                                  precise=precise)
    o_slc = _attn_call(qn, b3(slc_rows), tq=tq, tk=tk, bm=sel, bm_bs=NSA_SEL_BLOCK, mode=mode,
                       name="nsa_slc")
    tw = min(tq, tk, NSA_WINDOW // 2)
    o_win = _attn_call(qn, b3(win_rows), tq=tw, tk=tw, window=NSA_WINDOW, mode=mode, name="nsa_win")
    moba3 = b3(moba_rows)
    kmean = _pad_rows(_kmean_call(moba3), LANE)
    n_blk = -(-t // MOBA_BLOCK)
    bm = _moba_gate_call(qm, kmean, tq=tq, k_top=min(MOBA_TOPK, n_blk - 1), precise=precise)
    o_moba = _attn_call(qm, moba3, tq=tq, tk=tk, bm=bm, bm_bs=MOBA_BLOCK, bm_hs=LANE, mode=mode,
                        name="moba")
    logf = small[:, SM_FOXF:SM_FOXF + N_HEADS].reshape(bsz, t, N_HEADS)
    cum_t = _cum_call(_heads_to_sublanes(logf, t))
    cq = jnp.pad(jnp.swapaxes(cum_t, 1, 2), ((0, 0), (0, 0), (0, LANE - SUB)))
    o_fox = _attn_call(qf, b3(fox_rows), tq=tq, tk=tk, cq=cq, cum_t=cum_t, mode=mode, name="fox")
    y_ssd, ssd_state = _ssd_call(b3(xbc), b3(z), b3(dt), w['convw'], w['convb'], w['hp'], w['norm'],
                                 precise)
    f2 = lambda a: a.reshape(n, a.shape[-1])
    x1, xn2, comb = _post_call(x2, w['gmix'], w['wgate'], small, f2(o_cmp), f2(o_slc), f2(o_win),
                               f2(o_moba), f2(o_fox), f2(y_ssd), w['wbr'], w['wout'], w['gffn'],
                               w['wr'], w['br'], tm, mode, t // tm)
    y = _moe_call(xn2, x1, comb, w_ffn['weg'], w_ffn['weu'], w_ffn['wed'], min(n, 4 * tm))
    wb = min(NSA_WINDOW, t)
    state = (cmp_rows.reshape(bsz, t, 2, HEAD_DIM), slc_rows.reshape(bsz, t, 2, HEAD_DIM),
             win_rows.reshape(bsz, t, 2, HEAD_DIM)[:, t - wb:],
             moba_rows.reshape(bsz, t, 2, N_HEADS, HEAD_DIM), fox_rows.reshape(bsz, t, 2, N_HEADS, HEAD_DIM),
             logf, ssd_state, b3(xbc)[:, t - (SSD_CONV - 1):])
    return y.reshape(bsz, t, D_MODEL), state


def _cache_views(cache_nsa_cmp, cache_nsa_slc, state_nsa_win, cache_moba, cache_fox, cache_fox_logf):
    rows_last = lambda c: jnp.moveaxis(c, 2, -1)
    return (rows_last(cache_nsa_cmp), rows_last(cache_nsa_slc)[:, :, :, None],
            rows_last(state_nsa_win)[:, :, :, None], rows_last(cache_moba), rows_last(cache_fox),
            rows_last(cache_fox_logf))


def _layer_sample(x, w, views, layer, s_win, s_ssd, s_conv, page_table, *, g_pages):
    cmp_t, slc_t, win_t, moba_t, fox_t, logf_t = views
    bsz = x.shape[0]
    n_pages = page_table.shape[1]
    pos0 = n_pages * PAGE
    x2 = x.reshape(bsz, D_MODEL)
    tabs = _rope_tables(jnp.full((bsz,), pos0, I32))
    (qn, cmp_rows, slc_rows, win_rows, qm, moba_rows, qf, fox_rows, z, xbc, small, dt) = _proj_call(
        x2, w['gmix'], w['wp'], w['gains'], tabs, w['biases'], bsz, 'precise')
    q8 = lambda a: _pad_rows(a[:, None, :], SUB)
    qn, qm, qf = q8(qn), q8(qm), q8(qf)
    every_head = lambda n: jnp.full((bsz, n), -1, I32)
    assert (pos0 + 1 - NSA_CMP_LEN) // NSA_CMP_STRIDE + 1 == pos0 // NSA_CMP_STRIDE - 1
    kvc = _kt_cmp_call(cmp_t, layer, page_table, *w['cmp'], g=g_pages)
    n_cmp = pos0 // NSA_CMP_STRIDE - 1
    n_sel = -(-(pos0 + 1) // NSA_SEL_BLOCK)
    nselp = -(-n_sel // LANE) * LANE
    o_cmp, _, sel_idx = _nsa_cmp_call(qn, kvc, tq=SUB, q_pos0=pos0, n_cmp=n_cmp, n_sel=n_sel,
                                      nselp=nselp, precise=True)
    k_top = min(NSA_SEL_TOPK, n_sel)
    sel_ids = sel_idx[:, 0, :k_top]
    sel_heads = jnp.where(sel_ids < pos0 // NSA_SEL_BLOCK, -1, N_HEADS)
    o_slc = _kt_attn_call(qn, slc_t, layer, page_table, sel_ids, sel_heads, q8(slc_rows),
                          r=NSA_SEL_BLOCK, g=k_top, name="nsa_slc_s")
    wb = s_win.shape[1]
    o_win = _kt_attn_call(qn, win_t, layer, jnp.arange(bsz, dtype=I32)[:, None], jnp.zeros((bsz, 1), I32),
                          every_head(1), q8(win_rows), r=wb, g=1, lo=wb - NSA_WINDOW + 1, name="nsa_win_s")
    n_blk = -(-(pos0 + 1) // MOBA_BLOCK)
    m_top = min(MOBA_TOPK, n_blk - 1)
    blk_ids = _kt_gate_call(qm, moba_t, layer, page_table, g=g_pages, k_top=m_top)[:, :N_HEADS, :m_top]
    ppb = MOBA_BLOCK // PAGE
    moba_fetch = (blk_ids[..., None] * ppb + jnp.arange(ppb, dtype=I32)).reshape(bsz, -1)
    head_id = jnp.arange(N_HEADS, dtype=I32)[None, :, None, None]
    moba_heads = jnp.broadcast_to(jnp.where(blk_ids[..., None] >= 0, head_id, N_HEADS),
                                  blk_ids.shape + (ppb,)).reshape(bsz, -1)
    o_moba = _kt_attn_call(qm, moba_t, layer, page_table, moba_fetch, moba_heads, q8(moba_rows), r=PAGE,
                           g=m_top * ppb, one_head=True, name="moba_s")
    logf_new = small[:, SM_FOXF:SM_FOXF + N_HEADS]
    fox_fetch = jnp.tile(jnp.arange(n_pages - 1, -1, -1, dtype=I32)[None, :], (bsz, 1))
    o_fox = _kt_attn_call(qf, fox_t, layer, page_table, fox_fetch, every_head(n_pages), q8(fox_rows),
                          r=PAGE, g=g_pages, logf_t=logf_t, logf_new=q8(small), name="fox_s")
    full = jnp.concatenate([s_conv, xbc[:, None, :]], axis=1)
    y_ssd, ssd_state = _ssd_step_call(_pad_rows(full, SUB), z[:, None, :], dt[:, None, :], s_ssd,
                                      w['w8'], w['convb'], w['hp'], w['norm'])
    first = lambda a: a[:, 0, :]
    x1, xn2, comb = _post_call(x2, w['gmix'], w['wgate'], small, first(o_cmp), first(o_slc), first(o_win),
                               first(o_moba), first(o_fox), first(y_ssd), w['wbr'], w['wout'],
                               w['gffn'], w['wr'], w['br'], bsz, 'precise', 1)
    y = _moe_call(xn2, x1, comb, w['weg'], w['weu'], w['wed'], bsz, precise=True)
    state = (cmp_rows.reshape(bsz, 1, 2, HEAD_DIM), slc_rows.reshape(bsz, 1, 2, HEAD_DIM),
             jnp.concatenate([s_win[:, 1:], win_rows.reshape(bsz, 1, 2, HEAD_DIM)], axis=1),
             moba_rows.reshape(bsz, 1, 2, N_HEADS, HEAD_DIM), fox_rows.reshape(bsz, 1, 2, N_HEADS, HEAD_DIM),
             logf_new[:, None, :], ssd_state, full[:, 1:])
    return y.reshape(bsz, 1, D_MODEL), state


_PARAM_NAMES = ('norm_mix', 'norm_ffn', 'w_in', 'g_nsa_q', 'g_nsa_k', 'g_nsa_kc', 'pe_cmp_k', 'pe_cmp_v',
                'w_cmp_k1', 'w_cmp_k2', 'w_cmp_v1', 'w_cmp_v2', 'g_moba_q', 'g_moba_k', 'g_fox_q',
                'g_fox_k', 'b_fox_f', 'ssd_conv_w', 'ssd_conv_b', 'ssd_dt_bias', 'ssd_a_log', 'ssd_d',
                'ssd_norm', 'w_br_nsa', 'w_br_moba', 'w_br_fox', 'w_br_ssd', 'w_out', 'w_router_grp',
                'b_router_grp', 'w_router_exp', 'b_router_exp', 'w_exp_gate', 'w_exp_up', 'w_exp_down')


def kernel(x_prompt, x_sample, cache_nsa_cmp, cache_nsa_slc, state_nsa_win, cache_moba, cache_fox, cache_fox_logf, state_ssd, state_ssd_conv, page_table, norm_mix, norm_ffn, w_in, g_nsa_q, g_nsa_k, g_nsa_kc, pe_cmp_k, pe_cmp_v, w_cmp_k1, w_cmp_k2, w_cmp_v1, w_cmp_v2, g_moba_q, g_moba_k, g_fox_q, g_fox_k, b_fox_f, ssd_conv_w, ssd_conv_b, ssd_dt_bias, ssd_a_log, ssd_d, ssd_norm, w_br_nsa, w_br_moba, w_br_fox, w_br_ssd, w_out, w_router_grp, b_router_grp, w_router_exp, b_router_exp, w_exp_gate, w_exp_up, w_exp_down):
    params = dict(zip(_PARAM_NAMES, (
        norm_mix, norm_ffn, w_in, g_nsa_q, g_nsa_k, g_nsa_kc, pe_cmp_k, pe_cmp_v, w_cmp_k1, w_cmp_k2,
        w_cmp_v1, w_cmp_v2, g_moba_q, g_moba_k, g_fox_q, g_fox_k, b_fox_f, ssd_conv_w, ssd_conv_b,
        ssd_dt_bias, ssd_a_log, ssd_d, ssd_norm, w_br_nsa, w_br_moba, w_br_fox, w_br_ssd, w_out,
        w_router_grp, b_router_grp, w_router_exp, b_router_exp, w_exp_gate, w_exp_up, w_exp_down)))
    depth = norm_mix.shape[0]
    views = _cache_views(cache_nsa_cmp, cache_nsa_slc, state_nsa_win, cache_moba, cache_fox, cache_fox_logf)
    xp, xs = x_prompt, x_sample
    sp, ss = [], []
    for l in range(depth):
        w_fast, w_exact = _layer_weights({k: v[l] for k, v in params.items()})
        feeds_next = l < depth - 1
        xp, st_p = _layer_prompt(xp, w_exact if feeds_next else w_fast, w_fast, tm=256, tq=512, tk=512,
                                 mode='tail' if feeds_next else 'fast')
        xs, st_s = _layer_sample(xs, w_exact, views, l, state_nsa_win[l], state_ssd[l], state_ssd_conv[l],
                                 page_table, g_pages=16)
        sp.append(st_p)
        ss.append(st_s)
    outs = [xp, xs]
    for i in range(8):
        outs.append(jnp.stack([s[i] for s in sp], axis=0))
        outs.append(jnp.stack([s[i] for s in ss], axis=0))
    return tuple(outs)
```
